```python
import math
import jax, jax.numpy as jnp
from jax import lax
import numpy as np

D_MODEL = 1024
BATCH = 4
SEQ = 8192
DEPTH = 1

D_RNN = D_MODEL
RNN_HEADS = 16
RNN_BLOCK = D_RNN // RNN_HEADS
C_RGLRU = 8.0
LRU_CONV = 4
D_HYENA = D_MODEL
HY_CONV = 3
HY_EMB = 33
HY_BANDS = (HY_EMB - 1) // 2
HY_FF = 64
HY_FAST_DECAY = 0.3
HY_SLOW_DECAY = 1.5
HY_TARGET = 1e-2
N_IN = 2 * D_RNN + 3 * D_HYENA + 2 * D_MODEL
N_EXPERTS = 32
TOP_K = 4
D_FF_EXPERT = D_MODEL
SWIGLU_LIMIT = 7.0
SWIGLU_ALPHA = 1.702
MOE_BLOCK = 512
EPS = 1e-5

kernel_name = 'hybrid_rglru_hyena_moe_encoder'


def rmsnorm(x, g):
    xf = x.astype(jnp.float32)
    y = xf * lax.rsqrt(jnp.mean(xf * xf, axis=-1, keepdims=True) + EPS)
    return (y * g.astype(jnp.float32)).astype(x.dtype)


def depthwise_conv(x, w, b, pad_left):
    width = w.shape[0]
    L = x.shape[1]
    xp = jnp.pad(x, ((0, 0), (pad_left, width - 1 - pad_left), (0, 0)))
    out = b
    for k in range(width):
        out = out + w[k] * xp[:, k:k + L]
    return out


def linear_scan(a, b):
    def combine(left, right):
        a_l, b_l = left
        a_r, b_r = right
        return a_l * a_r, a_r * b_l + b_r
    _, h = lax.associative_scan(combine, (a, b), axis=1)
    return h


def rglru_direction(xr, wa, ba, wi, bi, lam):
    B, L, _ = xr.shape
    xh = xr.reshape(B, L, RNN_HEADS, RNN_BLOCK)
    r = jax.nn.sigmoid(jnp.einsum('blhi,hij->blhj', xh, wa).reshape(B, L, D_RNN) + ba)
    i = jax.nn.sigmoid(jnp.einsum('blhi,hij->blhj', xh, wi).reshape(B, L, D_RNN) + bi)
    log_a = -C_RGLRU * r * jax.nn.softplus(-lam)
    a = jnp.exp(log_a)
    mult = jnp.sqrt(-jnp.expm1(2.0 * log_a))
    return linear_scan(a, mult * (i * xr))


def rglru_bidirectional(xr, wa, ba, wi, bi, lam):
    xf = xr.astype(jnp.float32)
    h_fwd = rglru_direction(xf, wa[0], ba[0], wi[0], bi[0], lam[0])
    h_bwd = jnp.flip(rglru_direction(jnp.flip(xf, 1), wa[1], ba[1], wi[1], bi[1], lam[1]), 1)
    return (h_fwd + h_bwd).astype(xr.dtype)


def hyena_filter(L, w1, b1, w2, b2, w3, b3, w4, freq):
    f32 = jnp.float32
    t = jnp.linspace(0.0, 1.0, L, dtype=f32)[:, None]
    w = 2.0 * math.pi * jnp.arange(L, dtype=f32)[:, None] / L
    bands = jnp.linspace(1e-4, HY_BANDS - 1, HY_BANDS, dtype=f32)[None, :]
    feats = jnp.concatenate([t, jnp.cos(w * bands), -jnp.sin(w * bands)], axis=-1)
    fr = freq.astype(f32)
    hh = jnp.sin(fr * (feats @ w1.astype(f32) + b1.astype(f32)))
    hh = jnp.sin(fr * (hh @ w2.astype(f32) + b2.astype(f32)))
    hh = jnp.sin(fr * (hh @ w3.astype(f32) + b3.astype(f32)))
    k = (hh @ w4.astype(f32)).reshape(L, 2, D_HYENA)
    deltas = jnp.linspace(math.log(HY_TARGET) / HY_SLOW_DECAY, math.log(HY_TARGET) / HY_FAST_DECAY, D_HYENA, dtype=f32)
    k = k * jnp.exp(-t * jnp.abs(deltas))[:, None, :]
    k_fwd = k[:, 0].at[0].add(k[0, 1])
    k_bwd = k[1:, 1]
    k_circ = jnp.concatenate([k_fwd, jnp.zeros((1, D_HYENA), f32), k_bwd[::-1]], axis=0)
    return k_circ / jnp.sum(jnp.abs(k_circ), axis=0, keepdims=True)


def hyena_mixer(p_hy, conv_w, conv_b, w1, b1, w2, b2, w3, b3, w4, freq, bias):
    B, L, _ = p_hy.shape
    q = depthwise_conv(p_hy, conv_w, conv_b, 1)
    v = q[..., :D_HYENA]
    x1 = q[..., D_HYENA:2 * D_HYENA]
    x2 = q[..., 2 * D_HYENA:]
    z = (x2 * v).astype(jnp.float32)
    k_circ = hyena_filter(L, w1, b1, w2, b2, w3, b3, w4, freq)
    K = jnp.fft.rfft(k_circ, n=2 * L, axis=0)
    Z = jnp.fft.rfft(z, n=2 * L, axis=1)
    conv = jnp.fft.irfft(Z * K[None], n=2 * L, axis=1)[:, :L]
    return x1 * (conv + bias.astype(jnp.float32) * z).astype(p_hy.dtype)


def moe_ffn(h, w_router, b_router, w_gate_up, b_gate_up, w_down, b_down):
    B, L, D = h.shape
    T = B * L
    N = T * TOP_K
    hf = h.reshape(T, D)
    logits = (hf @ w_router + b_router).astype(jnp.float32)
    top_vals, top_idx = lax.top_k(logits, TOP_K)
    gates = jax.nn.softmax(top_vals, axis=-1).astype(h.dtype)
    e_flat = top_idx.reshape(N)
    tok_flat = jnp.arange(N, dtype=jnp.int32) // TOP_K
    g_flat = gates.reshape(N)
    order = jnp.argsort(e_flat)
    e_sorted = e_flat[order]
    counts = jnp.zeros((N_EXPERTS,), jnp.int32).at[e_flat].add(1)
    starts = jnp.cumsum(counts) - counts
    padded = ((counts + MOE_BLOCK - 1) // MOE_BLOCK) * MOE_BLOCK
    pad_ends = jnp.cumsum(padded)
    pad_starts = pad_ends - padded
    dest = pad_starts[e_sorted] + jnp.arange(N, dtype=jnp.int32) - starts[e_sorted]
    n_blocks = (N + MOE_BLOCK - 1) // MOE_BLOCK + N_EXPERTS
    n_slots = n_blocks * MOE_BLOCK
    slot_tok = jnp.full((n_slots,), T, jnp.int32).at[dest].set(tok_flat[order])
    slot_gate = jnp.zeros((n_slots,), h.dtype).at[dest].set(g_flat[order])
    block_e = jnp.minimum(jnp.searchsorted(pad_ends, jnp.arange(n_blocks, dtype=jnp.int32) * MOE_BLOCK, side='right'), N_EXPERTS - 1)
    h_pad = jnp.concatenate([hf, jnp.zeros((1, D), hf.dtype)], axis=0)

    def run_block(args):
        tok, gate, e = args
        xb = h_pad[tok]
        gu = xb @ w_gate_up[e] + b_gate_up[e]
        g_ = jnp.minimum(gu[:, :D_FF_EXPERT], SWIGLU_LIMIT)
        u_ = jnp.clip(gu[:, D_FF_EXPERT:], -SWIGLU_LIMIT, SWIGLU_LIMIT)
        act = (u_ + 1.0) * (g_ * jax.nn.sigmoid(SWIGLU_ALPHA * g_))
        return (act @ w_down[e] + b_down[e]) * gate[:, None]

    yb = lax.map(run_block, (slot_tok.reshape(n_blocks, MOE_BLOCK), slot_gate.reshape(n_blocks, MOE_BLOCK), block_e))
    y = jax.ops.segment_sum(yb.reshape(n_slots, D), slot_tok, num_segments=T + 1)[:T]
    return y.reshape(B, L, D)


def setup_inputs(seed: int = 0) -> dict:
    key = jax.random.key(seed)
    ks = jax.random.split(key, 40)
    f32 = jnp.float32

    def nrm(k, shape, scale):
        return jax.random.normal(k, shape, f32) * scale

    a_c = jax.random.uniform(ks[9], (2, D_RNN), f32, minval=0.9, maxval=0.999)
    a0 = a_c ** (1.0 / C_RGLRU)
    lru_lambda = jnp.log(a0) - jnp.log1p(-a0)
    return {
        'x': nrm(ks[0], (BATCH, SEQ, D_MODEL), 1.0),
        'g_mix': 1.0 + nrm(ks[1], (D_MODEL,), 0.01),
        'w_in': nrm(ks[2], (D_MODEL, N_IN), D_MODEL ** -0.5),
        'b_in': nrm(ks[3], (N_IN,), 0.01),
        'lru_conv_w': nrm(ks[4], (LRU_CONV, D_RNN), LRU_CONV ** -0.5),
        'lru_conv_b': nrm(ks[5], (D_RNN,), 0.01),
        'lru_wa': nrm(ks[6], (2, RNN_HEADS, RNN_BLOCK, RNN_BLOCK), RNN_BLOCK ** -0.5),
        'lru_ba': nrm(ks[7], (2, D_RNN), 0.01),
        'lru_wi': nrm(ks[8], (2, RNN_HEADS, RNN_BLOCK, RNN_BLOCK), RNN_BLOCK ** -0.5),
        'lru_bi': nrm(ks[10], (2, D_RNN), 0.01),
        'lru_lambda': lru_lambda,
        'w_lru_br': nrm(ks[11], (D_RNN, D_MODEL), D_RNN ** -0.5),
        'hy_conv_w': nrm(ks[12], (HY_CONV, 3 * D_HYENA), HY_CONV ** -0.5),
        'hy_conv_b': nrm(ks[13], (3 * D_HYENA,), 0.01),
        'hy_w1': nrm(ks[14], (HY_EMB, HY_FF), HY_EMB ** -0.5),
        'hy_b1': nrm(ks[15], (HY_FF,), 0.1),
        'hy_w2': nrm(ks[16], (HY_FF, HY_FF), HY_FF ** -0.5),
        'hy_b2': nrm(ks[17], (HY_FF,), 0.1),
        'hy_w3': nrm(ks[18], (HY_FF, HY_FF), HY_FF ** -0.5),
        'hy_b3': nrm(ks[19], (HY_FF,), 0.1),
        'hy_w4': nrm(ks[20], (HY_FF, 2 * D_HYENA), HY_FF ** -0.5),
        'hy_sin_freq': 1.0 + nrm(ks[21], (HY_FF,), 0.01),
        'hy_bias': nrm(ks[22], (D_HYENA,), 1.0),
        'w_hy_br': nrm(ks[23], (D_HYENA, D_MODEL), D_HYENA ** -0.5),
        'w_out': nrm(ks[24], (D_MODEL, D_MODEL), D_MODEL ** -0.5),
        'g_ffn': 1.0 + nrm(ks[25], (D_MODEL,), 0.01),
        'w_router': nrm(ks[26], (D_MODEL, N_EXPERTS), D_MODEL ** -0.5),
        'b_router': nrm(ks[27], (N_EXPERTS,), 0.01),
        'w_gate_up': nrm(ks[28], (N_EXPERTS, D_MODEL, 2 * D_FF_EXPERT), D_MODEL ** -0.5),
        'b_gate_up': nrm(ks[29], (N_EXPERTS, 2 * D_FF_EXPERT), 0.01),
        'w_down': nrm(ks[30], (N_EXPERTS, D_FF_EXPERT, D_MODEL), D_FF_EXPERT ** -0.5),
        'b_down': nrm(ks[31], (N_EXPERTS, D_MODEL), 0.01),
        'g_final': 1.0 + nrm(ks[32], (D_MODEL,), 0.01),
    }


def reference(x, g_mix, w_in, b_in, lru_conv_w, lru_conv_b, lru_wa, lru_ba, lru_wi, lru_bi, lru_lambda, w_lru_br, hy_conv_w, hy_conv_b, hy_w1, hy_b1, hy_w2, hy_b2, hy_w3, hy_b3, hy_w4, hy_sin_freq, hy_bias, w_hy_br, w_out, g_ffn, w_router, b_router, w_gate_up, b_gate_up, w_down, b_down, g_final):
    h = x
    o1 = D_RNN
    o2 = 2 * D_RNN
    o3 = o2 + 3 * D_HYENA
    o4 = o3 + D_MODEL
    for _ in range(DEPTH):
        u = rmsnorm(h, g_mix)
        p = u @ w_in + b_in
        xr = depthwise_conv(p[..., :o1], lru_conv_w, lru_conv_b, LRU_CONV // 2)
        h_rnn = rglru_bidirectional(xr, lru_wa, lru_ba, lru_wi, lru_bi, lru_lambda)
        y_lru = jax.nn.gelu(p[..., o1:o2]) * h_rnn
        y_hy = hyena_mixer(p[..., o2:o3], hy_conv_w, hy_conv_b, hy_w1, hy_b1, hy_w2, hy_b2, hy_w3, hy_b3, hy_w4, hy_sin_freq, hy_bias)
        gate_lru = jax.nn.sigmoid(p[..., o3:o4])
        gate_hy = jax.nn.sigmoid(p[..., o4:])
        mix = gate_lru * (y_lru @ w_lru_br) + gate_hy * (y_hy @ w_hy_br)
        h = h + mix @ w_out
        h = h + moe_ffn(rmsnorm(h, g_ffn), w_router, b_router, w_gate_up, b_gate_up, w_down, b_down)
    return rmsnorm(h, g_final)
```

```python
import functools
import math

import jax
import jax.numpy as jnp
from jax import lax
from jax.experimental import pallas as pl
from jax.experimental.pallas import tpu as pltpu

F32 = jnp.float32
BF16 = jnp.bfloat16

EPS = 1e-5
C_RGLRU = 8.0
RNN_BLOCK = 64
HY_EMB = 33
HY_BANDS = 16
HY_FAST_DECAY = 0.3
HY_SLOW_DECAY = 1.5
HY_TARGET = 1e-2
N_EXPERTS = 32
TOP_K = 4
SWIGLU_LIMIT = 7.0
SWIGLU_ALPHA = 1.702
MOE_BLOCK = 512

LANES = 128
SUBLANES = 8
FFT_N2 = 128
VMEM_LIMIT = 56 * 1024 * 1024


def _cparams(sem):
    return pltpu.CompilerParams(dimension_semantics=sem, vmem_limit_bytes=VMEM_LIMIT)


def _inproj_kernel(x_ref, g_ref, w_ref, b_ref, o_ref, u_ref):
    j = pl.program_id(1)

    @pl.when(j == 0)
    def _():
        x = x_ref[...]
        ms = jnp.mean(x * x, axis=-1, keepdims=True)
        u_ref[...] = (x * lax.rsqrt(ms + EPS) * g_ref[...]).astype(BF16)

    acc = jnp.dot(u_ref[...], w_ref[...], preferred_element_type=F32) + b_ref[...]

    @pl.when(j == 1)
    def _():
        o_ref[...] = jax.nn.gelu(acc)

    @pl.when(j >= 5)
    def _():
        o_ref[...] = jax.nn.sigmoid(acc)

    @pl.when(jnp.logical_and(j != 1, j < 5))
    def _():
        o_ref[...] = acc


def _inproj(x2, g_mix, w_in, b_in, tm):
    T, D = x2.shape
    n_in = w_in.shape[1]
    return pl.pallas_call(
        _inproj_kernel,
        grid=(T // tm, n_in // D),
        in_specs=[
            pl.BlockSpec((tm, D), lambda i, j: (i, 0)),
            pl.BlockSpec((1, D), lambda i, j: (0, 0)),
            pl.BlockSpec((D, D), lambda i, j: (0, j)),
            pl.BlockSpec((1, D), lambda i, j: (0, j)),
        ],
        out_specs=pl.BlockSpec((tm, D), lambda i, j: (i, j)),
        out_shape=jax.ShapeDtypeStruct((T, n_in), F32),
        scratch_shapes=[pltpu.VMEM((tm, D), BF16)],
        compiler_params=_cparams(("parallel", "arbitrary")),
    )(x2, g_mix.reshape(1, D), w_in.astype(BF16), b_in.reshape(1, n_in))


def _lru_kernel(x_ref, g_ref, cw_ref, cb_ref, wg_ref, bg_ref, lam_ref, o_ref,
                xp_ref, hb_ref, af_ref, bf_ref, ab_ref, bb_ref, *, L, R):
    nchunk = L // R
    ng = R // SUBLANES
    zeros8 = jnp.zeros((SUBLANES, LANES), F32)
    xp_ref[0:SUBLANES, :] = zeros8
    xp_ref[SUBLANES + L:2 * SUBLANES + L, :] = zeros8
    xp_ref[SUBLANES:SUBLANES + L, :] = x_ref[...]

    sub = lax.broadcasted_iota(jnp.int32, (R, LANES), 0) % SUBLANES
    c8 = [-C_RGLRU * jax.nn.softplus(-lam_ref[d]) for d in range(2)]
    cb = cb_ref[...]
    cw = [cw_ref[k:k + 1, :] for k in range(4)]

    def prep(t0, d):
        xr = cb
        for k in range(4):
            xr = xr + cw[k] * xp_ref[pl.ds(t0 + (SUBLANES - 2 + k), R), :]
        gz = jnp.dot(xr.astype(BF16), wg_ref[d], preferred_element_type=F32) + bg_ref[d]
        r = jax.nn.sigmoid(gz[:, :LANES])
        i = jax.nn.sigmoid(gz[:, LANES:])
        log_a = c8[d] * r
        a = jnp.exp(log_a)
        b = jnp.sqrt((1.0 + a * a) * jnp.tanh(-log_a)) * (i * xr)
        return a, b

    def chunk(c, carry):
        t0f = pl.multiple_of(c * R, R)
        t0b = pl.multiple_of((nchunk - 1 - c) * R, R)
        a, b = prep(t0f, 0)
        for s in (1, 2, 4):
            a_sh = pltpu.roll(a, s, 0)
            b_sh = pltpu.roll(b, s, 0)
            m = sub >= s
            b = jnp.where(m, a * b_sh + b, b)
            a = jnp.where(m, a * a_sh, a)
        af_ref[...] = a
        bf_ref[...] = b
        a, b = prep(t0b, 1)
        for s in (1, 2, 4):
            a_sh = pltpu.roll(a, R - s, 0)
            b_sh = pltpu.roll(b, R - s, 0)
            m = sub < SUBLANES - s
            b = jnp.where(m, a * b_sh + b, b)
            a = jnp.where(m, a * a_sh, a)
        ab_ref[...] = a
        bb_ref[...] = b

        def group(j, hc):
            hf, hk = hc
            rf = pl.multiple_of(j * SUBLANES, SUBLANES)
            h8 = bf_ref[pl.ds(rf, SUBLANES), :] + af_ref[pl.ds(rf, SUBLANES), :] * hf
            o_ref[pl.ds(t0f + rf, SUBLANES), :] = h8
            hf = jnp.broadcast_to(h8[SUBLANES - 1:SUBLANES, :], (SUBLANES, LANES))
            rb = pl.multiple_of((ng - 1 - j) * SUBLANES, SUBLANES)
            k8 = bb_ref[pl.ds(rb, SUBLANES), :] + ab_ref[pl.ds(rb, SUBLANES), :] * hk
            hb_ref[pl.ds(t0b + rb, SUBLANES), :] = k8
            hk = jnp.broadcast_to(k8[0:1, :], (SUBLANES, LANES))
            return hf, hk

        return lax.fori_loop(0, ng, group, carry, unroll=8)

    lax.fori_loop(0, nchunk, chunk, (zeros8, zeros8))

    def fin(c, _):
        t0 = pl.multiple_of(c * R, R)
        o_ref[pl.ds(t0, R), :] = g_ref[pl.ds(t0, R), :] * (o_ref[pl.ds(t0, R), :] + hb_ref[pl.ds(t0, R), :])
        return 0

    lax.fori_loop(0, nchunk, fin, 0)


def _lru(p3, conv_w, conv_b, wa, ba, wi, bi, lam, R):
    B, L, _ = p3.shape
    D = conv_w.shape[1]
    nt = D // LANES
    hpt = LANES // RNN_BLOCK

    def blockdiag(w):
        w = w.reshape(2, nt, hpt, RNN_BLOCK, RNN_BLOCK)
        eye = jnp.eye(hpt, dtype=w.dtype)
        return jnp.einsum('dthij,hg->dthigj', w, eye).reshape(2, nt, LANES, LANES)

    wg = jnp.concatenate([blockdiag(wa), blockdiag(wi)], axis=-1).astype(BF16)
    bg = jnp.concatenate([ba.reshape(2, nt, 1, LANES), bi.reshape(2, nt, 1, LANES)], axis=-1)
    lam4 = lam.reshape(2, nt, 1, LANES)
    return pl.pallas_call(
        functools.partial(_lru_kernel, L=L, R=R),
        grid=(B, nt),
        in_specs=[
            pl.BlockSpec((None, L, LANES), lambda b, c: (b, 0, c)),
            pl.BlockSpec((None, L, LANES), lambda b, c: (b, 0, nt + c)),
            pl.BlockSpec((4, LANES), lambda b, c: (0, c)),
            pl.BlockSpec((1, LANES), lambda b, c: (0, c)),
            pl.BlockSpec((2, None, LANES, 2 * LANES), lambda b, c: (0, c, 0, 0)),
            pl.BlockSpec((2, None, 1, 2 * LANES), lambda b, c: (0, c, 0, 0)),
            pl.BlockSpec((2, None, 1, LANES), lambda b, c: (0, c, 0, 0)),
        ],
        out_specs=pl.BlockSpec((None, L, LANES), lambda b, c: (b, 0, c)),
        out_shape=jax.ShapeDtypeStruct((B, L, D), F32),
        scratch_shapes=[
            pltpu.VMEM((L + 2 * SUBLANES, LANES), F32),
            pltpu.VMEM((L, LANES), F32),
            pltpu.VMEM((R, LANES), F32), pltpu.VMEM((R, LANES), F32),
            pltpu.VMEM((R, LANES), F32), pltpu.VMEM((R, LANES), F32),
        ],
        compiler_params=_cparams(("parallel", "parallel")),
    )(p3, p3, conv_w, conv_b.reshape(1, D), wg, bg, lam4)


def _hy_pre_kernel(v_ref, x1_ref, x2_ref, vp_ref, x1p_ref, x2p_ref, vn_ref, x1n_ref, x2n_ref,
                   w_ref, b_ref, z_ref, x1c_ref, *, nt):
    i = pl.program_id(1)
    tl = v_ref.shape[0]
    row = lax.broadcasted_iota(jnp.int32, v_ref.shape, 0)
    first = i == 0
    last = i == nt - 1

    def conv(c, m_ref, p_ref, n_ref):
        x = m_ref[...]
        prev = jnp.where(first, 0.0, p_ref[SUBLANES - 1:SUBLANES, :])
        nxt = jnp.where(last, 0.0, n_ref[0:1, :])
        xm = jnp.where(row == 0, prev, pltpu.roll(x, 1, 0))
        xn = jnp.where(row == tl - 1, nxt, pltpu.roll(x, tl - 1, 0))
        return b_ref[c] + w_ref[c, 0:1, :] * xm + w_ref[c, 1:2, :] * x + w_ref[c, 2:3, :] * xn

    v = conv(0, v_ref, vp_ref, vn_ref)
    x1c_ref[...] = conv(1, x1_ref, x1p_ref, x1n_ref)
    z_ref[...] = conv(2, x2_ref, x2p_ref, x2n_ref) * v


def _hy_pre(p3, conv_w, conv_b, D, col0, tl, ct):
    B, L, n_in = p3.shape
    nt = L // tl
    ncb = D // ct
    g8 = tl // SUBLANES
    p4 = p3.reshape(B, L // SUBLANES, SUBLANES, n_in)
    w3 = conv_w.reshape(3, 3, D).transpose(1, 0, 2)
    b3 = conv_b.reshape(3, 1, D)
    cb0 = col0 // ct

    def main(part):
        return pl.BlockSpec((None, tl, ct), lambda b, i, c: (b, i, cb0 + part * ncb + c))

    def prev(part):
        return pl.BlockSpec((None, None, SUBLANES, ct),
                            lambda b, i, c: (b, jnp.maximum(i * g8 - 1, 0), 0, cb0 + part * ncb + c))

    def nxt(part):
        return pl.BlockSpec((None, None, SUBLANES, ct),
                            lambda b, i, c: (b, jnp.minimum((i + 1) * g8, L // SUBLANES - 1), 0, cb0 + part * ncb + c))

    out = pl.BlockSpec((None, tl, ct), lambda b, i, c: (b, i, c))
    return pl.pallas_call(
        functools.partial(_hy_pre_kernel, nt=nt),
        grid=(B, nt, ncb),
        in_specs=[main(0), main(1), main(2), prev(0), prev(1), prev(2), nxt(0), nxt(1), nxt(2),
                  pl.BlockSpec((3, 3, ct), lambda b, i, c: (0, 0, c)),
                  pl.BlockSpec((3, 1, ct), lambda b, i, c: (0, 0, c))],
        out_specs=[out, out],
        out_shape=[jax.ShapeDtypeStruct((B, L, D), F32)] * 2,
        compiler_params=_cparams(("parallel", "parallel", "parallel")),
    )(p3, p3, p3, p4, p4, p4, p4, p4, p4, w3, b3)


def _filt_kernel(feat_ref, t_ref, w1_ref, b1_ref, w2_ref, b2_ref, w3_ref, b3_ref, w4_ref, fr_ref, dl_ref,
                 kf_ref, kb_ref, s_ref):
    i = pl.program_id(0)
    D = kf_ref.shape[1]
    hp = lax.Precision.HIGHEST
    fr = fr_ref[...]
    hh = jnp.sin(fr * (jnp.dot(feat_ref[...], w1_ref[...], precision=hp, preferred_element_type=F32) + b1_ref[...]))
    hh = jnp.sin(fr * (jnp.dot(hh, w2_ref[...], precision=hp, preferred_element_type=F32) + b2_ref[...]))
    hh = jnp.sin(fr * (jnp.dot(hh, w3_ref[...], precision=hp, preferred_element_type=F32) + b3_ref[...]))
    k = jnp.dot(hh, w4_ref[...], precision=hp, preferred_element_type=F32)
    decay = jnp.exp(-t_ref[...] * jnp.abs(dl_ref[...]))
    kf = k[:, :D] * decay
    kb = k[:, D:] * decay
    row = lax.broadcasted_iota(jnp.int32, kf.shape, 0)
    lag0 = jnp.logical_and(row == 0, i == 0)
    kf = jnp.where(lag0, kf + kb, kf)
    kb = jnp.where(lag0, 0.0, kb)
    kf_ref[...] = kf
    kb_ref[...] = kb
    part = jnp.sum(jnp.abs(kf) + jnp.abs(kb), axis=0, keepdims=True)

    @pl.when(i == 0)
    def _():
        s_ref[...] = part

    @pl.when(i > 0)
    def _():
        s_ref[...] += part


def _hyena_filter_parts(L, D, w1, b1, w2, b2, w3, b3, w4, freq, tl):
    t = jnp.linspace(0.0, 1.0, L, dtype=F32)[:, None]
    w = 2.0 * math.pi * jnp.arange(L, dtype=F32)[:, None] / L
    bands = jnp.linspace(1e-4, HY_BANDS - 1, HY_BANDS, dtype=F32)[None, :]
    feats = jnp.concatenate([t, jnp.cos(w * bands), -jnp.sin(w * bands)], axis=-1)
    deltas = jnp.linspace(math.log(HY_TARGET) / HY_SLOW_DECAY, math.log(HY_TARGET) / HY_FAST_DECAY, D, dtype=F32)
    ff = w1.shape[1]
    full = lambda shape: pl.BlockSpec(shape, lambda i: (0,) * len(shape))
    return pl.pallas_call(
        _filt_kernel,
        grid=(L // tl,),
        in_specs=[pl.BlockSpec((tl, HY_EMB), lambda i: (i, 0)), pl.BlockSpec((tl, 1), lambda i: (i, 0)),
                  full((HY_EMB, ff)), full((1, ff)), full((ff, ff)), full((1, ff)), full((ff, ff)), full((1, ff)),
                  full((ff, 2 * D)), full((1, ff)), full((1, D))],
        out_specs=[pl.BlockSpec((tl, D), lambda i: (i, 0)), pl.BlockSpec((tl, D), lambda i: (i, 0)),
                   pl.BlockSpec((1, D), lambda i: (0, 0))],
        out_shape=[jax.ShapeDtypeStruct((L, D), F32), jax.ShapeDtypeStruct((L, D), F32),
                   jax.ShapeDtypeStruct((1, D), F32)],
        compiler_params=_cparams(("arbitrary",)),
    )(feats, t, w1, b1.reshape(1, ff), w2, b2.reshape(1, ff), w3, b3.reshape(1, ff), w4,
      freq.reshape(1, ff), deltas.reshape(1, D))


def _angles(rows, cols, n):
    r = (rows * cols) % n
    th = r.astype(F32) * (2.0 * math.pi / n)
    return jnp.cos(th), jnp.sin(th)


def _dft_mats(N1, N2):
    N = N1 * N2
    h = N1 // 2
    ar = lambda n: jnp.arange(n, dtype=jnp.int32)
    k1 = ar(N1)[None, :, None]
    tn = ar(h)[None, None, :] * N2 + ar(N2)[:, None, None]
    c, s = _angles(k1, tn, N)
    a_data = jnp.concatenate([jnp.concatenate([c, s], -1), jnp.concatenate([-s, c], -1)], 1)
    tn = ar(N1)[None, None, :] * N2 + ar(N2)[:, None, None]
    c, s = _angles(k1, tn, N)
    a_filt = jnp.concatenate([c, -s], 1)
    c, s = _angles(ar(N2)[:, None], ar(N2)[None, :], N2)
    m2 = jnp.concatenate([jnp.concatenate([c, s], -1), jnp.concatenate([-s, c], -1)], 0)
    kk = ar(N1)[:, None, None] + N1 * ar(N2)[None, None, :]
    c, s = _angles(ar(N2)[None, :, None], kk, N)
    hm = jnp.concatenate([jnp.concatenate([c, -s], -1), jnp.concatenate([s, c], -1)], 1)
    c, s = _angles(ar(h)[:, None], ar(N1)[None, :], N1)
    i2 = jnp.concatenate([jnp.concatenate([c, -s], -1), jnp.concatenate([s, c], -1)], 0) / N
    return (a_data.astype(BF16), a_filt.astype(BF16), m2.astype(BF16), hm.astype(BF16), i2.astype(BF16))


def _fft1_kernel(*refs, scaled):
    a_ref = refs[0]
    inv_s = refs[1][...] if scaled else None
    x_ref, o_ref = refs[-2], refs[-1]
    nin = x_ref.shape[0]
    n1 = o_ref.shape[1]
    for j in range(SUBLANES):
        rows = [x_ref[q, :, j, :] for q in range(nin)]
        x = rows[0] if nin == 1 else jnp.concatenate(rows, axis=0)
        if scaled:
            x = x * inv_s
        y = jnp.dot(a_ref[j], x.astype(BF16), preferred_element_type=F32)
        o_ref[0, :, j, :] = y[:n1]
        o_ref[1, :, j, :] = y[n1:]


def _fft1_data(z5, a_data, dt):
    _, P, h, N2, D = z5.shape
    N1 = 2 * h
    return pl.pallas_call(
        functools.partial(_fft1_kernel, scaled=False),
        grid=(N2 // SUBLANES, P, D // dt),
        in_specs=[pl.BlockSpec((SUBLANES, 2 * N1, N1), lambda j, p, c: (j, 0, 0)),
                  pl.BlockSpec((2, None, h, SUBLANES, dt), lambda j, p, c: (0, p, 0, j, c))],
        out_specs=pl.BlockSpec((None, 2, N1, SUBLANES, dt), lambda j, p, c: (p, 0, 0, j, c)),
        out_shape=jax.ShapeDtypeStruct((P, 2, N1, N2, D), F32),
        compiler_params=_cparams(("parallel", "parallel", "parallel")),
    )(a_data, z5)


def _fft1_filt(k4, inv_s, a_filt, dt):
    _, N1, N2, D = k4.shape
    return pl.pallas_call(
        functools.partial(_fft1_kernel, scaled=True),
        grid=(N2 // SUBLANES, D // dt),
        in_specs=[pl.BlockSpec((SUBLANES, 2 * N1, N1), lambda j, c: (j, 0, 0)),
                  pl.BlockSpec((1, dt), lambda j, c: (0, c)),
                  pl.BlockSpec((1, N1, SUBLANES, dt), lambda j, c: (0, 0, j, c))],
        out_specs=pl.BlockSpec((2, N1, SUBLANES, dt), lambda j, c: (0, 0, j, c)),
        out_shape=jax.ShapeDtypeStruct((2, N1, N2, D), F32),
        compiler_params=_cparams(("parallel", "parallel")),
    )(a_filt, inv_s, k4)


def _fft2_filt_kernel(m2_ref, y_ref, o_ref):
    n2 = y_ref.shape[1]
    y = jnp.concatenate([y_ref[0], y_ref[1]], axis=0).astype(BF16)
    x = jnp.dot(m2_ref[...], y, preferred_element_type=F32)
    o_ref[0] = x[:n2]
    o_ref[1] = x[n2:]


def _fft2_filt(yk, m2, dt):
    _, N1, N2, D = yk.shape
    return pl.pallas_call(
        _fft2_filt_kernel,
        grid=(N1, D // dt),
        in_specs=[pl.BlockSpec((2 * N2, 2 * N2), lambda k, c: (0, 0)),
                  pl.BlockSpec((2, None, N2, dt), lambda k, c: (0, k, 0, c))],
        out_specs=pl.BlockSpec((2, None, N2, dt), lambda k, c: (0, k, 0, c)),
        out_shape=jax.ShapeDtypeStruct((2, N1, N2, D), F32),
        compiler_params=_cparams(("parallel", "parallel")),
    )(m2, yk)


def _fft_mid_kernel(m2_ref, h_ref, y_ref, k_ref, o_ref):
    n2 = y_ref.shape[1]
    y = jnp.concatenate([y_ref[0], y_ref[1]], axis=0).astype(BF16)
    x = jnp.dot(m2_ref[...], y, preferred_element_type=F32)
    xr, xi = x[:n2], x[n2:]
    kr, ki = k_ref[0], k_ref[1]
    pr = xr * kr - xi * ki
    pi = xr * ki + xi * kr
    p = jnp.concatenate([pr, pi], axis=0).astype(BF16)
    u = jnp.dot(h_ref[...], p, preferred_element_type=F32)
    o_ref[0] = u[:n2]
    o_ref[1] = u[n2:]


def _fft_mid(y, kf, m2, hm, dt):
    P, _, N1, N2, D = y.shape
    return pl.pallas_call(
        _fft_mid_kernel,
        grid=(N1, D // dt, P),
        in_specs=[pl.BlockSpec((2 * N2, 2 * N2), lambda k, c, p: (0, 0)),
                  pl.BlockSpec((None, 2 * N2, 2 * N2), lambda k, c, p: (k, 0, 0)),
                  pl.BlockSpec((None, 2, None, N2, dt), lambda k, c, p: (p, 0, k, 0, c)),
                  pl.BlockSpec((2, None, N2, dt), lambda k, c, p: (0, k, 0, c))],
        out_specs=pl.BlockSpec((None, 2, None, N2, dt), lambda k, c, p: (p, 0, k, 0, c)),
        out_shape=jax.ShapeDtypeStruct((P, 2, N1, N2, D), F32),
        compiler_params=_cparams(("parallel", "parallel", "parallel")),
    )(m2, hm, y, kf)


def _fft_inv_kernel(i2_ref, u_ref, z_ref, x1_ref, bias_ref, o_ref):
    h = o_ref.shape[1]
    bias = bias_ref[...]
    for j in range(SUBLANES):
        u = jnp.concatenate([u_ref[0, :, j, :], u_ref[1, :, j, :]], axis=0).astype(BF16)
        y = jnp.dot(i2_ref[...], u, preferred_element_type=F32)
        o_ref[0, :, j, :] = x1_ref[0, :, j, :] * (y[:h] + bias * z_ref[0, :, j, :])
        o_ref[1, :, j, :] = x1_ref[1, :, j, :] * (y[h:] + bias * z_ref[1, :, j, :])


def _fft_inv(u, z5, x15, bias, i2, dt):
    P, _, N1, N2, D = u.shape
    h = N1 // 2
    pair = pl.BlockSpec((2, None, h, SUBLANES, dt), lambda j, p, c: (0, p, 0, j, c))
    return pl.pallas_call(
        _fft_inv_kernel,
        grid=(N2 // SUBLANES, P, D // dt),
        in_specs=[pl.BlockSpec((N1, 2 * N1), lambda j, p, c: (0, 0)),
                  pl.BlockSpec((None, 2, N1, SUBLANES, dt), lambda j, p, c: (p, 0, 0, j, c)),
                  pair, pair,
                  pl.BlockSpec((1, dt), lambda j, p, c: (0, c))],
        out_specs=pair,
        out_shape=jax.ShapeDtypeStruct((2, P, h, N2, D), F32),
        compiler_params=_cparams(("parallel", "parallel", "parallel")),
    )(i2, u, z5, x15, bias)


def _hyena(p3, D, col0, conv_w, conv_b, w1, b1, w2, b2, w3, b3, w4, freq, bias, tl, ct, dt):
    B, L, _ = p3.shape
    N2 = FFT_N2
    N1 = 2 * L // N2
    h = N1 // 2
    z, x1c = _hy_pre(p3, conv_w, conv_b, D, col0, tl, ct)
    kf, kb, s = _hyena_filter_parts(L, D, w1, b1, w2, b2, w3, b3, w4, freq, min(tl, 512))
    k_circ = jnp.concatenate([kf, jnp.roll(jnp.flip(kb, 0), 1, 0)], axis=0)
    a_data, a_filt, m2, hm, i2 = _dft_mats(N1, N2)
    kspec = _fft2_filt(_fft1_filt(k_circ.reshape(1, N1, N2, D), 1.0 / s, a_filt, dt), m2, dt)
    z5 = z.reshape(2, B // 2, h, N2, D)
    y = _fft1_data(z5, a_data, dt)
    u = _fft_mid(y, kspec, m2, hm, dt)
    out = _fft_inv(u, z5, x1c.reshape(2, B // 2, h, N2, D), bias.reshape(1, D), i2, dt)
    return out.reshape(B * L, D)


def _merge_kernel(x_ref, yl_ref, yh_ref, gl_ref, gh_ref, wl_ref, wh_ref, wo_ref, gf_ref, wr_ref, br_ref,
                  h_ref, hn_ref, idx_ref, gate_ref):
    a = jnp.dot(yl_ref[...].astype(BF16), wl_ref[...], preferred_element_type=F32)
    b = jnp.dot(yh_ref[...].astype(BF16), wh_ref[...], preferred_element_type=F32)
    mix = gl_ref[...] * a + gh_ref[...] * b
    h = x_ref[...] + jnp.dot(mix.astype(BF16), wo_ref[...], preferred_element_type=F32)
    h_ref[...] = h
    ms = jnp.mean(h * h, axis=-1, keepdims=True)
    hn = h * lax.rsqrt(ms + EPS) * gf_ref[...]
    hn_ref[...] = hn
    logits = jnp.dot(hn, wr_ref[...], precision=lax.Precision.HIGHEST, preferred_element_type=F32) + br_ref[...]
    ne = logits.shape[1]
    lane = lax.broadcasted_iota(jnp.int32, logits.shape, 1)
    vals = []
    for k in range(TOP_K):
        m = jnp.max(logits, axis=-1, keepdims=True)
        ix = jnp.min(jnp.where(logits == m, lane, ne), axis=-1, keepdims=True)
        vals.append(m)
        idx_ref[:, k:k + 1] = ix
        logits = jnp.where(lane == ix, -jnp.inf, logits)
    ex = [jnp.exp(v - vals[0]) for v in vals]
    tot = ex[0] + ex[1] + ex[2] + ex[3]
    for k in range(TOP_K):
        gate_ref[:, k:k + 1] = ex[k] / tot


def _merge(x2, y_lru, y_hy, p_all, w_lru_br, w_hy_br, w_out, g_ffn, w_router, b_router, tm):
    T, D = x2.shape
    ne = w_router.shape[1]
    gcol = p_all.shape[1] // D - 2
    row = lambda i: (i, 0)
    full = lambda i: (0, 0)
    return pl.pallas_call(
        _merge_kernel,
        grid=(T // tm,),
        in_specs=[pl.BlockSpec((tm, D), row), pl.BlockSpec((tm, D), row), pl.BlockSpec((tm, D), row),
                  pl.BlockSpec((tm, D), lambda i: (i, gcol)), pl.BlockSpec((tm, D), lambda i: (i, gcol + 1)),
                  pl.BlockSpec((D, D), full), pl.BlockSpec((D, D), full), pl.BlockSpec((D, D), full),
                  pl.BlockSpec((1, D), full), pl.BlockSpec((D, ne), full), pl.BlockSpec((1, ne), full)],
        out_specs=[pl.BlockSpec((tm, D), row), pl.BlockSpec((tm, D), row),
                   pl.BlockSpec((tm, TOP_K), row), pl.BlockSpec((tm, TOP_K), row)],
        out_shape=[jax.ShapeDtypeStruct((T, D), F32), jax.ShapeDtypeStruct((T, D), F32),
                   jax.ShapeDtypeStruct((T, TOP_K), jnp.int32), jax.ShapeDtypeStruct((T, TOP_K), F32)],
        compiler_params=_cparams(("parallel",)),
    )(x2, y_lru, y_hy, p_all, p_all, w_lru_br.astype(BF16), w_hy_br.astype(BF16), w_out.astype(BF16),
      g_ffn.reshape(1, D), w_router, b_router.reshape(1, ne))


def _gather_rows(idx_ref, src_hbm, dst_ref, sem, n):
    def issue(r, c):
        t = idx_ref[0, 0, r]
        pltpu.make_async_copy(src_hbm.at[pl.ds(t, 1), :], dst_ref.at[pl.ds(r, 1), :], sem).start()
        return c

    lax.fori_loop(0, n, issue, 0, unroll=8)

    def wait(r, c):
        pltpu.make_async_copy(src_hbm.at[pl.ds(0, 1), :], dst_ref.at[pl.ds(r, 1), :], sem).wait()
        return c

    lax.fori_loop(0, n, wait, 0, unroll=8)


def _moe_kernel(be_ref, nv_ref, tok_ref, hn_hbm, gate_ref, wgu_ref, bgu_ref, wd_ref, bd_ref, o_ref, xbuf, sem):
    i = pl.program_id(0)
    dff = wd_ref.shape[0]

    @pl.when(i < nv_ref[0])
    def _():
        _gather_rows(tok_ref, hn_hbm, xbuf, sem, xbuf.shape[0])
        xb = xbuf[...].astype(BF16)
        gu = jnp.dot(xb, wgu_ref[...], preferred_element_type=F32) + bgu_ref[...]
        g = jnp.minimum(gu[:, :dff], SWIGLU_LIMIT)
        u = jnp.clip(gu[:, dff:], -SWIGLU_LIMIT, SWIGLU_LIMIT)
        act = (u + 1.0) * (g * jax.nn.sigmoid(SWIGLU_ALPHA * g))
        y = jnp.dot(act.astype(BF16), wd_ref[...], preferred_element_type=F32) + bd_ref[...]
        o_ref[...] = y * gate_ref[...]

    @pl.when(i >= nv_ref[0])
    def _():
        o_ref[...] = jnp.zeros_like(o_ref)


def _moe_blocks(hn, slot_tok, slot_gate, block_e, n_valid, w_gate_up, b_gate_up, w_down, b_down):
    T, D = hn.shape
    n_blocks = block_e.shape[0]
    dff = w_down.shape[1]
    grid_spec = pltpu.PrefetchScalarGridSpec(
        num_scalar_prefetch=2,
        grid=(n_blocks,),
        in_specs=[
            pl.BlockSpec((1, 1, MOE_BLOCK), lambda i, be, nv: (i, 0, 0), memory_space=pltpu.SMEM),
            pl.BlockSpec(memory_space=pl.ANY),
            pl.BlockSpec((MOE_BLOCK, 1), lambda i, be, nv: (i, 0)),
            pl.BlockSpec((None, D, 2 * dff), lambda i, be, nv: (be[i], 0, 0)),
            pl.BlockSpec((None, 1, 2 * dff), lambda i, be, nv: (be[i], 0, 0)),
            pl.BlockSpec((None, dff, D), lambda i, be, nv: (be[i], 0, 0)),
            pl.BlockSpec((None, 1, D), lambda i, be, nv: (be[i], 0, 0)),
        ],
        out_specs=pl.BlockSpec((MOE_BLOCK, D), lambda i, be, nv: (i, 0)),
        scratch_shapes=[pltpu.VMEM((MOE_BLOCK, D), F32), pltpu.SemaphoreType.DMA(())],
    )
    return pl.pallas_call(
        _moe_kernel,
        grid_spec=grid_spec,
        out_shape=jax.ShapeDtypeStruct((n_blocks * MOE_BLOCK, D), F32),
        compiler_params=_cparams(("arbitrary",)),
    )(block_e, n_valid, slot_tok.reshape(n_blocks, 1, MOE_BLOCK), hn, slot_gate.reshape(-1, 1),
      w_gate_up.astype(BF16), b_gate_up.reshape(N_EXPERTS, 1, 2 * dff), w_down.astype(BF16),
      b_down.reshape(N_EXPERTS, 1, D))


def _combine_kernel(slot_ref, yb_hbm, h_ref, g_ref, o_ref, buf, sem):
    tq = h_ref.shape[0]
    _gather_rows(slot_ref, yb_hbm, buf, sem, buf.shape[0])
    y = buf[0:tq, :]
    for k in range(1, TOP_K):
        y = y + buf[k * tq:(k + 1) * tq, :]
    h = h_ref[...] + y
    ms = jnp.mean(h * h, axis=-1, keepdims=True)
    o_ref[...] = h * lax.rsqrt(ms + EPS) * g_ref[...]


def _combine(yb, slot_of, h, g_final, tq):
    T, D = h.shape
    nt = T // tq
    slots = slot_of.reshape(nt, tq, TOP_K).transpose(0, 2, 1).reshape(nt, 1, TOP_K * tq)
    return pl.pallas_call(
        _combine_kernel,
        grid=(nt,),
        in_specs=[pl.BlockSpec((1, 1, TOP_K * tq), lambda i: (i, 0, 0), memory_space=pltpu.SMEM),
                  pl.BlockSpec(memory_space=pl.ANY),
                  pl.BlockSpec((tq, D), lambda i: (i, 0)),
                  pl.BlockSpec((1, D), lambda i: (0, 0))],
        out_specs=pl.BlockSpec((tq, D), lambda i: (i, 0)),
        out_shape=jax.ShapeDtypeStruct((T, D), F32),
        scratch_shapes=[pltpu.VMEM((TOP_K * tq, D), F32), pltpu.SemaphoreType.DMA(())],
        compiler_params=_cparams(("arbitrary",)),
    )(slots, yb, h, g_final.reshape(1, D))


def _route(top_idx, gates):
    T = top_idx.shape[0]
    N = T * TOP_K
    e_flat = top_idx.reshape(N)
    g_flat = gates.reshape(N)
    order = jnp.argsort(e_flat)
    e_sorted = e_flat[order]
    counts = jnp.zeros((N_EXPERTS,), jnp.int32).at[e_flat].add(1)
    starts = jnp.cumsum(counts) - counts
    padded = ((counts + MOE_BLOCK - 1) // MOE_BLOCK) * MOE_BLOCK
    pad_ends = jnp.cumsum(padded)
    pad_starts = pad_ends - padded
    dest = pad_starts[e_sorted] + jnp.arange(N, dtype=jnp.int32) - starts[e_sorted]
    n_blocks = (N + MOE_BLOCK - 1) // MOE_BLOCK + N_EXPERTS
    n_slots = n_blocks * MOE_BLOCK
    slot_tok = jnp.zeros((n_slots,), jnp.int32).at[dest].set((order // TOP_K).astype(jnp.int32))
    slot_gate = jnp.zeros((n_slots,), F32).at[dest].set(g_flat[order])
    slot_of = jnp.zeros((N,), jnp.int32).at[order].set(dest)
    block_e = jnp.minimum(
        jnp.searchsorted(pad_ends, jnp.arange(n_blocks, dtype=jnp.int32) * MOE_BLOCK, side='right'),
        N_EXPERTS - 1).astype(jnp.int32)
    n_valid = (pad_ends[-1:] // MOE_BLOCK).astype(jnp.int32)
    return slot_tok, slot_gate, slot_of, block_e, n_valid


def kernel(x, g_mix, w_in, b_in, lru_conv_w, lru_conv_b, lru_wa, lru_ba, lru_wi, lru_bi, lru_lambda, w_lru_br, hy_conv_w, hy_conv_b, hy_w1, hy_b1, hy_w2, hy_b2, hy_w3, hy_b3, hy_w4, hy_sin_freq, hy_bias, w_hy_br, w_out, g_ffn, w_router, b_router, w_gate_up, b_gate_up, w_down, b_down, g_final):
    B, L, D = x.shape
    T = B * L
    x2 = x.reshape(T, D)
    tm = min(1024, T)
    p_all = _inproj(x2, g_mix, w_in, b_in, tm)
    p3 = p_all.reshape(B, L, p_all.shape[1])
    y_lru = _lru(p3, lru_conv_w, lru_conv_b, lru_wa, lru_ba, lru_wi, lru_bi, lru_lambda, min(512, L))
    y_hy = _hyena(p3, D, 2 * D, hy_conv_w, hy_conv_b, hy_w1, hy_b1, hy_w2, hy_b2, hy_w3, hy_b3, hy_w4,
                  hy_sin_freq, hy_bias, min(1024, L), min(256, D), min(512, D))
    h, hn, top_idx, gates = _merge(x2, y_lru.reshape(T, D), y_hy, p_all, w_lru_br, w_hy_br, w_out,
                                   g_ffn, w_router, b_router, min(512, T))
    slot_tok, slot_gate, slot_of, block_e, n_valid = _route(top_idx, gates)
    yb = _moe_blocks(hn, slot_tok, slot_gate, block_e, n_valid, w_gate_up, b_gate_up, w_down, b_down)
    out = _combine(yb, slot_of, h, g_final, min(256, T))
    return out.reshape(B, L, D)
```

```python
import functools
import math

import jax
import jax.numpy as jnp
from jax import lax
from jax.experimental import pallas as pl
from jax.experimental.pallas import tpu as pltpu

F32 = jnp.float32
BF16 = jnp.bfloat16

EPS = 1e-5
C_RGLRU = 8.0
RNN_BLOCK = 64
HY_EMB = 33
HY_BANDS = 16
HY_FAST_DECAY = 0.3
HY_SLOW_DECAY = 1.5
HY_TARGET = 1e-2
N_EXPERTS = 32
TOP_K = 4
SWIGLU_LIMIT = 7.0
SWIGLU_ALPHA = 1.702
MOE_BLOCK = 512

LANES = 128
SUBLANES = 8
FFT_N2 = 128
VMEM_LIMIT = 56 * 1024 * 1024


def _cparams(sem):
    return pltpu.CompilerParams(dimension_semantics=sem, vmem_limit_bytes=VMEM_LIMIT)


def _inproj_kernel(x_ref, g_ref, w_ref, b_ref, o_ref, u_ref):
    j = pl.program_id(1)

    @pl.when(j == 0)
    def _():
        x = x_ref[...]
        ms = jnp.mean(x * x, axis=-1, keepdims=True)
        u_ref[...] = (x * lax.rsqrt(ms + EPS) * g_ref[...]).astype(BF16)

    acc = jnp.dot(u_ref[...], w_ref[...], preferred_element_type=F32) + b_ref[...]

    @pl.when(j == 1)
    def _():
        o_ref[...] = jax.nn.gelu(acc)

    @pl.when(j >= 5)
    def _():
        o_ref[...] = jax.nn.sigmoid(acc)

    @pl.when(jnp.logical_and(j != 1, j < 5))
    def _():
        o_ref[...] = acc


def _inproj(x2, g_mix, w_in, b_in, tm):
    T, D = x2.shape
    n_in = w_in.shape[1]
    return pl.pallas_call(
        _inproj_kernel,
        grid=(T // tm, n_in // D),
        in_specs=[
            pl.BlockSpec((tm, D), lambda i, j: (i, 0)),
            pl.BlockSpec((1, D), lambda i, j: (0, 0)),
            pl.BlockSpec((D, D), lambda i, j: (0, j)),
            pl.BlockSpec((1, D), lambda i, j: (0, j)),
        ],
        out_specs=pl.BlockSpec((tm, D), lambda i, j: (i, j)),
        out_shape=jax.ShapeDtypeStruct((T, n_in), F32),
        scratch_shapes=[pltpu.VMEM((tm, D), BF16)],
        compiler_params=_cparams(("parallel", "arbitrary")),
    )(x2, g_mix.reshape(1, D), w_in.astype(BF16), b_in.reshape(1, n_in))


def _lru_kernel(x_ref, g_ref, cw_ref, cb_ref, wg_ref, bg_ref, lam_ref, o_ref,
                xp_ref, hb_ref, af_ref, bf_ref, ab_ref, bb_ref, *, L, R):
    nchunk = L // R
    ng = R // SUBLANES
    zeros8 = jnp.zeros((SUBLANES, LANES), F32)
    xp_ref[0:SUBLANES, :] = zeros8
    xp_ref[SUBLANES + L:2 * SUBLANES + L, :] = zeros8
    xp_ref[SUBLANES:SUBLANES + L, :] = x_ref[...]

    sub = lax.broadcasted_iota(jnp.int32, (R, LANES), 0) % SUBLANES
    c8 = [-C_RGLRU * jax.nn.softplus(-lam_ref[d]) for d in range(2)]
    cb = cb_ref[...]
    cw = [cw_ref[k:k + 1, :] for k in range(4)]

    def prep(t0, d):
        xr = cb
        for k in range(4):
            xr = xr + cw[k] * xp_ref[pl.ds(t0 + (SUBLANES - 2 + k), R), :]
        gz = jnp.dot(xr.astype(BF16), wg_ref[d], preferred_element_type=F32) + bg_ref[d]
        r = jax.nn.sigmoid(gz[:, :LANES])
        i = jax.nn.sigmoid(gz[:, LANES:])
        log_a = c8[d] * r
        a = jnp.exp(log_a)
        b = jnp.sqrt((1.0 + a * a) * jnp.tanh(-log_a)) * (i * xr)
        return a, b

    def chunk(c, carry):
        t0f = pl.multiple_of(c * R, R)
        t0b = pl.multiple_of((nchunk - 1 - c) * R, R)
        a, b = prep(t0f, 0)
        for s in (1, 2, 4):
            a_sh = pltpu.roll(a, s, 0)
            b_sh = pltpu.roll(b, s, 0)
            m = sub >= s
            b = jnp.where(m, a * b_sh + b, b)
            a = jnp.where(m, a * a_sh, a)
        af_ref[...] = a
        bf_ref[...] = b
        a, b = prep(t0b, 1)
        for s in (1, 2, 4):
            a_sh = pltpu.roll(a, R - s, 0)
            b_sh = pltpu.roll(b, R - s, 0)
            m = sub < SUBLANES - s
            b = jnp.where(m, a * b_sh + b, b)
            a = jnp.where(m, a * a_sh, a)
        ab_ref[...] = a
        bb_ref[...] = b

        def group(j, hc):
            hf, hk = hc
            rf = pl.multiple_of(j * SUBLANES, SUBLANES)
            h8 = bf_ref[pl.ds(rf, SUBLANES), :] + af_ref[pl.ds(rf, SUBLANES), :] * hf
            o_ref[pl.ds(t0f + rf, SUBLANES), :] = h8
            hf = jnp.broadcast_to(h8[SUBLANES - 1:SUBLANES, :], (SUBLANES, LANES))
            rb = pl.multiple_of((ng - 1 - j) * SUBLANES, SUBLANES)
            k8 = bb_ref[pl.ds(rb, SUBLANES), :] + ab_ref[pl.ds(rb, SUBLANES), :] * hk
            hb_ref[pl.ds(t0b + rb, SUBLANES), :] = k8
            hk = jnp.broadcast_to(k8[0:1, :], (SUBLANES, LANES))
            return hf, hk

        return lax.fori_loop(0, ng, group, carry, unroll=8)

    lax.fori_loop(0, nchunk, chunk, (zeros8, zeros8))

    def fin(c, _):
        t0 = pl.multiple_of(c * R, R)
        o_ref[pl.ds(t0, R), :] = g_ref[pl.ds(t0, R), :] * (o_ref[pl.ds(t0, R), :] + hb_ref[pl.ds(t0, R), :])
        return 0

    lax.fori_loop(0, nchunk, fin, 0)


def _lru(p3, conv_w, conv_b, wa, ba, wi, bi, lam, R):
    B, L, _ = p3.shape
    D = conv_w.shape[1]
    nt = D // LANES
    hpt = LANES // RNN_BLOCK

    def blockdiag(w):
        w = w.reshape(2, nt, hpt, RNN_BLOCK, RNN_BLOCK)
        eye = jnp.eye(hpt, dtype=w.dtype)
        return jnp.einsum('dthij,hg->dthigj', w, eye).reshape(2, nt, LANES, LANES)

    wg = jnp.concatenate([blockdiag(wa), blockdiag(wi)], axis=-1).astype(BF16)
    bg = jnp.concatenate([ba.reshape(2, nt, 1, LANES), bi.reshape(2, nt, 1, LANES)], axis=-1)
    lam4 = lam.reshape(2, nt, 1, LANES)
    return pl.pallas_call(
        functools.partial(_lru_kernel, L=L, R=R),
        grid=(B, nt),
        in_specs=[
            pl.BlockSpec((None, L, LANES), lambda b, c: (b, 0, c)),
            pl.BlockSpec((None, L, LANES), lambda b, c: (b, 0, nt + c)),
            pl.BlockSpec((4, LANES), lambda b, c: (0, c)),
            pl.BlockSpec((1, LANES), lambda b, c: (0, c)),
            pl.BlockSpec((2, None, LANES, 2 * LANES), lambda b, c: (0, c, 0, 0)),
            pl.BlockSpec((2, None, 1, 2 * LANES), lambda b, c: (0, c, 0, 0)),
            pl.BlockSpec((2, None, 1, LANES), lambda b, c: (0, c, 0, 0)),
        ],
        out_specs=pl.BlockSpec((None, L, LANES), lambda b, c: (b, 0, c)),
        out_shape=jax.ShapeDtypeStruct((B, L, D), F32),
        scratch_shapes=[
            pltpu.VMEM((L + 2 * SUBLANES, LANES), F32),
            pltpu.VMEM((L, LANES), F32),
            pltpu.VMEM((R, LANES), F32), pltpu.VMEM((R, LANES), F32),
            pltpu.VMEM((R, LANES), F32), pltpu.VMEM((R, LANES), F32),
        ],
        compiler_params=_cparams(("parallel", "parallel")),
    )(p3, p3, conv_w, conv_b.reshape(1, D), wg, bg, lam4)


def _hy_pre_kernel(v_ref, x1_ref, x2_ref, vp_ref, x1p_ref, x2p_ref, vn_ref, x1n_ref, x2n_ref,
                   w_ref, b_ref, z_ref, x1c_ref, *, nt):
    i = pl.program_id(1)
    tl = v_ref.shape[0]
    row = lax.broadcasted_iota(jnp.int32, v_ref.shape, 0)
    first = i == 0
    last = i == nt - 1

    def conv(c, m_ref, p_ref, n_ref):
        x = m_ref[...]
        prev = jnp.where(first, 0.0, p_ref[SUBLANES - 1:SUBLANES, :])
        nxt = jnp.where(last, 0.0, n_ref[0:1, :])
        xm = jnp.where(row == 0, prev, pltpu.roll(x, 1, 0))
        xn = jnp.where(row == tl - 1, nxt, pltpu.roll(x, tl - 1, 0))
        return b_ref[c] + w_ref[c, 0:1, :] * xm + w_ref[c, 1:2, :] * x + w_ref[c, 2:3, :] * xn

    v = conv(0, v_ref, vp_ref, vn_ref)
    x1c_ref[...] = conv(1, x1_ref, x1p_ref, x1n_ref)
    z_ref[...] = conv(2, x2_ref, x2p_ref, x2n_ref) * v


def _hy_pre(p3, conv_w, conv_b, D, col0, tl, ct):
    B, L, n_in = p3.shape
    nt = L // tl
    ncb = D // ct
    g8 = tl // SUBLANES
    p4 = p3.reshape(B, L // SUBLANES, SUBLANES, n_in)
    w3 = conv_w.reshape(3, 3, D).transpose(1, 0, 2)
    b3 = conv_b.reshape(3, 1, D)
    cb0 = col0 // ct

    def main(part):
        return pl.BlockSpec((None, tl, ct), lambda b, i, c: (b, i, cb0 + part * ncb + c))

    def prev(part):
        return pl.BlockSpec((None, None, SUBLANES, ct),
                            lambda b, i, c: (b, jnp.maximum(i * g8 - 1, 0), 0, cb0 + part * ncb + c))

    def nxt(part):
        return pl.BlockSpec((None, None, SUBLANES, ct),
                            lambda b, i, c: (b, jnp.minimum((i + 1) * g8, L // SUBLANES - 1), 0, cb0 + part * ncb + c))

    out = pl.BlockSpec((None, tl, ct), lambda b, i, c: (b, i, c))
    return pl.pallas_call(
        functools.partial(_hy_pre_kernel, nt=nt),
        grid=(B, nt, ncb),
        in_specs=[main(0), main(1), main(2), prev(0), prev(1), prev(2), nxt(0), nxt(1), nxt(2),
                  pl.BlockSpec((3, 3, ct), lambda b, i, c: (0, 0, c)),
                  pl.BlockSpec((3, 1, ct), lambda b, i, c: (0, 0, c))],
        out_specs=[out, out],
        out_shape=[jax.ShapeDtypeStruct((B, L, D), F32)] * 2,
        compiler_params=_cparams(("parallel", "parallel", "parallel")),
    )(p3, p3, p3, p4, p4, p4, p4, p4, p4, w3, b3)


def _filt_kernel(feat_ref, t_ref, w1_ref, b1_ref, w2_ref, b2_ref, w3_ref, b3_ref, w4f_ref, w4b_ref, fr_ref, dl_ref,
                 k_ref, s_ref, *, nb):
    i = pl.program_id(0)
    hp = lax.Precision.HIGHEST
    fr = fr_ref[...]
    fwd = i < nb
    hh = jnp.sin(fr * (jnp.dot(feat_ref[...], w1_ref[...], precision=hp, preferred_element_type=F32) + b1_ref[...]))
    hh = jnp.sin(fr * (jnp.dot(hh, w2_ref[...], precision=hp, preferred_element_type=F32) + b2_ref[...]))
    hh = jnp.sin(fr * (jnp.dot(hh, w3_ref[...], precision=hp, preferred_element_type=F32) + b3_ref[...]))
    w4 = jnp.where(fwd, w4f_ref[...], w4b_ref[...])
    decay = jnp.exp(-t_ref[...] * jnp.abs(dl_ref[...]))
    k = jnp.dot(hh, w4, precision=hp, preferred_element_type=F32) * decay
    row = lax.broadcasted_iota(jnp.int32, k.shape, 0)
    kb0 = (jnp.dot(hh[0:SUBLANES], w4b_ref[...], precision=hp, preferred_element_type=F32) * decay[0:SUBLANES])[0:1]
    k = jnp.where(jnp.logical_and(row == 0, i == 0), k + kb0, k)
    k = jnp.where(jnp.logical_and(row == 0, i == nb), 0.0, k)
    k_ref[...] = k
    part = jnp.sum(jnp.abs(k), axis=0, keepdims=True)

    @pl.when(i == 0)
    def _():
        s_ref[...] = part

    @pl.when(i > 0)
    def _():
        s_ref[...] += part


def _hyena_filter(L, D, w1, b1, w2, b2, w3, b3, w4, freq, tl):
    t = jnp.linspace(0.0, 1.0, L, dtype=F32)[:, None]
    w = 2.0 * math.pi * jnp.arange(L, dtype=F32)[:, None] / L
    bands = jnp.linspace(1e-4, HY_BANDS - 1, HY_BANDS, dtype=F32)[None, :]
    feats = jnp.concatenate([t, jnp.cos(w * bands), -jnp.sin(w * bands)], axis=-1)
    deltas = jnp.linspace(math.log(HY_TARGET) / HY_SLOW_DECAY, math.log(HY_TARGET) / HY_FAST_DECAY, D, dtype=F32)
    rev = lambda a: jnp.roll(jnp.flip(a, 0), 1, 0)
    feats2 = jnp.concatenate([feats, rev(feats)], axis=0)
    t2 = jnp.concatenate([t, rev(t)], axis=0)
    ff = w1.shape[1]
    nb = L // tl
    full = lambda shape: pl.BlockSpec(shape, lambda i: (0,) * len(shape))
    return pl.pallas_call(
        functools.partial(_filt_kernel, nb=nb),
        grid=(2 * nb,),
        in_specs=[pl.BlockSpec((tl, HY_EMB), lambda i: (i, 0)), pl.BlockSpec((tl, 1), lambda i: (i, 0)),
                  full((HY_EMB, ff)), full((1, ff)), full((ff, ff)), full((1, ff)), full((ff, ff)), full((1, ff)),
                  full((ff, D)), full((ff, D)), full((1, ff)), full((1, D))],
        out_specs=[pl.BlockSpec((tl, D), lambda i: (i, 0)), pl.BlockSpec((1, D), lambda i: (0, 0))],
        out_shape=[jax.ShapeDtypeStruct((2 * L, D), F32), jax.ShapeDtypeStruct((1, D), F32)],
        compiler_params=_cparams(("arbitrary",)),
    )(feats2, t2, w1, b1.reshape(1, ff), w2, b2.reshape(1, ff), w3, b3.reshape(1, ff), w4[:, :D], w4[:, D:],
      freq.reshape(1, ff), deltas.reshape(1, D))


def _angles(rows, cols, n):
    r = (rows * cols) % n
    th = r.astype(F32) * (2.0 * math.pi / n)
    return jnp.cos(th), jnp.sin(th)


def _dft_mats(N1, N2):
    N = N1 * N2
    h = N1 // 2
    ar = lambda n: jnp.arange(n, dtype=jnp.int32)
    k1 = ar(N1)[None, :, None]
    tn = ar(h)[None, None, :] * N2 + ar(N2)[:, None, None]
    c, s = _angles(k1, tn, N)
    a_data = jnp.concatenate([jnp.concatenate([c, s], -1), jnp.concatenate([-s, c], -1)], 1)
    tn = ar(N1)[None, None, :] * N2 + ar(N2)[:, None, None]
    c, s = _angles(k1, tn, N)
    a_filt = jnp.concatenate([c, -s], 1)
    c, s = _angles(ar(N2)[:, None], ar(N2)[None, :], N2)
    m2 = jnp.concatenate([jnp.concatenate([c, s], -1), jnp.concatenate([-s, c], -1)], 0)
    kk = ar(N1)[:, None, None] + N1 * ar(N2)[None, None, :]
    c, s = _angles(ar(N2)[None, :, None], kk, N)
    hm = jnp.concatenate([jnp.concatenate([c, -s], -1), jnp.concatenate([s, c], -1)], 1)
    c, s = _angles(ar(h)[:, None], ar(N1)[None, :], N1)
    i2 = jnp.concatenate([jnp.concatenate([c, -s], -1), jnp.concatenate([s, c], -1)], 0) / N
    return (a_data.astype(BF16), a_filt.astype(BF16), m2.astype(BF16), hm.astype(BF16), i2.astype(BF16))


def _fft1_kernel(*refs, scaled):
    a_ref = refs[0]
    inv_s = refs[1][...] if scaled else None
    x_ref, o_ref = refs[-2], refs[-1]
    nin = x_ref.shape[0]
    n1 = o_ref.shape[1]
    for j in range(SUBLANES):
        rows = [x_ref[q, :, j, :] for q in range(nin)]
        x = rows[0] if nin == 1 else jnp.concatenate(rows, axis=0)
        if scaled:
            x = x * inv_s
        y = jnp.dot(a_ref[j], x.astype(BF16), preferred_element_type=F32)
        o_ref[0, :, j, :] = y[:n1]
        o_ref[1, :, j, :] = y[n1:]


def _fft1_data(z5, a_data, dt):
    _, P, h, N2, D = z5.shape
    N1 = 2 * h
    return pl.pallas_call(
        functools.partial(_fft1_kernel, scaled=False),
        grid=(N2 // SUBLANES, P, D // dt),
        in_specs=[pl.BlockSpec((SUBLANES, 2 * N1, N1), lambda j, p, c: (j, 0, 0)),
                  pl.BlockSpec((2, None, h, SUBLANES, dt), lambda j, p, c: (0, p, 0, j, c))],
        out_specs=pl.BlockSpec((None, 2, N1, SUBLANES, dt), lambda j, p, c: (p, 0, 0, j, c)),
        out_shape=jax.ShapeDtypeStruct((P, 2, N1, N2, D), F32),
        compiler_params=_cparams(("parallel", "parallel", "parallel")),
    )(a_data, z5)


def _fft1_filt(k4, inv_s, a_filt, dt):
    _, N1, N2, D = k4.shape
    return pl.pallas_call(
        functools.partial(_fft1_kernel, scaled=True),
        grid=(N2 // SUBLANES, D // dt),
        in_specs=[pl.BlockSpec((SUBLANES, 2 * N1, N1), lambda j, c: (j, 0, 0)),
                  pl.BlockSpec((1, dt), lambda j, c: (0, c)),
                  pl.BlockSpec((1, N1, SUBLANES, dt), lambda j, c: (0, 0, j, c))],
        out_specs=pl.BlockSpec((2, N1, SUBLANES, dt), lambda j, c: (0, 0, j, c)),
        out_shape=jax.ShapeDtypeStruct((2, N1, N2, D), F32),
        compiler_params=_cparams(("parallel", "parallel")),
    )(a_filt, inv_s, k4)


def _fft2_filt_kernel(m2_ref, y_ref, o_ref):
    n2 = y_ref.shape[1]
    y = jnp.concatenate([y_ref[0], y_ref[1]], axis=0).astype(BF16)
    x = jnp.dot(m2_ref[...], y, preferred_element_type=F32)
    o_ref[0] = x[:n2]
    o_ref[1] = x[n2:]


def _fft2_filt(yk, m2, dt):
    _, N1, N2, D = yk.shape
    return pl.pallas_call(
        _fft2_filt_kernel,
        grid=(N1, D // dt),
        in_specs=[pl.BlockSpec((2 * N2, 2 * N2), lambda k, c: (0, 0)),
                  pl.BlockSpec((2, None, N2, dt), lambda k, c: (0, k, 0, c))],
        out_specs=pl.BlockSpec((2, None, N2, dt), lambda k, c: (0, k, 0, c)),
        out_shape=jax.ShapeDtypeStruct((2, N1, N2, D), F32),
        compiler_params=_cparams(("parallel", "parallel")),
    )(m2, yk)


def _fft_mid_kernel(m2_ref, h_ref, y_ref, k_ref, o_ref):
    n2 = y_ref.shape[1]
    y = jnp.concatenate([y_ref[0], y_ref[1]], axis=0).astype(BF16)
    x = jnp.dot(m2_ref[...], y, preferred_element_type=F32)
    xr, xi = x[:n2], x[n2:]
    kr, ki = k_ref[0], k_ref[1]
    pr = xr * kr - xi * ki
    pi = xr * ki + xi * kr
    p = jnp.concatenate([pr, pi], axis=0).astype(BF16)
    u = jnp.dot(h_ref[...], p, preferred_element_type=F32)
    o_ref[0] = u[:n2]
    o_ref[1] = u[n2:]


def _fft_mid(y, kf, m2, hm, dt):
    P, _, N1, N2, D = y.shape
    return pl.pallas_call(
        _fft_mid_kernel,
        grid=(N1, D // dt, P),
        in_specs=[pl.BlockSpec((2 * N2, 2 * N2), lambda k, c, p: (0, 0)),
                  pl.BlockSpec((None, 2 * N2, 2 * N2), lambda k, c, p: (k, 0, 0)),
                  pl.BlockSpec((None, 2, None, N2, dt), lambda k, c, p: (p, 0, k, 0, c)),
                  pl.BlockSpec((2, None, N2, dt), lambda k, c, p: (0, k, 0, c))],
        out_specs=pl.BlockSpec((None, 2, None, N2, dt), lambda k, c, p: (p, 0, k, 0, c)),
        out_shape=jax.ShapeDtypeStruct((P, 2, N1, N2, D), F32),
        compiler_params=_cparams(("parallel", "parallel", "parallel")),
    )(m2, hm, y, kf)


def _fft_inv_kernel(i2_ref, u_ref, z_ref, x1_ref, bias_ref, o_ref):
    h = o_ref.shape[1]
    bias = bias_ref[...]
    for j in range(SUBLANES):
        u = jnp.concatenate([u_ref[0, :, j, :], u_ref[1, :, j, :]], axis=0).astype(BF16)
        y = jnp.dot(i2_ref[...], u, preferred_element_type=F32)
        o_ref[0, :, j, :] = x1_ref[0, :, j, :] * (y[:h] + bias * z_ref[0, :, j, :])
        o_ref[1, :, j, :] = x1_ref[1, :, j, :] * (y[h:] + bias * z_ref[1, :, j, :])


def _fft_inv(u, z5, x15, bias, i2, dt):
    P, _, N1, N2, D = u.shape
    h = N1 // 2
    pair = pl.BlockSpec((2, None, h, SUBLANES, dt), lambda j, p, c: (0, p, 0, j, c))
    return pl.pallas_call(
        _fft_inv_kernel,
        grid=(N2 // SUBLANES, P, D // dt),
        in_specs=[pl.BlockSpec((N1, 2 * N1), lambda j, p, c: (0, 0)),
                  pl.BlockSpec((None, 2, N1, SUBLANES, dt), lambda j, p, c: (p, 0, 0, j, c)),
                  pair, pair,
                  pl.BlockSpec((1, dt), lambda j, p, c: (0, c))],
        out_specs=pair,
        out_shape=jax.ShapeDtypeStruct((2, P, h, N2, D), F32),
        compiler_params=_cparams(("parallel", "parallel", "parallel")),
    )(i2, u, z5, x15, bias)


def _hyena(p3, D, col0, conv_w, conv_b, w1, b1, w2, b2, w3, b3, w4, freq, bias, tl, ct, dt):
    B, L, _ = p3.shape
    N2 = FFT_N2
    N1 = 2 * L // N2
    h = N1 // 2
    z, x1c = _hy_pre(p3, conv_w, conv_b, D, col0, tl, ct)
    k_circ, s = _hyena_filter(L, D, w1, b1, w2, b2, w3, b3, w4, freq, min(tl, 512))
    a_data, a_filt, m2, hm, i2 = _dft_mats(N1, N2)
    kspec = _fft2_filt(_fft1_filt(k_circ.reshape(1, N1, N2, D), 1.0 / s, a_filt, dt), m2, dt)
    z5 = z.reshape(2, B // 2, h, N2, D)
    y = _fft1_data(z5, a_data, dt)
    u = _fft_mid(y, kspec, m2, hm, dt)
    out = _fft_inv(u, z5, x1c.reshape(2, B // 2, h, N2, D), bias.reshape(1, D), i2, dt)
    return out.reshape(B * L, D)


def _merge_kernel(x_ref, yl_ref, yh_ref, gl_ref, gh_ref, wl_ref, wh_ref, wo_ref, gf_ref, wr_ref, br_ref,
                  h_ref, hn_ref, idx_ref, gate_ref):
    a = jnp.dot(yl_ref[...].astype(BF16), wl_ref[...], preferred_element_type=F32)
    b = jnp.dot(yh_ref[...].astype(BF16), wh_ref[...], preferred_element_type=F32)
    mix = gl_ref[...] * a + gh_ref[...] * b
    h = x_ref[...] + jnp.dot(mix.astype(BF16), wo_ref[...], preferred_element_type=F32)
    h_ref[...] = h
    ms = jnp.mean(h * h, axis=-1, keepdims=True)
    hn = h * lax.rsqrt(ms + EPS) * gf_ref[...]
    hn_ref[...] = hn
    ne = br_ref.shape[1]
    h_hi = hn.astype(BF16)
    h_lo = (hn - h_hi.astype(F32)).astype(BF16)
    a2 = jnp.dot(h_hi, wr_ref[...], preferred_element_type=F32)
    b2 = jnp.dot(h_lo, wr_ref[:, :LANES], preferred_element_type=F32)
    logits = (a2[:, :LANES] + a2[:, LANES:] + b2)[:, :ne] + br_ref[...]
    lane = lax.broadcasted_iota(jnp.int32, logits.shape, 1)
    vals = []
    for k in range(TOP_K):
        m = jnp.max(logits, axis=-1, keepdims=True)
        ix = jnp.min(jnp.where(logits == m, lane, ne), axis=-1, keepdims=True)
        vals.append(m)
        idx_ref[:, k:k + 1] = ix
        logits = jnp.where(lane == ix, -jnp.inf, logits)
    ex = [jnp.exp(v - vals[0]) for v in vals]
    tot = ex[0] + ex[1] + ex[2] + ex[3]
    for k in range(TOP_K):
        gate_ref[:, k:k + 1] = ex[k] / tot


def _merge(x2, y_lru, y_hy, p_all, w_lru_br, w_hy_br, w_out, g_ffn, w_router, b_router, tm):
    T, D = x2.shape
    ne = w_router.shape[1]
    gcol = p_all.shape[1] // D - 2
    row = lambda i: (i, 0)
    full = lambda i: (0, 0)
    w_hi = w_router.astype(BF16)
    w_lo = (w_router - w_hi.astype(F32)).astype(BF16)
    lane_pad = lambda w: jnp.pad(w, ((0, 0), (0, LANES - ne)))
    wr2 = jnp.concatenate([lane_pad(w_hi), lane_pad(w_lo)], axis=1)
    return pl.pallas_call(
        _merge_kernel,
        grid=(T // tm,),
        in_specs=[pl.BlockSpec((tm, D), row), pl.BlockSpec((tm, D), row), pl.BlockSpec((tm, D), row),
                  pl.BlockSpec((tm, D), lambda i: (i, gcol)), pl.BlockSpec((tm, D), lambda i: (i, gcol + 1)),
                  pl.BlockSpec((D, D), full), pl.BlockSpec((D, D), full), pl.BlockSpec((D, D), full),
                  pl.BlockSpec((1, D), full), pl.BlockSpec((D, 2 * LANES), full), pl.BlockSpec((1, ne), full)],
        out_specs=[pl.BlockSpec((tm, D), row), pl.BlockSpec((tm, D), row),
                   pl.BlockSpec((tm, TOP_K), row), pl.BlockSpec((tm, TOP_K), row)],
        out_shape=[jax.ShapeDtypeStruct((T, D), F32), jax.ShapeDtypeStruct((T, D), F32),
                   jax.ShapeDtypeStruct((T, TOP_K), jnp.int32), jax.ShapeDtypeStruct((T, TOP_K), F32)],
        compiler_params=_cparams(("parallel",)),
    )(x2, y_lru, y_hy, p_all, p_all, w_lru_br.astype(BF16), w_hy_br.astype(BF16), w_out.astype(BF16),
      g_ffn.reshape(1, D), wr2, b_router.reshape(1, ne))


ROW_DMA_UNROLL = 8


def _for_rows(n, fn):
    def body(g, c):
        for k in range(ROW_DMA_UNROLL):
            fn(g * ROW_DMA_UNROLL + k, k % 2)
        return c

    lax.fori_loop(0, n // ROW_DMA_UNROLL, body, 0)


def _moe_kernel(ib_ref, ie_ref, bnd_ref, ni_ref, ord_ref, nord_ref, hn_hbm, gate_ref, wgu_ref, bgu_ref,
                wd_ref, bd_ref, yk_hbm, xbuf, ybuf, gsem, ssem, *, nblk):
    i = pl.program_id(0)
    ni = ni_ref[0]
    nrow = xbuf.shape[1]
    dff = wd_ref.shape[0]

    def gather_copy(idx_ref, r, slot):
        tok = lax.shift_right_logical(idx_ref[0, 0, r], 2)
        return pltpu.make_async_copy(hn_hbm.at[pl.ds(tok, 1), :], xbuf.at[slot, pl.ds(r, 1), :], gsem.at[slot])

    def scatter_copy(r, slot):
        return pltpu.make_async_copy(ybuf.at[slot, pl.ds(r, 1), :], yk_hbm.at[pl.ds(ord_ref[0, 0, r], 1), :],
                                     ssem.at[slot])

    @pl.when(i < ni)
    def _():
        b = ib_ref[i]
        e = ie_ref[i]
        slot = b % 2
        first = jnp.logical_or(i == 0, ib_ref[jnp.maximum(i - 1, 0)] != b)
        last = jnp.logical_or(i == ni - 1, ib_ref[jnp.minimum(i + 1, pl.num_programs(0) - 1)] != b)

        @pl.when(i == 0)
        def _():
            _for_rows(nrow, lambda r, p: gather_copy(ord_ref, r, 0).start(priority=p))

        @pl.when(first)
        def _():
            @pl.when(b + 1 < nblk)
            def _():
                _for_rows(nrow, lambda r, p: gather_copy(nord_ref, r, 1 - slot).start(priority=p))

            _for_rows(nrow, lambda r, p: gather_copy(ord_ref, r, slot).wait())

            @pl.when(b >= 2)
            def _():
                _for_rows(nrow, lambda r, p: scatter_copy(r, slot).wait())

        xb = xbuf[slot].astype(BF16)
        gu = jnp.dot(xb, wgu_ref[...], preferred_element_type=F32) + bgu_ref[...]
        g = jnp.minimum(gu[:, :dff], SWIGLU_LIMIT)
        u = jnp.clip(gu[:, dff:], -SWIGLU_LIMIT, SWIGLU_LIMIT)
        act = (u + 1.0) * (g * jax.nn.sigmoid(SWIGLU_ALPHA * g))
        y = (jnp.dot(act.astype(BF16), wd_ref[...], preferred_element_type=F32) + bd_ref[...]) * gate_ref[...]
        row = lax.broadcasted_iota(jnp.int32, (nrow, 1), 0) + b * nrow
        mine = jnp.logical_and(row >= bnd_ref[e], row < bnd_ref[e + 1])

        @pl.when(first)
        def _():
            ybuf[slot] = jnp.where(mine, y, 0.0)

        @pl.when(jnp.logical_not(first))
        def _():
            ybuf[slot] = jnp.where(mine, y, ybuf[slot])

        @pl.when(last)
        def _():
            _for_rows(nrow, lambda r, p: scatter_copy(r, slot).start(priority=p))

        @pl.when(i == ni - 1)
        def _():
            _for_rows(nrow, lambda r, p: scatter_copy(r, slot).wait())

            @pl.when(b >= 1)
            def _():
                _for_rows(nrow, lambda r, p: scatter_copy(r, 1 - slot).wait())


def _moe_items(hn, order, g_sorted, item_blk, item_exp, bounds, n_items, w_gate_up, b_gate_up, w_down, b_down):
    T, D = hn.shape
    N = order.shape[0]
    nblk = N // MOE_BLOCK
    n_steps = item_blk.shape[0]
    dff = w_down.shape[1]
    order3 = order.reshape(nblk, 1, MOE_BLOCK)
    grid_spec = pltpu.PrefetchScalarGridSpec(
        num_scalar_prefetch=4,
        grid=(n_steps,),
        in_specs=[
            pl.BlockSpec((1, 1, MOE_BLOCK), lambda i, ib, ie, bd, ni: (ib[i], 0, 0), memory_space=pltpu.SMEM),
            pl.BlockSpec((1, 1, MOE_BLOCK), lambda i, ib, ie, bd, ni: (jnp.minimum(ib[i] + 1, nblk - 1), 0, 0),
                         memory_space=pltpu.SMEM),
            pl.BlockSpec(memory_space=pl.ANY),
            pl.BlockSpec((MOE_BLOCK, 1), lambda i, ib, ie, bd, ni: (ib[i], 0)),
            pl.BlockSpec((None, D, 2 * dff), lambda i, ib, ie, bd, ni: (ie[i], 0, 0)),
            pl.BlockSpec((None, 1, 2 * dff), lambda i, ib, ie, bd, ni: (ie[i], 0, 0)),
            pl.BlockSpec((None, dff, D), lambda i, ib, ie, bd, ni: (ie[i], 0, 0)),
            pl.BlockSpec((None, 1, D), lambda i, ib, ie, bd, ni: (ie[i], 0, 0)),
        ],
        out_specs=pl.BlockSpec(memory_space=pl.ANY),
        scratch_shapes=[pltpu.VMEM((2, MOE_BLOCK, D), F32), pltpu.VMEM((2, MOE_BLOCK, D), F32),
                        pltpu.SemaphoreType.DMA((2,)), pltpu.SemaphoreType.DMA((2,))],
    )
    return pl.pallas_call(
        functools.partial(_moe_kernel, nblk=nblk),
        grid_spec=grid_spec,
        out_shape=jax.ShapeDtypeStruct((N, D), F32),
        compiler_params=_cparams(("arbitrary",)),
    )(item_blk, item_exp, bounds, n_items, order3, order3, hn, g_sorted.reshape(N, 1),
      w_gate_up.astype(BF16), b_gate_up.reshape(N_EXPERTS, 1, 2 * dff), w_down.astype(BF16),
      b_down.reshape(N_EXPERTS, 1, D))


def _combine_kernel(yk_ref, h_ref, g_ref, o_ref):
    D = h_ref.shape[1]
    y = yk_ref[:, 0:D]
    for k in range(1, TOP_K):
        y = y + yk_ref[:, k * D:(k + 1) * D]
    h = h_ref[...] + y
    ms = jnp.mean(h * h, axis=-1, keepdims=True)
    o_ref[...] = h * lax.rsqrt(ms + EPS) * g_ref[...]


def _combine(yk, h, g_final, tq):
    T, D = h.shape
    return pl.pallas_call(
        _combine_kernel,
        grid=(T // tq,),
        in_specs=[pl.BlockSpec((tq, TOP_K * D), lambda i: (i, 0)),
                  pl.BlockSpec((tq, D), lambda i: (i, 0)),
                  pl.BlockSpec((1, D), lambda i: (0, 0))],
        out_specs=pl.BlockSpec((tq, D), lambda i: (i, 0)),
        out_shape=jax.ShapeDtypeStruct((T, D), F32),
        compiler_params=_cparams(("parallel",)),
    )(yk.reshape(T, TOP_K * D), h, g_final.reshape(1, D))


def _route(top_idx, gates):
    T = top_idx.shape[0]
    N = T * TOP_K
    nblk = N // MOE_BLOCK
    i32 = jnp.int32
    e_sorted, order, g_sorted = lax.sort(
        (top_idx.reshape(N), jnp.arange(N, dtype=i32), gates.reshape(N)), num_keys=1)
    bounds = jnp.searchsorted(e_sorted, jnp.arange(N_EXPERTS + 1, dtype=i32), side='left').astype(i32)
    starts, ends = bounds[:-1], bounds[1:]
    first_blk = starts // MOE_BLOCK
    n_of = jnp.where(ends > starts, (ends - 1) // MOE_BLOCK - first_blk + 1, 0)
    item_end = jnp.cumsum(n_of)
    item_start = item_end - n_of
    n_items = item_end[-1:]
    n_steps = nblk + N_EXPERTS - 1
    i = jnp.arange(n_steps, dtype=i32)
    ie = jnp.minimum(jnp.searchsorted(item_end, i, side='right'), N_EXPERTS - 1).astype(i32)
    ib = jnp.clip(first_blk[ie] + i - item_start[ie], 0, nblk - 1).astype(i32)
    return order, g_sorted, ib, ie, bounds, n_items.astype(i32)


def kernel(x, g_mix, w_in, b_in, lru_conv_w, lru_conv_b, lru_wa, lru_ba, lru_wi, lru_bi, lru_lambda, w_lru_br, hy_conv_w, hy_conv_b, hy_w1, hy_b1, hy_w2, hy_b2, hy_w3, hy_b3, hy_w4, hy_sin_freq, hy_bias, w_hy_br, w_out, g_ffn, w_router, b_router, w_gate_up, b_gate_up, w_down, b_down, g_final):
    B, L, D = x.shape
    T = B * L
    x2 = x.reshape(T, D)
    tm = min(1024, T)
    p_all = _inproj(x2, g_mix, w_in, b_in, tm)
    p3 = p_all.reshape(B, L, p_all.shape[1])
    y_lru = _lru(p3, lru_conv_w, lru_conv_b, lru_wa, lru_ba, lru_wi, lru_bi, lru_lambda, min(512, L))
    y_hy = _hyena(p3, D, 2 * D, hy_conv_w, hy_conv_b, hy_w1, hy_b1, hy_w2, hy_b2, hy_w3, hy_b3, hy_w4,
                  hy_sin_freq, hy_bias, min(1024, L), min(256, D), min(512, D))
    h, hn, top_idx, gates = _merge(x2, y_lru.reshape(T, D), y_hy, p_all, w_lru_br, w_hy_br, w_out,
                                   g_ffn, w_router, b_router, min(512, T))
    order, g_sorted, item_blk, item_exp, bounds, n_items = _route(top_idx, gates)
    yk = _moe_items(hn, order, g_sorted, item_blk, item_exp, bounds, n_items,
                    w_gate_up, b_gate_up, w_down, b_down)
    out = _combine(yk, h, g_final, min(256, T))
    return out.reshape(B, L, D)
```

```python
import functools
import math

import jax
import jax.numpy as jnp
from jax import lax
from jax.experimental import pallas as pl
from jax.experimental.pallas import tpu as pltpu

F32 = jnp.float32
BF16 = jnp.bfloat16

EPS = 1e-5
C_RGLRU = 8.0
RNN_BLOCK = 64
HY_EMB = 33
HY_BANDS = 16
HY_FAST_DECAY = 0.3
HY_SLOW_DECAY = 1.5
HY_TARGET = 1e-2
N_EXPERTS = 32
TOP_K = 4
SWIGLU_LIMIT = 7.0
SWIGLU_ALPHA = 1.702
MOE_BLOCK = 512

LANES = 128
SUBLANES = 8
FFT_N2 = 128
VMEM_LIMIT = 56 * 1024 * 1024


def _cparams(sem):
    return pltpu.CompilerParams(dimension_semantics=sem, vmem_limit_bytes=VMEM_LIMIT)


def _inproj_kernel(x_ref, g_ref, w_ref, b_ref, o_ref, u_ref):
    j = pl.program_id(1)

    @pl.when(j == 0)
    def _():
        x = x_ref[...]
        ms = jnp.mean(x * x, axis=-1, keepdims=True)
        u_ref[...] = (x * lax.rsqrt(ms + EPS) * g_ref[...]).astype(BF16)

    acc = jnp.dot(u_ref[...], w_ref[...], preferred_element_type=F32) + b_ref[...]

    @pl.when(j == 1)
    def _():
        o_ref[...] = jax.nn.gelu(acc)

    @pl.when(j >= 5)
    def _():
        o_ref[...] = jax.nn.sigmoid(acc)

    @pl.when(jnp.logical_and(j != 1, j < 5))
    def _():
        o_ref[...] = acc


def _inproj(x2, g_mix, w_in, b_in, tm):
    T, D = x2.shape
    n_in = w_in.shape[1]
    return pl.pallas_call(
        _inproj_kernel,
        grid=(T // tm, n_in // D),
        in_specs=[
            pl.BlockSpec((tm, D), lambda i, j: (i, 0)),
            pl.BlockSpec((1, D), lambda i, j: (0, 0)),
            pl.BlockSpec((D, D), lambda i, j: (0, j)),
            pl.BlockSpec((1, D), lambda i, j: (0, j)),
        ],
        out_specs=pl.BlockSpec((tm, D), lambda i, j: (i, j)),
        out_shape=jax.ShapeDtypeStruct((T, n_in), F32),
        scratch_shapes=[pltpu.VMEM((tm, D), BF16)],
        compiler_params=_cparams(("parallel", "arbitrary")),
    )(x2, g_mix.reshape(1, D), w_in.astype(BF16), b_in.reshape(1, n_in))


def _lru_kernel(x_ref, g_ref, cw_ref, cb_ref, wg_ref, bg_ref, lam_ref, o_ref,
                xp_ref, hb_ref, af_ref, bf_ref, ab_ref, bb_ref, *, L, R):
    nchunk = L // R
    ng = R // SUBLANES
    zeros8 = jnp.zeros((SUBLANES, LANES), F32)
    xp_ref[0:SUBLANES, :] = zeros8
    xp_ref[SUBLANES + L:2 * SUBLANES + L, :] = zeros8
    xp_ref[SUBLANES:SUBLANES + L, :] = x_ref[...]

    sub = lax.broadcasted_iota(jnp.int32, (R, LANES), 0) % SUBLANES
    c8 = [-C_RGLRU * jax.nn.softplus(-lam_ref[d]) for d in range(2)]
    cb = cb_ref[...]
    cw = [cw_ref[k:k + 1, :] for k in range(4)]

    def prep(t0, d):
        xr = cb
        for k in range(4):
            xr = xr + cw[k] * xp_ref[pl.ds(t0 + (SUBLANES - 2 + k), R), :]
        gz = jnp.dot(xr.astype(BF16), wg_ref[d], preferred_element_type=F32) + bg_ref[d]
        r = jax.nn.sigmoid(gz[:, :LANES])
        i = jax.nn.sigmoid(gz[:, LANES:])
        log_a = c8[d] * r
        a = jnp.exp(log_a)
        b = jnp.sqrt((1.0 + a * a) * jnp.tanh(-log_a)) * (i * xr)
        return a, b

    def chunk(c, carry):
        t0f = pl.multiple_of(c * R, R)
        t0b = pl.multiple_of((nchunk - 1 - c) * R, R)
        a, b = prep(t0f, 0)
        for s in (1, 2, 4):
            a_sh = pltpu.roll(a, s, 0)
            b_sh = pltpu.roll(b, s, 0)
            m = sub >= s
            b = jnp.where(m, a * b_sh + b, b)
            a = jnp.where(m, a * a_sh, a)
        af_ref[...] = a
        bf_ref[...] = b
        a, b = prep(t0b, 1)
        for s in (1, 2, 4):
            a_sh = pltpu.roll(a, R - s, 0)
            b_sh = pltpu.roll(b, R - s, 0)
            m = sub < SUBLANES - s
            b = jnp.where(m, a * b_sh + b, b)
            a = jnp.where(m, a * a_sh, a)
        ab_ref[...] = a
        bb_ref[...] = b

        def group(j, hc):
            hf, hk = hc
            rf = pl.multiple_of(j * SUBLANES, SUBLANES)
            h8 = bf_ref[pl.ds(rf, SUBLANES), :] + af_ref[pl.ds(rf, SUBLANES), :] * hf
            o_ref[pl.ds(t0f + rf, SUBLANES), :] = h8
            hf = jnp.broadcast_to(h8[SUBLANES - 1:SUBLANES, :], (SUBLANES, LANES))
            rb = pl.multiple_of((ng - 1 - j) * SUBLANES, SUBLANES)
            k8 = bb_ref[pl.ds(rb, SUBLANES), :] + ab_ref[pl.ds(rb, SUBLANES), :] * hk
            hb_ref[pl.ds(t0b + rb, SUBLANES), :] = k8
            hk = jnp.broadcast_to(k8[0:1, :], (SUBLANES, LANES))
            return hf, hk

        return lax.fori_loop(0, ng, group, carry, unroll=8)

    lax.fori_loop(0, nchunk, chunk, (zeros8, zeros8))

    def fin(c, _):
        t0 = pl.multiple_of(c * R, R)
        o_ref[pl.ds(t0, R), :] = g_ref[pl.ds(t0, R), :] * (o_ref[pl.ds(t0, R), :] + hb_ref[pl.ds(t0, R), :])
        return 0

    lax.fori_loop(0, nchunk, fin, 0)


def _lru(p3, conv_w, conv_b, wa, ba, wi, bi, lam, R):
    B, L, _ = p3.shape
    D = conv_w.shape[1]
    nt = D // LANES
    hpt = LANES // RNN_BLOCK

    def blockdiag(w):
        w = w.reshape(2, nt, hpt, RNN_BLOCK, RNN_BLOCK)
        eye = jnp.eye(hpt, dtype=w.dtype)
        return jnp.einsum('dthij,hg->dthigj', w, eye).reshape(2, nt, LANES, LANES)

    wg = jnp.concatenate([blockdiag(wa), blockdiag(wi)], axis=-1).astype(BF16)
    bg = jnp.concatenate([ba.reshape(2, nt, 1, LANES), bi.reshape(2, nt, 1, LANES)], axis=-1)
    lam4 = lam.reshape(2, nt, 1, LANES)
    return pl.pallas_call(
        functools.partial(_lru_kernel, L=L, R=R),
        grid=(B, nt),
        in_specs=[
            pl.BlockSpec((None, L, LANES), lambda b, c: (b, 0, c)),
            pl.BlockSpec((None, L, LANES), lambda b, c: (b, 0, nt + c)),
            pl.BlockSpec((4, LANES), lambda b, c: (0, c)),
            pl.BlockSpec((1, LANES), lambda b, c: (0, c)),
            pl.BlockSpec((2, None, LANES, 2 * LANES), lambda b, c: (0, c, 0, 0)),
            pl.BlockSpec((2, None, 1, 2 * LANES), lambda b, c: (0, c, 0, 0)),
            pl.BlockSpec((2, None, 1, LANES), lambda b, c: (0, c, 0, 0)),
        ],
        out_specs=pl.BlockSpec((None, L, LANES), lambda b, c: (b, 0, c)),
        out_shape=jax.ShapeDtypeStruct((B, L, D), F32),
        scratch_shapes=[
            pltpu.VMEM((L + 2 * SUBLANES, LANES), F32),
            pltpu.VMEM((L, LANES), F32),
            pltpu.VMEM((R, LANES), F32), pltpu.VMEM((R, LANES), F32),
            pltpu.VMEM((R, LANES), F32), pltpu.VMEM((R, LANES), F32),
        ],
        compiler_params=_cparams(("parallel", "parallel")),
    )(p3, p3, conv_w, conv_b.reshape(1, D), wg, bg, lam4)


def _hy_pre_kernel(v_ref, x1_ref, x2_ref, vp_ref, x1p_ref, x2p_ref, vn_ref, x1n_ref, x2n_ref,
                   w_ref, b_ref, z_ref, x1c_ref, *, nt):
    i = pl.program_id(1)
    tl = v_ref.shape[0]
    row = lax.broadcasted_iota(jnp.int32, v_ref.shape, 0)
    first = i == 0
    last = i == nt - 1

    def conv(c, m_ref, p_ref, n_ref):
        x = m_ref[...]
        prev = jnp.where(first, 0.0, p_ref[SUBLANES - 1:SUBLANES, :])
        nxt = jnp.where(last, 0.0, n_ref[0:1, :])
        xm = jnp.where(row == 0, prev, pltpu.roll(x, 1, 0))
        xn = jnp.where(row == tl - 1, nxt, pltpu.roll(x, tl - 1, 0))
        return b_ref[c] + w_ref[c, 0:1, :] * xm + w_ref[c, 1:2, :] * x + w_ref[c, 2:3, :] * xn

    v = conv(0, v_ref, vp_ref, vn_ref)
    x1c_ref[...] = conv(1, x1_ref, x1p_ref, x1n_ref)
    z_ref[...] = conv(2, x2_ref, x2p_ref, x2n_ref) * v


def _hy_pre(p3, conv_w, conv_b, D, col0, tl, ct):
    B, L, n_in = p3.shape
    nt = L // tl
    ncb = D // ct
    g8 = tl // SUBLANES
    p4 = p3.reshape(B, L // SUBLANES, SUBLANES, n_in)
    w3 = conv_w.reshape(3, 3, D).transpose(1, 0, 2)
    b3 = conv_b.reshape(3, 1, D)
    cb0 = col0 // ct

    def main(part):
        return pl.BlockSpec((None, tl, ct), lambda b, i, c: (b, i, cb0 + part * ncb + c))

    def prev(part):
        return pl.BlockSpec((None, None, SUBLANES, ct),
                            lambda b, i, c: (b, jnp.maximum(i * g8 - 1, 0), 0, cb0 + part * ncb + c))

    def nxt(part):
        return pl.BlockSpec((None, None, SUBLANES, ct),
                            lambda b, i, c: (b, jnp.minimum((i + 1) * g8, L // SUBLANES - 1), 0, cb0 + part * ncb + c))

    out = pl.BlockSpec((None, tl, ct), lambda b, i, c: (b, i, c))
    return pl.pallas_call(
        functools.partial(_hy_pre_kernel, nt=nt),
        grid=(B, nt, ncb),
        in_specs=[main(0), main(1), main(2), prev(0), prev(1), prev(2), nxt(0), nxt(1), nxt(2),
                  pl.BlockSpec((3, 3, ct), lambda b, i, c: (0, 0, c)),
                  pl.BlockSpec((3, 1, ct), lambda b, i, c: (0, 0, c))],
        out_specs=[out, out],
        out_shape=[jax.ShapeDtypeStruct((B, L, D), F32)] * 2,
        compiler_params=_cparams(("parallel", "parallel", "parallel")),
    )(p3, p3, p3, p4, p4, p4, p4, p4, p4, w3, b3)


def _filt_kernel(feat_ref, t_ref, w1_ref, b1_ref, w2_ref, b2_ref, w3_ref, b3_ref, w4f_ref, w4b_ref, fr_ref, dl_ref,
                 k_ref, s_ref, *, nb):
    i = pl.program_id(0)
    hp = lax.Precision.HIGHEST
    fr = fr_ref[...]
    fwd = i < nb
    hh = jnp.sin(fr * (jnp.dot(feat_ref[...], w1_ref[...], precision=hp, preferred_element_type=F32) + b1_ref[...]))
    hh = jnp.sin(fr * (jnp.dot(hh, w2_ref[...], precision=hp, preferred_element_type=F32) + b2_ref[...]))
    hh = jnp.sin(fr * (jnp.dot(hh, w3_ref[...], precision=hp, preferred_element_type=F32) + b3_ref[...]))
    w4 = jnp.where(fwd, w4f_ref[...], w4b_ref[...])
    decay = jnp.exp(-t_ref[...] * jnp.abs(dl_ref[...]))
    k = jnp.dot(hh, w4, precision=hp, preferred_element_type=F32) * decay
    row = lax.broadcasted_iota(jnp.int32, k.shape, 0)
    kb0 = (jnp.dot(hh[0:SUBLANES], w4b_ref[...], precision=hp, preferred_element_type=F32) * decay[0:SUBLANES])[0:1]
    k = jnp.where(jnp.logical_and(row == 0, i == 0), k + kb0, k)
    k = jnp.where(jnp.logical_and(row == 0, i == nb), 0.0, k)
    k_ref[...] = k
    part = jnp.sum(jnp.abs(k), axis=0, keepdims=True)

    @pl.when(i == 0)
    def _():
        s_ref[...] = part

    @pl.when(i > 0)
    def _():
        s_ref[...] += part


def _hyena_filter(L, D, w1, b1, w2, b2, w3, b3, w4, freq, tl):
    t = jnp.linspace(0.0, 1.0, L, dtype=F32)[:, None]
    w = 2.0 * math.pi * jnp.arange(L, dtype=F32)[:, None] / L
    bands = jnp.linspace(1e-4, HY_BANDS - 1, HY_BANDS, dtype=F32)[None, :]
    feats = jnp.concatenate([t, jnp.cos(w * bands), -jnp.sin(w * bands)], axis=-1)
    deltas = jnp.linspace(math.log(HY_TARGET) / HY_SLOW_DECAY, math.log(HY_TARGET) / HY_FAST_DECAY, D, dtype=F32)
    rev = lambda a: jnp.roll(jnp.flip(a, 0), 1, 0)
    feats2 = jnp.concatenate([feats, rev(feats)], axis=0)
    t2 = jnp.concatenate([t, rev(t)], axis=0)
    ff = w1.shape[1]
    nb = L // tl
    full = lambda shape: pl.BlockSpec(shape, lambda i: (0,) * len(shape))
    return pl.pallas_call(
        functools.partial(_filt_kernel, nb=nb),
        grid=(2 * nb,),
        in_specs=[pl.BlockSpec((tl, HY_EMB), lambda i: (i, 0)), pl.BlockSpec((tl, 1), lambda i: (i, 0)),
                  full((HY_EMB, ff)), full((1, ff)), full((ff, ff)), full((1, ff)), full((ff, ff)), full((1, ff)),
                  full((ff, D)), full((ff, D)), full((1, ff)), full((1, D))],
        out_specs=[pl.BlockSpec((tl, D), lambda i: (i, 0)), pl.BlockSpec((1, D), lambda i: (0, 0))],
        out_shape=[jax.ShapeDtypeStruct((2 * L, D), F32), jax.ShapeDtypeStruct((1, D), F32)],
        compiler_params=_cparams(("arbitrary",)),
    )(feats2, t2, w1, b1.reshape(1, ff), w2, b2.reshape(1, ff), w3, b3.reshape(1, ff), w4[:, :D], w4[:, D:],
      freq.reshape(1, ff), deltas.reshape(1, D))


def _angles(rows, cols, n):
    r = (rows * cols) % n
    th = r.astype(F32) * (2.0 * math.pi / n)
    return jnp.cos(th), jnp.sin(th)


def _dft_mats(N1, N2):
    N = N1 * N2
    h = N1 // 2
    ar = lambda n: jnp.arange(n, dtype=jnp.int32)
    def add(c1, s1, c2, s2):
        return c1 * c2 - s1 * s2, s1 * c2 + c1 * s2

    cp, sp = _angles(ar(N1)[:, None], ar(N1)[None, :], N1)
    cq, sq = _angles(ar(N2)[:, None], ar(N1)[None, :], N)
    cf, sf = add(cp[None], sp[None], cq[:, :, None], sq[:, :, None])
    c, s = cf[:, :, :h], sf[:, :, :h]
    a_data = jnp.concatenate([jnp.concatenate([c, s], -1), jnp.concatenate([-s, c], -1)], 1)
    a_filt = jnp.concatenate([cf, -sf], 1)
    c, s = _angles(ar(N2)[:, None], ar(N2)[None, :], N2)
    m2 = jnp.concatenate([jnp.concatenate([c, s], -1), jnp.concatenate([-s, c], -1)], 0)
    cr, sr = _angles(ar(N1)[:, None], ar(N2)[None, :], N)
    cs, ss = _angles(ar(N2)[:, None], ar(N2)[None, :], N2)
    c, s = add(cr[:, :, None], sr[:, :, None], cs[None], ss[None])
    hm =jnp.concatenate([jnp.concatenate([c, -s], -1), jnp.concatenate([s, c], -1)], 1)
    c, s = _angles(ar(h)[:, None], ar(N1)[None, :], N1)
    i2 = jnp.concatenate([jnp.concatenate([c, -s], -1), jnp.concatenate([s, c], -1)], 0) / N
    return (a_data.astype(BF16), a_filt.astype(BF16), m2.astype(BF16), hm.astype(BF16), i2.astype(BF16))


def _fft1_kernel(*refs, scaled):
    a_ref = refs[0]
    inv_s = refs[1][...] if scaled else None
    x_ref, o_ref = refs[-2], refs[-1]
    nin = x_ref.shape[0]
    n1 = o_ref.shape[1]
    for j in range(SUBLANES):
        rows = [x_ref[q, :, j, :] for q in range(nin)]
        x = rows[0] if nin == 1 else jnp.concatenate(rows, axis=0)
        if scaled:
            x = x * inv_s
        y = jnp.dot(a_ref[j], x.astype(BF16), preferred_element_type=F32)
        o_ref[0, :, j, :] = y[:n1]
        o_ref[1, :, j, :] = y[n1:]


def _fft1_data(z5, a_data, dt):
    _, P, h, N2, D = z5.shape
    N1 = 2 * h
    return pl.pallas_call(
        functools.partial(_fft1_kernel, scaled=False),
        grid=(N2 // SUBLANES, P, D // dt),
        in_specs=[pl.BlockSpec((SUBLANES, 2 * N1, N1), lambda j, p, c: (j, 0, 0)),
                  pl.BlockSpec((2, None, h, SUBLANES, dt), lambda j, p, c: (0, p, 0, j, c))],
        out_specs=pl.BlockSpec((None, 2, N1, SUBLANES, dt), lambda j, p, c: (p, 0, 0, j, c)),
        out_shape=jax.ShapeDtypeStruct((P, 2, N1, N2, D), F32),
        compiler_params=_cparams(("parallel", "parallel", "parallel")),
    )(a_data, z5)


def _fft1_filt(k4, inv_s, a_filt, dt):
    _, N1, N2, D = k4.shape
    return pl.pallas_call(
        functools.partial(_fft1_kernel, scaled=True),
        grid=(N2 // SUBLANES, D // dt),
        in_specs=[pl.BlockSpec((SUBLANES, 2 * N1, N1), lambda j, c: (j, 0, 0)),
                  pl.BlockSpec((1, dt), lambda j, c: (0, c)),
                  pl.BlockSpec((1, N1, SUBLANES, dt), lambda j, c: (0, 0, j, c))],
        out_specs=pl.BlockSpec((2, N1, SUBLANES, dt), lambda j, c: (0, 0, j, c)),
        out_shape=jax.ShapeDtypeStruct((2, N1, N2, D), F32),
        compiler_params=_cparams(("parallel", "parallel")),
    )(a_filt, inv_s, k4)


def _fft2_filt_kernel(m2_ref, y_ref, o_ref):
    n2 = y_ref.shape[1]
    y = jnp.concatenate([y_ref[0], y_ref[1]], axis=0).astype(BF16)
    x = jnp.dot(m2_ref[...], y, preferred_element_type=F32)
    o_ref[0] = x[:n2]
    o_ref[1] = x[n2:]


def _fft2_filt(yk, m2, dt):
    _, N1, N2, D = yk.shape
    return pl.pallas_call(
        _fft2_filt_kernel,
        grid=(N1, D // dt),
        in_specs=[pl.BlockSpec((2 * N2, 2 * N2), lambda k, c: (0, 0)),
                  pl.BlockSpec((2, None, N2, dt), lambda k, c: (0, k, 0, c))],
        out_specs=pl.BlockSpec((2, None, N2, dt), lambda k, c: (0, k, 0, c)),
        out_shape=jax.ShapeDtypeStruct((2, N1, N2, D), F32),
        compiler_params=_cparams(("parallel", "parallel")),
    )(m2, yk)


def _fft_mid_kernel(m2_ref, h_ref, y_ref, k_ref, o_ref):
    n2 = y_ref.shape[1]
    y = jnp.concatenate([y_ref[0], y_ref[1]], axis=0).astype(BF16)
    x = jnp.dot(m2_ref[...], y, preferred_element_type=F32)
    xr, xi = x[:n2], x[n2:]
    kr, ki = k_ref[0], k_ref[1]
    pr = xr * kr - xi * ki
    pi = xr * ki + xi * kr
    p = jnp.concatenate([pr, pi], axis=0).astype(BF16)
    u = jnp.dot(h_ref[...], p, preferred_element_type=F32)
    o_ref[0] = u[:n2]
    o_ref[1] = u[n2:]


def _fft_mid(y, kf, m2, hm, dt):
    P, _, N1, N2, D = y.shape
    return pl.pallas_call(
        _fft_mid_kernel,
        grid=(N1, D // dt, P),
        in_specs=[pl.BlockSpec((2 * N2, 2 * N2), lambda k, c, p: (0, 0)),
                  pl.BlockSpec((None, 2 * N2, 2 * N2), lambda k, c, p: (k, 0, 0)),
                  pl.BlockSpec((None, 2, None, N2, dt), lambda k, c, p: (p, 0, k, 0, c)),
                  pl.BlockSpec((2, None, N2, dt), lambda k, c, p: (0, k, 0, c))],
        out_specs=pl.BlockSpec((None, 2, None, N2, dt), lambda k, c, p: (p, 0, k, 0, c)),
        out_shape=jax.ShapeDtypeStruct((P, 2, N1, N2, D), F32),
        compiler_params=_cparams(("parallel", "parallel", "parallel")),
    )(m2, hm, y, kf)


def _fft_inv_kernel(i2_ref, u_ref, z_ref, x1_ref, bias_ref, o_ref):
    h = o_ref.shape[1]
    bias = bias_ref[...]
    for j in range(SUBLANES):
        u = jnp.concatenate([u_ref[0, :, j, :], u_ref[1, :, j, :]], axis=0).astype(BF16)
        y = jnp.dot(i2_ref[...], u, preferred_element_type=F32)
        o_ref[0, :, j, :] = x1_ref[0, :, j, :] * (y[:h] + bias * z_ref[0, :, j, :])
        o_ref[1, :, j, :] = x1_ref[1, :, j, :] * (y[h:] + bias * z_ref[1, :, j, :])


def _fft_inv(u, z5, x15, bias, i2, dt):
    P, _, N1, N2, D = u.shape
    h = N1 // 2
    pair = pl.BlockSpec((2, None, h, SUBLANES, dt), lambda j, p, c: (0, p, 0, j, c))
    return pl.pallas_call(
        _fft_inv_kernel,
        grid=(N2 // SUBLANES, P, D // dt),
        in_specs=[pl.BlockSpec((N1, 2 * N1), lambda j, p, c: (0, 0)),
                  pl.BlockSpec((None, 2, N1, SUBLANES, dt), lambda j, p, c: (p, 0, 0, j, c)),
                  pair, pair,
                  pl.BlockSpec((1, dt), lambda j, p, c: (0, c))],
        out_specs=pair,
        out_shape=jax.ShapeDtypeStruct((2, P, h, N2, D), F32),
        compiler_params=_cparams(("parallel", "parallel", "parallel")),
    )(i2, u, z5, x15, bias)


def _hyena(p3, D, col0, conv_w, conv_b, w1, b1, w2, b2, w3, b3, w4, freq, bias, tl, ct, dt):
    B, L, _ = p3.shape
    N2 = FFT_N2
    N1 = 2 * L // N2
    h = N1 // 2
    z, x1c = _hy_pre(p3, conv_w, conv_b, D, col0, tl, ct)
    k_circ, s = _hyena_filter(L, D, w1, b1, w2, b2, w3, b3, w4, freq, min(tl, 512))
    a_data, a_filt, m2, hm, i2 = _dft_mats(N1, N2)
    kspec = _fft2_filt(_fft1_filt(k_circ.reshape(1, N1, N2, D), 1.0 / s, a_filt, dt), m2, dt)
    z5 = z.reshape(2, B // 2, h, N2, D)
    y = _fft1_data(z5, a_data, dt)
    u = _fft_mid(y, kspec, m2, hm, dt)
    out = _fft_inv(u, z5, x1c.reshape(2, B // 2, h, N2, D), bias.reshape(1, D), i2, dt)
    return out.reshape(B * L, D)


def _merge_kernel(x_ref, yl_ref, yh_ref, gl_ref, gh_ref, wl_ref, wh_ref, wo_ref, gf_ref, wr_ref, br_ref,
                  h_ref, hn_ref, idx_ref, gate_ref):
    a = jnp.dot(yl_ref[...].astype(BF16), wl_ref[...], preferred_element_type=F32)
    b = jnp.dot(yh_ref[...].astype(BF16), wh_ref[...], preferred_element_type=F32)
    mix = gl_ref[...] * a + gh_ref[...] * b
    h = x_ref[...] + jnp.dot(mix.astype(BF16), wo_ref[...], preferred_element_type=F32)
    h_ref[...] = h
    ms = jnp.mean(h * h, axis=-1, keepdims=True)
    hn = h * lax.rsqrt(ms + EPS) * gf_ref[...]
    nc = hn.shape[1] // LANES
    for c in range(nc):
        hn_ref[pl.ds(c, hn.shape[0], stride=nc), :] = hn[:, c * LANES:(c + 1) * LANES]
    ne = br_ref.shape[1]
    h_hi = hn.astype(BF16)
    h_lo = (hn - h_hi.astype(F32)).astype(BF16)
    a2 = jnp.dot(h_hi, wr_ref[...], preferred_element_type=F32)
    b2 = jnp.dot(h_lo, wr_ref[:, :LANES], preferred_element_type=F32)
    logits = (a2[:, :LANES] + a2[:, LANES:] + b2)[:, :ne] + br_ref[...]
    lane = lax.broadcasted_iota(jnp.int32, logits.shape, 1)
    vals = []
    for k in range(TOP_K):
        m = jnp.max(logits, axis=-1, keepdims=True)
        ix = jnp.min(jnp.where(logits == m, lane, ne), axis=-1, keepdims=True)
        vals.append(m)
        idx_ref[:, k:k + 1] = ix
        logits = jnp.where(lane == ix, -jnp.inf, logits)
    ex = [jnp.exp(v - vals[0]) for v in vals]
    tot = ex[0] + ex[1] + ex[2] + ex[3]
    for k in range(TOP_K):
        gate_ref[:, k:k + 1] = ex[k] / tot


def _merge(x2, y_lru, y_hy, p_all, w_lru_br, w_hy_br, w_out, g_ffn, w_router, b_router, tm):
    T, D = x2.shape
    ne = w_router.shape[1]
    gcol = p_all.shape[1] // D - 2
    row = lambda i: (i, 0)
    full = lambda i: (0, 0)
    w_hi = w_router.astype(BF16)
    w_lo = (w_router - w_hi.astype(F32)).astype(BF16)
    lane_pad = lambda w: jnp.pad(w, ((0, 0), (0, LANES - ne)))
    wr2 = jnp.concatenate([lane_pad(w_hi), lane_pad(w_lo)], axis=1)
    return pl.pallas_call(
        _merge_kernel,
        grid=(T // tm,),
        in_specs=[pl.BlockSpec((tm, D), row), pl.BlockSpec((tm, D), row), pl.BlockSpec((tm, D), row),
                  pl.BlockSpec((tm, D), lambda i: (i, gcol)), pl.BlockSpec((tm, D), lambda i: (i, gcol + 1)),
                  pl.BlockSpec((D, D), full), pl.BlockSpec((D, D), full), pl.BlockSpec((D, D), full),
                  pl.BlockSpec((1, D), full), pl.BlockSpec((D, 2 * LANES), full), pl.BlockSpec((1, ne), full)],
        out_specs=[pl.BlockSpec((tm, D), row), pl.BlockSpec((tm * (D // LANES), LANES), row),
                   pl.BlockSpec((tm, TOP_K), row), pl.BlockSpec((tm, TOP_K), row)],
        out_shape=[jax.ShapeDtypeStruct((T, D), F32), jax.ShapeDtypeStruct((T * (D // LANES), LANES), F32),
                   jax.ShapeDtypeStruct((T, TOP_K), jnp.int32), jax.ShapeDtypeStruct((T, TOP_K), F32)],
        compiler_params=_cparams(("parallel",)),
    )(x2, y_lru, y_hy, p_all, p_all, w_lru_br.astype(BF16), w_hy_br.astype(BF16), w_out.astype(BF16),
      g_ffn.reshape(1, D), wr2, b_router.reshape(1, ne))


ROW_DMA_UNROLL = 8


def _for_rows(n, fn):
    def body(g, c):
        for k in range(ROW_DMA_UNROLL):
            fn(g * ROW_DMA_UNROLL + k, k % 2)
        return c

    lax.fori_loop(0, n // ROW_DMA_UNROLL, body, 0)


def _moe_kernel(ib_ref, ie_ref, bnd_ref, ni_ref, ord_ref, nord_ref, hn_hbm, gate_ref, wgu_ref, bgu_ref,
                wd_ref, bd_ref, yk_hbm, xbuf, ybuf, xs_ref, gsem, ssem, *, nblk):
    i = pl.program_id(0)
    ni = ni_ref[0]
    nrow = xs_ref.shape[0]
    nc = xbuf.shape[1] // nrow
    dff = wd_ref.shape[0]

    def gather_copy(idx_ref, r, slot):
        tok = lax.shift_right_logical(idx_ref[0, 0, r], 2)
        return pltpu.make_async_copy(hn_hbm.at[tok], xbuf.at[slot, pl.ds(r * nc, nc), :], gsem.at[slot])

    def scatter_copy(r, slot):
        return pltpu.make_async_copy(ybuf.at[slot, pl.ds(r * nc, nc), :], yk_hbm.at[ord_ref[0, 0, r]],
                                     ssem.at[slot])

    @pl.when(i < ni)
    def _():
        b = ib_ref[i]
        e = ie_ref[i]
        slot = b % 2
        first = jnp.logical_or(i == 0, ib_ref[jnp.maximum(i - 1, 0)] != b)
        last = jnp.logical_or(i == ni - 1, ib_ref[jnp.minimum(i + 1, pl.num_programs(0) - 1)] != b)

        @pl.when(i == 0)
        def _():
            _for_rows(nrow, lambda r, p: gather_copy(ord_ref, r, 0).start(priority=p))

        @pl.when(first)
        def _():
            @pl.when(b + 1 < nblk)
            def _():
                _for_rows(nrow, lambda r, p: gather_copy(nord_ref, r, 1 - slot).start(priority=p))

            _for_rows(nrow, lambda r, p: gather_copy(ord_ref, r, slot).wait())

            @pl.when(b >= 2)
            def _():
                _for_rows(nrow, lambda r, p: scatter_copy(r, slot).wait())

        for c in range(nc):
            xs_ref[:, c * LANES:(c + 1) * LANES] = xbuf[slot, pl.ds(c, nrow, stride=nc), :].astype(BF16)
        gu = jnp.dot(xs_ref[...], wgu_ref[...], preferred_element_type=F32) + bgu_ref[...]
        g = jnp.minimum(gu[:, :dff], SWIGLU_LIMIT)
        u = jnp.clip(gu[:, dff:], -SWIGLU_LIMIT, SWIGLU_LIMIT)
        act = (u + 1.0) * (g * jax.nn.sigmoid(SWIGLU_ALPHA * g))
        y = (jnp.dot(act.astype(BF16), wd_ref[...], preferred_element_type=F32) + bd_ref[...]) * gate_ref[...]
        row = lax.broadcasted_iota(jnp.int32, (nrow, 1), 0) + b * nrow
        mine = jnp.logical_and(row >= bnd_ref[e], row < bnd_ref[e + 1])

        @pl.when(first)
        def _():
            for c in range(nc):
                ybuf[slot, pl.ds(c, nrow, stride=nc), :] = jnp.where(mine, y[:, c * LANES:(c + 1) * LANES], 0.0)

        @pl.when(jnp.logical_not(first))
        def _():
            for c in range(nc):
                rows = pl.ds(c, nrow, stride=nc)
                ybuf[slot, rows, :] = jnp.where(mine, y[:, c * LANES:(c + 1) * LANES], ybuf[slot, rows, :])

        @pl.when(last)
        def _():
            _for_rows(nrow, lambda r, p: scatter_copy(r, slot).start(priority=p))

        @pl.when(i == ni - 1)
        def _():
            _for_rows(nrow, lambda r, p: scatter_copy(r, slot).wait())

            @pl.when(b >= 1)
            def _():
                _for_rows(nrow, lambda r, p: scatter_copy(r, 1 - slot).wait())


def _moe_items(hn, order, g_sorted, item_blk, item_exp, bounds, n_items, w_gate_up, b_gate_up, w_down, b_down):
    T, nc, _ = hn.shape
    D = nc * LANES
    N = order.shape[0]
    nblk = N // MOE_BLOCK
    n_steps = item_blk.shape[0]
    dff = w_down.shape[1]
    order3 = order.reshape(nblk, 1, MOE_BLOCK)
    grid_spec = pltpu.PrefetchScalarGridSpec(
        num_scalar_prefetch=4,
        grid=(n_steps,),
        in_specs=[
            pl.BlockSpec((1, 1, MOE_BLOCK), lambda i, ib, ie, bd, ni: (ib[i], 0, 0), memory_space=pltpu.SMEM),
            pl.BlockSpec((1, 1, MOE_BLOCK), lambda i, ib, ie, bd, ni: (jnp.minimum(ib[i] + 1, nblk - 1), 0, 0),
                         memory_space=pltpu.SMEM),
            pl.BlockSpec(memory_space=pl.ANY),
            pl.BlockSpec((MOE_BLOCK, 1), lambda i, ib, ie, bd, ni: (ib[i], 0)),
            pl.BlockSpec((None, D, 2 * dff), lambda i, ib, ie, bd, ni: (ie[i], 0, 0)),
            pl.BlockSpec((None, 1, 2 * dff), lambda i, ib, ie, bd, ni: (ie[i], 0, 0)),
            pl.BlockSpec((None, dff, D), lambda i, ib, ie, bd, ni: (ie[i], 0, 0)),
            pl.BlockSpec((None, 1, D), lambda i, ib, ie, bd, ni: (ie[i], 0, 0)),
        ],
        out_specs=pl.BlockSpec(memory_space=pl.ANY),
        scratch_shapes=[pltpu.VMEM((2, MOE_BLOCK * nc, LANES), F32), pltpu.VMEM((2, MOE_BLOCK * nc, LANES), F32),
                        pltpu.VMEM((MOE_BLOCK, D), BF16),
                        pltpu.SemaphoreType.DMA((2,)), pltpu.SemaphoreType.DMA((2,))],
    )
    return pl.pallas_call(
        functools.partial(_moe_kernel, nblk=nblk),
        grid_spec=grid_spec,
        out_shape=jax.ShapeDtypeStruct((N, nc, LANES), F32),
        compiler_params=_cparams(("arbitrary",)),
    )(item_blk, item_exp, bounds, n_items, order3, order3, hn, g_sorted.reshape(N, 1),
      w_gate_up.astype(BF16), b_gate_up.reshape(N_EXPERTS, 1, 2 * dff), w_down.astype(BF16),
      b_down.reshape(N_EXPERTS, 1, D))


def _combine_kernel(yk_ref, h_ref, g_ref, o_ref, ys_ref, hs_ref):
    tq = h_ref.shape[0]
    nc = yk_ref.shape[1] // TOP_K
    y = yk_ref[:, 0:nc, :]
    for k in range(1, TOP_K):
        y = y + yk_ref[:, k * nc:(k + 1) * nc, :]
    ys_ref[...] = y.reshape(tq * nc, LANES)
    for c in range(nc):
        hs_ref[:, c * LANES:(c + 1) * LANES] = (h_ref[:, c * LANES:(c + 1) * LANES]
                                                + ys_ref[pl.ds(c, tq, stride=nc), :])
    h = hs_ref[...]
    ms = jnp.mean(h * h, axis=-1, keepdims=True)
    o_ref[...] = h * lax.rsqrt(ms + EPS) * g_ref[...]


def _combine(yk, h, g_final, tq):
    T, D = h.shape
    nc = yk.shape[1]
    return pl.pallas_call(
        _combine_kernel,
        grid=(T // tq,),
        in_specs=[pl.BlockSpec((tq, TOP_K * nc, LANES), lambda i: (i, 0, 0)),
                  pl.BlockSpec((tq, D), lambda i: (i, 0)),
                  pl.BlockSpec((1, D), lambda i: (0, 0))],
        out_specs=pl.BlockSpec((tq, D), lambda i: (i, 0)),
        out_shape=jax.ShapeDtypeStruct((T, D), F32),
        scratch_shapes=[pltpu.VMEM((tq * nc, LANES), F32), pltpu.VMEM((tq, D), F32)],
        compiler_params=_cparams(("parallel",)),
    )(yk.reshape(T, TOP_K * nc, LANES), h, g_final.reshape(1, D))


def _route(top_idx, gates):
    T = top_idx.shape[0]
    N = T * TOP_K
    nblk = N // MOE_BLOCK
    i32 = jnp.int32
    e_sorted, order, g_sorted = lax.sort(
        (top_idx.reshape(N), jnp.arange(N, dtype=i32), gates.reshape(N)), num_keys=1)
    bounds = jnp.sum(e_sorted[None, :] < jnp.arange(N_EXPERTS + 1, dtype=i32)[:, None], axis=1, dtype=i32)
    starts, ends = bounds[:-1], bounds[1:]
    first_blk = starts // MOE_BLOCK
    n_of = jnp.where(ends > starts, (ends - 1) // MOE_BLOCK - first_blk + 1, 0)
    item_end = jnp.cumsum(n_of)
    item_start = item_end - n_of
    n_items = item_end[-1:]
    n_steps = nblk + N_EXPERTS - 1
    i = jnp.arange(n_steps, dtype=i32)
    ie = jnp.minimum(jnp.searchsorted(item_end, i, side='right'), N_EXPERTS - 1).astype(i32)
    ib = jnp.clip(first_blk[ie] + i - item_start[ie], 0, nblk - 1).astype(i32)
    return order, g_sorted, ib, ie, bounds, n_items.astype(i32)


def kernel(x, g_mix, w_in, b_in, lru_conv_w, lru_conv_b, lru_wa, lru_ba, lru_wi, lru_bi, lru_lambda, w_lru_br, hy_conv_w, hy_conv_b, hy_w1, hy_b1, hy_w2, hy_b2, hy_w3, hy_b3, hy_w4, hy_sin_freq, hy_bias, w_hy_br, w_out, g_ffn, w_router, b_router, w_gate_up, b_gate_up, w_down, b_down, g_final):
    B, L, D = x.shape
    T = B * L
    x2 = x.reshape(T, D)
    tm = min(1024, T)
    p_all = _inproj(x2, g_mix, w_in, b_in, tm)
    p3 = p_all.reshape(B, L, p_all.shape[1])
    y_lru = _lru(p3, lru_conv_w, lru_conv_b, lru_wa, lru_ba, lru_wi, lru_bi, lru_lambda, min(512, L))
    y_hy = _hyena(p3, D, 2 * D, hy_conv_w, hy_conv_b, hy_w1, hy_b1, hy_w2, hy_b2, hy_w3, hy_b3, hy_w4,
                  hy_sin_freq, hy_bias, min(1024, L), min(256, D), min(512, D))
    h, hn, top_idx, gates = _merge(x2, y_lru.reshape(T, D), y_hy, p_all, w_lru_br, w_hy_br, w_out,
                                   g_ffn, w_router, b_router, min(512, T))
    order, g_sorted, item_blk, item_exp, bounds, n_items = _route(top_idx, gates)
    yk = _moe_items(hn.reshape(T, D // LANES, LANES), order, g_sorted, item_blk, item_exp, bounds, n_items,
                    w_gate_up, b_gate_up, w_down, b_down)
    out = _combine(yk, h, g_final, min(256, T))
    return out.reshape(B, L, D)
```

```python
import functools
import math

import numpy as np
import jax
import jax.numpy as jnp
from jax import lax
from jax.experimental import pallas as pl
from jax.experimental.pallas import tpu as pltpu

F32 = jnp.float32
BF16 = jnp.bfloat16

EPS = 1e-5
C_RGLRU = 8.0
RNN_BLOCK = 64
HY_EMB = 33
HY_BANDS = 16
HY_FAST_DECAY = 0.3
HY_SLOW_DECAY = 1.5
HY_TARGET = 1e-2
N_EXPERTS = 32
TOP_K = 4
SWIGLU_LIMIT = 7.0
SWIGLU_ALPHA = 1.702
MOE_BLOCK = 512

LANES = 128
SUBLANES = 8
FFT_N2 = 128
VMEM_LIMIT = 56 * 1024 * 1024


def _cparams(sem):
    return pltpu.CompilerParams(dimension_semantics=sem, vmem_limit_bytes=VMEM_LIMIT)


def _inproj_kernel(x_ref, g_ref, w_ref, b_ref, o_ref, u_ref):
    j = pl.program_id(1)

    @pl.when(j == 0)
    def _():
        x = x_ref[...]
        ms = jnp.mean(x * x, axis=-1, keepdims=True)
        u_ref[...] = (x * lax.rsqrt(ms + EPS) * g_ref[...]).astype(BF16)

    def proj():
        return jnp.dot(u_ref[...], w_ref[...], preferred_element_type=F32) + b_ref[...]

    @pl.when(j == 1)
    def _():
        o_ref[...] = jax.nn.gelu(proj())

    @pl.when(j >= 5)
    def _():
        o_ref[...] = jax.nn.sigmoid(proj())

    @pl.when(jnp.logical_and(j != 1, j < 5))
    def _():
        o_ref[...] = proj()


def _inproj(x2, g_mix, w_in, b_in, tm):
    T, D = x2.shape
    n_in = w_in.shape[1]
    return pl.pallas_call(
        _inproj_kernel,
        grid=(T // tm, n_in // D),
        in_specs=[
            pl.BlockSpec((tm, D), lambda i, j: (i, 0)),
            pl.BlockSpec((1, D), lambda i, j: (0, 0)),
            pl.BlockSpec((D, D), lambda i, j: (0, j)),
            pl.BlockSpec((1, D), lambda i, j: (0, j)),
        ],
        out_specs=pl.BlockSpec((tm, D), lambda i, j: (i, j)),
        out_shape=jax.ShapeDtypeStruct((T, n_in), F32),
        scratch_shapes=[pltpu.VMEM((tm, D), BF16)],
        compiler_params=_cparams(("parallel", "arbitrary")),
    )(x2, g_mix.reshape(1, D), w_in.astype(BF16), b_in.reshape(1, n_in))


def _lru_kernel(x_ref, g_ref, cw_ref, cb_ref, wg_ref, bg_ref, lam_ref, o_ref,
                xp_ref, xr_ref, hb_ref, af_ref, bf_ref, ab_ref, bb_ref, *, L, R):
    nchunk = L // R
    ng = R // SUBLANES
    zeros8 = jnp.zeros((SUBLANES, LANES), F32)
    xp_ref[0:SUBLANES, :] = zeros8
    xp_ref[SUBLANES + L:2 * SUBLANES + L, :] = zeros8
    xp_ref[SUBLANES:SUBLANES + L, :] = x_ref[...]

    sub = lax.broadcasted_iota(jnp.int32, (ng, SUBLANES, LANES), 1)
    c8 = [-C_RGLRU * jax.nn.softplus(-lam_ref[d]) for d in range(2)]
    cb = cb_ref[...]
    cw = [cw_ref[k:k + 1, :] for k in range(4)]

    def conv(c, _):
        t0 = pl.multiple_of(c * R, R)
        xr = cb
        for k in range(4):
            xr = xr + cw[k] * xp_ref[pl.ds(t0 + (SUBLANES - 2 + k), R), :]
        xr_ref[pl.ds(t0, R), :] = xr
        return 0

    lax.fori_loop(0, nchunk, conv, 0)

    def prep(t0, d):
        xr = xr_ref[pl.ds(t0, R), :]
        gz = jnp.dot(xr.astype(BF16), wg_ref[d], preferred_element_type=F32) + bg_ref[d]
        r = 0.5 * jnp.tanh(0.5 * gz[:, :LANES]) + 0.5
        i = 0.5 * jnp.tanh(0.5 * gz[:, LANES:]) + 0.5
        log_a = c8[d] * r
        a = jnp.exp(log_a)
        b = jnp.sqrt((1.0 + a * a) * jnp.tanh(-log_a)) * (i * xr)
        return a, b

    def chunk(c, carry):
        t0f = pl.multiple_of(c * R, R)
        t0b = pl.multiple_of((nchunk - 1 - c) * R, R)
        a, b = (v.reshape(ng, SUBLANES, LANES) for v in prep(t0f, 0))
        for s in (1, 2, 4):
            a_sh = pltpu.roll(a, s, 1)
            b_sh = pltpu.roll(b, s, 1)
            m = sub >= s
            b = jnp.where(m, a * b_sh + b, b)
            a = jnp.where(m, a * a_sh, a)
        af_ref[...] = a.reshape(R, LANES)
        bf_ref[...] = b.reshape(R, LANES)
        a, b = (v.reshape(ng, SUBLANES, LANES) for v in prep(t0b, 1))
        for s in (1, 2, 4):
            a_sh = pltpu.roll(a, SUBLANES - s, 1)
            b_sh = pltpu.roll(b, SUBLANES - s, 1)
            m = sub < SUBLANES - s
            b = jnp.where(m, a * b_sh + b, b)
            a = jnp.where(m, a * a_sh, a)
        ab_ref[...] = a.reshape(R, LANES)
        bb_ref[...] = b.reshape(R, LANES)

        def group(j, hc):
            hf, hk = hc
            rf = pl.multiple_of(j * SUBLANES, SUBLANES)
            h8 = bf_ref[pl.ds(rf, SUBLANES), :] + af_ref[pl.ds(rf, SUBLANES), :] * hf
            o_ref[pl.ds(t0f + rf, SUBLANES), :] = h8
            hf = jnp.broadcast_to(h8[SUBLANES - 1:SUBLANES, :], (SUBLANES, LANES))
            rb = pl.multiple_of((ng - 1 - j) * SUBLANES, SUBLANES)
            k8 = bb_ref[pl.ds(rb, SUBLANES), :] + ab_ref[pl.ds(rb, SUBLANES), :] * hk
            hb_ref[pl.ds(t0b + rb, SUBLANES), :] = k8
            hk = jnp.broadcast_to(k8[0:1, :], (SUBLANES, LANES))
            return hf, hk

        return lax.fori_loop(0, ng, group, carry, unroll=8)

    lax.fori_loop(0, nchunk, chunk, (zeros8, zeros8))

    def fin(c, _):
        t0 = pl.multiple_of(c * R, R)
        o_ref[pl.ds(t0, R), :] = g_ref[pl.ds(t0, R), :] * (o_ref[pl.ds(t0, R), :] + hb_ref[pl.ds(t0, R), :])
        return 0

    lax.fori_loop(0, nchunk, fin, 0)


def _lru(p3, conv_w, conv_b, wa, ba, wi, bi, lam, R):
    B, L, _ = p3.shape
    D = conv_w.shape[1]
    nt = D // LANES
    hpt = LANES // RNN_BLOCK

    def blockdiag(w):
        w = w.reshape(2, nt, hpt, RNN_BLOCK, RNN_BLOCK)
        eye = jnp.eye(hpt, dtype=w.dtype)
        return jnp.einsum('dthij,hg->dthigj', w, eye).reshape(2, nt, LANES, LANES)

    wg = jnp.concatenate([blockdiag(wa), blockdiag(wi)], axis=-1).astype(BF16)
    bg = jnp.concatenate([ba.reshape(2, nt, 1, LANES), bi.reshape(2, nt, 1, LANES)], axis=-1)
    lam4 = lam.reshape(2, nt, 1, LANES)
    return pl.pallas_call(
        functools.partial(_lru_kernel, L=L, R=R),
        grid=(B, nt),
        in_specs=[
            pl.BlockSpec((None, L, LANES), lambda b, c: (b, 0, c)),
            pl.BlockSpec((None, L, LANES), lambda b, c: (b, 0, nt + c)),
            pl.BlockSpec((4, LANES), lambda b, c: (0, c)),
            pl.BlockSpec((1, LANES), lambda b, c: (0, c)),
            pl.BlockSpec((2, None, LANES, 2 * LANES), lambda b, c: (0, c, 0, 0)),
            pl.BlockSpec((2, None, 1, 2 * LANES), lambda b, c: (0, c, 0, 0)),
            pl.BlockSpec((2, None, 1, LANES), lambda b, c: (0, c, 0, 0)),
        ],
        out_specs=pl.BlockSpec((None, L, LANES), lambda b, c: (b, 0, c)),
        out_shape=jax.ShapeDtypeStruct((B, L, D), F32),
        scratch_shapes=[
            pltpu.VMEM((L + 2 * SUBLANES, LANES), F32),
            pltpu.VMEM((L, LANES), F32),
            pltpu.VMEM((L, LANES), F32),
            pltpu.VMEM((R, LANES), F32), pltpu.VMEM((R, LANES), F32),
            pltpu.VMEM((R, LANES), F32), pltpu.VMEM((R, LANES), F32),
        ],
        compiler_params=_cparams(("parallel", "parallel")),
    )(p3, p3, conv_w, conv_b.reshape(1, D), wg, bg, lam4)


def _hy_pre_kernel(v_ref, x1_ref, x2_ref, vp_ref, x1p_ref, x2p_ref, vn_ref, x1n_ref, x2n_ref,
                   w_ref, b_ref, z_ref, x1c_ref, *, nt):
    i = pl.program_id(1)
    tl = v_ref.shape[0]
    row = lax.broadcasted_iota(jnp.int32, v_ref.shape, 0)
    first = i == 0
    last = i == nt - 1

    def conv(c, m_ref, p_ref, n_ref):
        x = m_ref[...]
        prev = jnp.where(first, 0.0, p_ref[SUBLANES - 1:SUBLANES, :])
        nxt = jnp.where(last, 0.0, n_ref[0:1, :])
        xm = jnp.where(row == 0, prev, pltpu.roll(x, 1, 0))
        xn = jnp.where(row == tl - 1, nxt, pltpu.roll(x, tl - 1, 0))
        return b_ref[c] + w_ref[c, 0:1, :] * xm + w_ref[c, 1:2, :] * x + w_ref[c, 2:3, :] * xn

    v = conv(0, v_ref, vp_ref, vn_ref)
    x1c_ref[...] = conv(1, x1_ref, x1p_ref, x1n_ref)
    z_ref[...] = conv(2, x2_ref, x2p_ref, x2n_ref) * v


def _hy_pre(p3, conv_w, conv_b, D, col0, tl, ct):
    B, L, n_in = p3.shape
    nt = L // tl
    ncb = D // ct
    g8 = tl // SUBLANES
    p4 = p3.reshape(B, L // SUBLANES, SUBLANES, n_in)
    w3 = conv_w.reshape(3, 3, D).transpose(1, 0, 2)
    b3 = conv_b.reshape(3, 1, D)
    cb0 = col0 // ct

    def main(part):
        return pl.BlockSpec((None, tl, ct), lambda b, i, c: (b, i, cb0 + part * ncb + c))

    def prev(part):
        return pl.BlockSpec((None, None, SUBLANES, ct),
                            lambda b, i, c: (b, jnp.maximum(i * g8 - 1, 0), 0, cb0 + part * ncb + c))

    def nxt(part):
        return pl.BlockSpec((None, None, SUBLANES, ct),
                            lambda b, i, c: (b, jnp.minimum((i + 1) * g8, L // SUBLANES - 1), 0, cb0 + part * ncb + c))

    out = pl.BlockSpec((None, tl, ct), lambda b, i, c: (b, i, c))
    return pl.pallas_call(
        functools.partial(_hy_pre_kernel, nt=nt),
        grid=(B, nt, ncb),
        in_specs=[main(0), main(1), main(2), prev(0), prev(1), prev(2), nxt(0), nxt(1), nxt(2),
                  pl.BlockSpec((3, 3, ct), lambda b, i, c: (0, 0, c)),
                  pl.BlockSpec((3, 1, ct), lambda b, i, c: (0, 0, c))],
        out_specs=[out, out],
        out_shape=[jax.ShapeDtypeStruct((B, L, D), F32)] * 2,
        compiler_params=_cparams(("parallel", "parallel", "parallel")),
    )(p3, p3, p3, p4, p4, p4, p4, p4, p4, w3, b3)


def _filt_kernel(feat_ref, t_ref, w1_ref, b1_ref, w2_ref, b2_ref, w3_ref, b3_ref, w4f_ref, w4b_ref, fr_ref, dl_ref,
                 k_ref, s_ref, *, nb):
    i = pl.program_id(0)
    hp = lax.Precision.HIGHEST
    fr = fr_ref[...]
    fwd = i < nb
    hh = jnp.sin(fr * (jnp.dot(feat_ref[...], w1_ref[...], precision=hp, preferred_element_type=F32) + b1_ref[...]))
    hh = jnp.sin(fr * (jnp.dot(hh, w2_ref[...], precision=hp, preferred_element_type=F32) + b2_ref[...]))
    hh = jnp.sin(fr * (jnp.dot(hh, w3_ref[...], precision=hp, preferred_element_type=F32) + b3_ref[...]))
    w4 = jnp.where(fwd, w4f_ref[...], w4b_ref[...])
    decay = jnp.exp(-t_ref[...] * jnp.abs(dl_ref[...]))
    k = jnp.dot(hh, w4, precision=hp, preferred_element_type=F32) * decay
    row = lax.broadcasted_iota(jnp.int32, k.shape, 0)
    kb0 = (jnp.dot(hh[0:SUBLANES], w4b_ref[...], precision=hp, preferred_element_type=F32) * decay[0:SUBLANES])[0:1]
    k = jnp.where(jnp.logical_and(row == 0, i == 0), k + kb0, k)
    k = jnp.where(jnp.logical_and(row == 0, i == nb), 0.0, k)
    k_ref[...] = k
    part = jnp.sum(jnp.abs(k), axis=0, keepdims=True)

    @pl.when(i == 0)
    def _():
        s_ref[...] = part

    @pl.when(i > 0)
    def _():
        s_ref[...] += part


def _hyena_filter(L, D, w1, b1, w2, b2, w3, b3, w4, freq, tl):
    t = jnp.linspace(0.0, 1.0, L, dtype=F32)[:, None]
    w = 2.0 * math.pi * jnp.arange(L, dtype=F32)[:, None] / L
    bands = jnp.linspace(1e-4, HY_BANDS - 1, HY_BANDS, dtype=F32)[None, :]
    feats = jnp.concatenate([t, jnp.cos(w * bands), -jnp.sin(w * bands)], axis=-1)
    deltas = jnp.linspace(math.log(HY_TARGET) / HY_SLOW_DECAY, math.log(HY_TARGET) / HY_FAST_DECAY, D, dtype=F32)
    rev = lambda a: jnp.roll(jnp.flip(a, 0), 1, 0)
    feats2 = jnp.concatenate([feats, rev(feats)], axis=0)
    t2 = jnp.concatenate([t, rev(t)], axis=0)
    ff = w1.shape[1]
    nb = L // tl
    full = lambda shape: pl.BlockSpec(shape, lambda i: (0,) * len(shape))
    return pl.pallas_call(
        functools.partial(_filt_kernel, nb=nb),
        grid=(2 * nb,),
        in_specs=[pl.BlockSpec((tl, HY_EMB), lambda i: (i, 0)), pl.BlockSpec((tl, 1), lambda i: (i, 0)),
                  full((HY_EMB, ff)), full((1, ff)), full((ff, ff)), full((1, ff)), full((ff, ff)), full((1, ff)),
                  full((ff, D)), full((ff, D)), full((1, ff)), full((1, D))],
        out_specs=[pl.BlockSpec((tl, D), lambda i: (i, 0)), pl.BlockSpec((1, D), lambda i: (0, 0))],
        out_shape=[jax.ShapeDtypeStruct((2 * L, D), F32), jax.ShapeDtypeStruct((1, D), F32)],
        compiler_params=_cparams(("arbitrary",)),
    )(feats2, t2, w1, b1.reshape(1, ff), w2, b2.reshape(1, ff), w3, b3.reshape(1, ff), w4[:, :D], w4[:, D:],
      freq.reshape(1, ff), deltas.reshape(1, D))


def _angles(rows, cols, n):
    r = (rows.astype(np.int64) * cols.astype(np.int64)) % n
    th = r.astype(np.float64) * (2.0 * math.pi / n)
    return jnp.asarray(np.cos(th), F32), jnp.asarray(np.sin(th), F32)


def _dft_mats(N1, N2):
    N = N1 * N2
    h = N1 // 2
    ar = lambda n: np.arange(n)
    def add(c1, s1, c2, s2):
        return c1 * c2 - s1 * s2, s1 * c2 + c1 * s2

    cp, sp = _angles(ar(N1)[:, None], ar(N1)[None, :], N1)
    cq, sq = _angles(ar(N2)[:, None], ar(N1)[None, :], N)
    cf, sf = add(cp[None], sp[None], cq[:, :, None], sq[:, :, None])
    c, s = cf[:, :, :h], sf[:, :, :h]
    a_data = jnp.concatenate([jnp.concatenate([c, s], -1), jnp.concatenate([-s, c], -1)], 1)
    a_filt = jnp.concatenate([cf, -sf], 1)
    c, s = _angles(ar(N2)[:, None], ar(N2)[None, :], N2)
    m2 = jnp.concatenate([jnp.concatenate([c, s], -1), jnp.concatenate([-s, c], -1)], 0)
    cr, sr = _angles(ar(N1)[:, None], ar(N2)[None, :], N)
    cs, ss = _angles(ar(N2)[:, None], ar(N2)[None, :], N2)
    c, s = add(cr[:, :, None], sr[:, :, None], cs[None], ss[None])
    hm =jnp.concatenate([jnp.concatenate([c, -s], -1), jnp.concatenate([s, c], -1)], 1)
    c, s = _angles(ar(h)[:, None], ar(N1)[None, :], N1)
    i2 = jnp.concatenate([jnp.concatenate([c, -s], -1), jnp.concatenate([s, c], -1)], 0) / N
    return (a_data.astype(BF16), a_filt.astype(BF16), m2.astype(BF16), hm.astype(BF16), i2.astype(BF16))


U32 = jnp.uint32


def _pack_c(re, im):
    r = lax.bitcast_convert_type(re.astype(BF16).astype(F32), U32)
    i = lax.bitcast_convert_type(im.astype(BF16).astype(F32), U32)
    return r | (i >> 16)


def _unpack_c(p):
    re = lax.bitcast_convert_type(p & jnp.uint32(0xFFFF0000), F32)
    im = lax.bitcast_convert_type(p << 16, F32)
    return jnp.concatenate([re, im], axis=0).astype(BF16)


def _fft1_kernel(*refs, scaled):
    a_ref = refs[0]
    inv_s = refs[1][...] if scaled else None
    x_ref, o_ref = refs[-2], refs[-1]
    nin = x_ref.shape[0]
    n1 = o_ref.shape[0]
    for j in range(SUBLANES):
        rows = [x_ref[q, :, j, :] for q in range(nin)]
        x = rows[0] if nin == 1 else jnp.concatenate(rows, axis=0)
        if scaled:
            x = x * inv_s
        y = jnp.dot(a_ref[j], x.astype(BF16), preferred_element_type=F32)
        o_ref[:, j, :] = _pack_c(y[:n1], y[n1:])


def _fft1_data(z5, a_data, dt):
    _, P, h, N2, D = z5.shape
    N1 = 2 * h
    return pl.pallas_call(
        functools.partial(_fft1_kernel, scaled=False),
        grid=(N2 // SUBLANES, P, D // dt),
        in_specs=[pl.BlockSpec((SUBLANES, 2 * N1, N1), lambda j, p, c: (j, 0, 0)),
                  pl.BlockSpec((2, None, h, SUBLANES, dt), lambda j, p, c: (0, p, 0, j, c))],
        out_specs=pl.BlockSpec((None, N1, SUBLANES, dt), lambda j, p, c: (p, 0, j, c)),
        out_shape=jax.ShapeDtypeStruct((P, N1, N2, D), U32),
        compiler_params=_cparams(("parallel", "parallel", "parallel")),
    )(a_data, z5)


def _fft1_filt(k4, inv_s, a_filt, dt):
    _, N1, N2, D = k4.shape
    return pl.pallas_call(
        functools.partial(_fft1_kernel, scaled=True),
        grid=(N2 // SUBLANES, D // dt),
        in_specs=[pl.BlockSpec((SUBLANES, 2 * N1, N1), lambda j, c: (j, 0, 0)),
                  pl.BlockSpec((1, dt), lambda j, c: (0, c)),
                  pl.BlockSpec((1, N1, SUBLANES, dt), lambda j, c: (0, 0, j, c))],
        out_specs=pl.BlockSpec((N1, SUBLANES, dt), lambda j, c: (0, j, c)),
        out_shape=jax.ShapeDtypeStruct((N1, N2, D), U32),
        compiler_params=_cparams(("parallel", "parallel")),
    )(a_filt, inv_s, k4)


def _fft_mid_kernel(m2_ref, h_ref, y_ref, yk_ref, o_ref, kf_ref):
    n2 = y_ref.shape[0]

    @pl.when(pl.program_id(1) == 0)
    def _():
        kf_ref[...] = jnp.dot(m2_ref[...], _unpack_c(yk_ref[...]), preferred_element_type=F32)

    x = jnp.dot(m2_ref[...], _unpack_c(y_ref[...]), preferred_element_type=F32)
    xr, xi = x[:n2], x[n2:]
    kr, ki = kf_ref[0:n2, :], kf_ref[n2:, :]
    pr = xr * kr - xi * ki
    pi = xr * ki + xi * kr
    p = jnp.concatenate([pr, pi], axis=0).astype(BF16)
    u = jnp.dot(h_ref[...], p, preferred_element_type=F32)
    o_ref[...] = _pack_c(u[:n2], u[n2:])


def _fft_mid(y, yk, m2, hm):
    P, N1, N2, D = y.shape
    return pl.pallas_call(
        _fft_mid_kernel,
        grid=(N1, P),
        in_specs=[pl.BlockSpec((2 * N2, 2 * N2), lambda k, p: (0, 0)),
                  pl.BlockSpec((None, 2 * N2, 2 * N2), lambda k, p: (k, 0, 0)),
                  pl.BlockSpec((None, None, N2, D), lambda k, p: (p, k, 0, 0)),
                  pl.BlockSpec((None, N2, D), lambda k, p: (k, 0, 0))],
        out_specs=pl.BlockSpec((None, None, N2, D), lambda k, p: (p, k, 0, 0)),
        out_shape=jax.ShapeDtypeStruct((P, N1, N2, D), U32),
        scratch_shapes=[pltpu.VMEM((2 * N2, D), F32)],
        compiler_params=_cparams(("parallel", "arbitrary")),
    )(m2, hm, y, yk)


def _fft_inv_kernel(i2_ref, u_ref, z_ref, x1_ref, bias_ref, o_ref):
    h = o_ref.shape[1]
    bias = bias_ref[...]
    for j in range(SUBLANES):
        y = jnp.dot(i2_ref[...], _unpack_c(u_ref[:, j, :]), preferred_element_type=F32)
        o_ref[0, :, j, :] = x1_ref[0, :, j, :] * (y[:h] + bias * z_ref[0, :, j, :])
        o_ref[1, :, j, :] = x1_ref[1, :, j, :] * (y[h:] + bias * z_ref[1, :, j, :])


def _fft_inv(u, z5, x15, bias, i2, dt):
    P, N1, N2, D = u.shape
    h = N1 // 2
    pair = pl.BlockSpec((2, None, h, SUBLANES, dt), lambda j, p, c: (0, p, 0, j, c))
    return pl.pallas_call(
        _fft_inv_kernel,
        grid=(N2 // SUBLANES, P, D // dt),
        in_specs=[pl.BlockSpec((N1, 2 * N1), lambda j, p, c: (0, 0)),
                  pl.BlockSpec((None, N1, SUBLANES, dt), lambda j, p, c: (p, 0, j, c)),
                  pair, pair,
                  pl.BlockSpec((1, dt), lambda j, p, c: (0, c))],
        out_specs=pair,
        out_shape=jax.ShapeDtypeStruct((2, P, h, N2, D), F32),
        compiler_params=_cparams(("parallel", "parallel", "parallel")),
    )(i2, u, z5, x15, bias)


def _hyena(p3, D, col0, conv_w, conv_b, w1, b1, w2, b2, w3, b3, w4, freq, bias, tl, ct, dt):
    B, L, _ = p3.shape
    N2 = FFT_N2
    N1 = 2 * L // N2
    h = N1 // 2
    z, x1c = _hy_pre(p3, conv_w, conv_b, D, col0, tl, ct)
    k_circ, s = _hyena_filter(L, D, w1, b1, w2, b2, w3, b3, w4, freq, min(tl, 512))
    a_data, a_filt, m2, hm, i2 = _dft_mats(N1, N2)
    yk = _fft1_filt(k_circ.reshape(1, N1, N2, D), 1.0 / s, a_filt, dt)
    z5 = z.reshape(2, B // 2, h, N2, D)
    y = _fft1_data(z5, a_data, dt)
    u = _fft_mid(y, yk, m2, hm)
    out = _fft_inv(u, z5, x1c.reshape(2, B // 2, h, N2, D), bias.reshape(1, D), i2, dt)
    return out.reshape(B * L, D)


def _merge_kernel(x_ref, yl_ref, yh_ref, gl_ref, gh_ref, wl_ref, wh_ref, wo_ref, gf_ref, wr_ref, br_ref,
                  h_ref, hn_ref, idx_ref, gate_ref):
    a = jnp.dot(yl_ref[...].astype(BF16), wl_ref[...], preferred_element_type=F32)
    b = jnp.dot(yh_ref[...].astype(BF16), wh_ref[...], preferred_element_type=F32)
    mix = gl_ref[...] * a + gh_ref[...] * b
    h = x_ref[...] + jnp.dot(mix.astype(BF16), wo_ref[...], preferred_element_type=F32)
    h_ref[...] = h
    ms = jnp.mean(h * h, axis=-1, keepdims=True)
    hn = h * lax.rsqrt(ms + EPS) * gf_ref[...]
    nc = hn.shape[1] // LANES
    for c in range(nc):
        hn_ref[pl.ds(c, hn.shape[0], stride=nc), :] = hn[:, c * LANES:(c + 1) * LANES]
    ne = br_ref.shape[1]
    h_hi = hn.astype(BF16)
    h_lo = (hn - h_hi.astype(F32)).astype(BF16)
    a2 = jnp.dot(h_hi, wr_ref[...], preferred_element_type=F32)
    b2 = jnp.dot(h_lo, wr_ref[:, :LANES], preferred_element_type=F32)
    logits = (a2[:, :LANES] + a2[:, LANES:] + b2)[:, :ne] + br_ref[...]
    lane = lax.broadcasted_iota(jnp.int32, logits.shape, 1)
    vals = []
    for k in range(TOP_K):
        m = jnp.max(logits, axis=-1, keepdims=True)
        ix = jnp.min(jnp.where(logits == m, lane, ne), axis=-1, keepdims=True)
        vals.append(m)
        idx_ref[:, k:k + 1] = ix
        logits = jnp.where(lane == ix, -jnp.inf, logits)
    ex = [jnp.exp(v - vals[0]) for v in vals]
    tot = ex[0] + ex[1] + ex[2] + ex[3]
    for k in range(TOP_K):
        gate_ref[:, k:k + 1] = ex[k] / tot


def _merge(x2, y_lru, y_hy, p_all, w_lru_br, w_hy_br, w_out, g_ffn, w_router, b_router, tm):
    T, D = x2.shape
    ne = w_router.shape[1]
    gcol = p_all.shape[1] // D - 2
    row = lambda i: (i, 0)
    full = lambda i: (0, 0)
    w_hi = w_router.astype(BF16)
    w_lo = (w_router - w_hi.astype(F32)).astype(BF16)
    lane_pad = lambda w: jnp.pad(w, ((0, 0), (0, LANES - ne)))
    wr2 = jnp.concatenate([lane_pad(w_hi), lane_pad(w_lo)], axis=1)
    return pl.pallas_call(
        _merge_kernel,
        grid=(T // tm,),
        in_specs=[pl.BlockSpec((tm, D), row), pl.BlockSpec((tm, D), row), pl.BlockSpec((tm, D), row),
                  pl.BlockSpec((tm, D), lambda i: (i, gcol)), pl.BlockSpec((tm, D), lambda i: (i, gcol + 1)),
                  pl.BlockSpec((D, D), full), pl.BlockSpec((D, D), full), pl.BlockSpec((D, D), full),
                  pl.BlockSpec((1, D), full), pl.BlockSpec((D, 2 * LANES), full), pl.BlockSpec((1, ne), full)],
        out_specs=[pl.BlockSpec((tm, D), row), pl.BlockSpec((tm * (D // LANES), LANES), row),
                   pl.BlockSpec((tm, TOP_K), row), pl.BlockSpec((tm, TOP_K), row)],
        out_shape=[jax.ShapeDtypeStruct((T, D), F32), jax.ShapeDtypeStruct((T * (D // LANES), LANES), F32),
                   jax.ShapeDtypeStruct((T, TOP_K), jnp.int32), jax.ShapeDtypeStruct((T, TOP_K), F32)],
        compiler_params=_cparams(("parallel",)),
    )(x2, y_lru, y_hy, p_all, p_all, w_lru_br.astype(BF16), w_hy_br.astype(BF16), w_out.astype(BF16),
      g_ffn.reshape(1, D), wr2, b_router.reshape(1, ne))


ROW_DMA_UNROLL = 8


def _for_rows(n, fn):
    def body(g, c):
        for k in range(ROW_DMA_UNROLL):
            fn(g * ROW_DMA_UNROLL + k, k % 2)
        return c

    lax.fori_loop(0, n // ROW_DMA_UNROLL, body, 0)


def _moe_kernel(ib_ref, ie_ref, bnd_ref, ni_ref, ord_ref, nord_ref, hn_hbm, gate_ref, wgu_ref, bgu_ref,
                wd_ref, bd_ref, yk_hbm, xbuf, ybuf, xs_ref, wgu_bf, wd_bf, gsem, ssem, *, nblk):
    i = pl.program_id(0)
    ni = ni_ref[0]
    nrow = xs_ref.shape[0]
    nc = xbuf.shape[1] // nrow
    dff = wd_ref.shape[0]

    def gather_copy(idx_ref, r, slot):
        tok = lax.shift_right_logical(idx_ref[0, 0, r], 2)
        return pltpu.make_async_copy(hn_hbm.at[tok], xbuf.at[slot, pl.ds(r * nc, nc), :], gsem.at[slot])

    def scatter_copy(r, slot):
        return pltpu.make_async_copy(ybuf.at[slot, pl.ds(r * nc, nc), :], yk_hbm.at[ord_ref[0, 0, r]],
                                     ssem.at[slot])

    @pl.when(i < ni)
    def _():
        b = ib_ref[i]
        e = ie_ref[i]
        slot = b % 2
        first = jnp.logical_or(i == 0, ib_ref[jnp.maximum(i - 1, 0)] != b)
        last = jnp.logical_or(i == ni - 1, ib_ref[jnp.minimum(i + 1, pl.num_programs(0) - 1)] != b)

        @pl.when(i == 0)
        def _():
            _for_rows(nrow, lambda r, p: gather_copy(ord_ref, r, 0).start(priority=p))

        @pl.when(first)
        def _():
            @pl.when(b + 1 < nblk)
            def _():
                _for_rows(nrow, lambda r, p: gather_copy(nord_ref, r, 1 - slot).start(priority=p))

            _for_rows(nrow, lambda r, p: gather_copy(ord_ref, r, slot).wait())

            @pl.when(b >= 2)
            def _():
                _for_rows(nrow, lambda r, p: scatter_copy(r, slot).wait())

        @pl.when(jnp.logical_or(i == 0, ie_ref[jnp.maximum(i - 1, 0)] != e))
        def _():
            wgu_bf[...] = wgu_ref[...].astype(BF16)
            wd_bf[...] = wd_ref[...].astype(BF16)

        for c in range(nc):
            xs_ref[:, c * LANES:(c + 1) * LANES] = xbuf[slot, pl.ds(c, nrow, stride=nc), :].astype(BF16)
        gu = jnp.dot(xs_ref[...], wgu_bf[...], preferred_element_type=F32) + bgu_ref[...]
        g = jnp.minimum(gu[:, :dff], SWIGLU_LIMIT)
        u = jnp.clip(gu[:, dff:], -SWIGLU_LIMIT, SWIGLU_LIMIT)
        act = (u + 1.0) * (g * jax.nn.sigmoid(SWIGLU_ALPHA * g))
        y = (jnp.dot(act.astype(BF16), wd_bf[...], preferred_element_type=F32) + bd_ref[...]) * gate_ref[...]
        row = lax.broadcasted_iota(jnp.int32, (nrow, 1), 0) + b * nrow
        mine = jnp.logical_and(row >= bnd_ref[e], row < bnd_ref[e + 1])

        @pl.when(first)
        def _():
            for c in range(nc):
                ybuf[slot, pl.ds(c, nrow, stride=nc), :] = jnp.where(mine, y[:, c * LANES:(c + 1) * LANES], 0.0)

        @pl.when(jnp.logical_not(first))
        def _():
            for c in range(nc):
                rows = pl.ds(c, nrow, stride=nc)
                ybuf[slot, rows, :] = jnp.where(mine, y[:, c * LANES:(c + 1) * LANES], ybuf[slot, rows, :])

        @pl.when(last)
        def _():
            _for_rows(nrow, lambda r, p: scatter_copy(r, slot).start(priority=p))

        @pl.when(i == ni - 1)
        def _():
            _for_rows(nrow, lambda r, p: scatter_copy(r, slot).wait())

            @pl.when(b >= 1)
            def _():
                _for_rows(nrow, lambda r, p: scatter_copy(r, 1 - slot).wait())


def _moe_items(hn, order, g_sorted, item_blk, item_exp, bounds, n_items, w_gate_up, b_gate_up, w_down, b_down):
    T, nc, _ = hn.shape
    D = nc * LANES
    N = order.shape[0]
    nblk = N // MOE_BLOCK
    n_steps = item_blk.shape[0]
    dff = w_down.shape[1]
    order3 = order.reshape(nblk, 1, MOE_BLOCK)
    grid_spec = pltpu.PrefetchScalarGridSpec(
        num_scalar_prefetch=4,
        grid=(n_steps,),
        in_specs=[
            pl.BlockSpec((1, 1, MOE_BLOCK), lambda i, ib, ie, bd, ni: (ib[i], 0, 0), memory_space=pltpu.SMEM),
            pl.BlockSpec((1, 1, MOE_BLOCK), lambda i, ib, ie, bd, ni: (jnp.minimum(ib[i] + 1, nblk - 1), 0, 0),
                         memory_space=pltpu.SMEM),
            pl.BlockSpec(memory_space=pl.ANY),
            pl.BlockSpec((MOE_BLOCK, 1), lambda i, ib, ie, bd, ni: (ib[i], 0)),
            pl.BlockSpec((None, D, 2 * dff), lambda i, ib, ie, bd, ni: (ie[i], 0, 0)),
            pl.BlockSpec((None, 1, 2 * dff), lambda i, ib, ie, bd, ni: (ie[i], 0, 0)),
            pl.BlockSpec((None, dff, D), lambda i, ib, ie, bd, ni: (ie[i], 0, 0)),
            pl.BlockSpec((None, 1, D), lambda i, ib, ie, bd, ni: (ie[i], 0, 0)),
        ],
        out_specs=pl.BlockSpec(memory_space=pl.ANY),
        scratch_shapes=[pltpu.VMEM((2, MOE_BLOCK * nc, LANES), F32), pltpu.VMEM((2, MOE_BLOCK * nc, LANES), F32),
                        pltpu.VMEM((MOE_BLOCK, D), BF16),
                        pltpu.VMEM((D, 2 * dff), BF16), pltpu.VMEM((dff, D), BF16),
                        pltpu.SemaphoreType.DMA((2,)), pltpu.SemaphoreType.DMA((2,))],
    )
    return pl.pallas_call(
        functools.partial(_moe_kernel, nblk=nblk),
        grid_spec=grid_spec,
        out_shape=jax.ShapeDtypeStruct((N, nc, LANES), F32),
        compiler_params=_cparams(("arbitrary",)),
    )(item_blk, item_exp, bounds, n_items, order3, order3, hn, g_sorted.reshape(N, 1),
      w_gate_up, b_gate_up.reshape(N_EXPERTS, 1, 2 * dff), w_down, b_down.reshape(N_EXPERTS, 1, D))


def _combine_kernel(yk_ref, h_ref, g_ref, o_ref, ys_ref, hs_ref):
    tq = h_ref.shape[0]
    nc = yk_ref.shape[1] // TOP_K
    y = yk_ref[:, 0:nc, :]
    for k in range(1, TOP_K):
        y = y + yk_ref[:, k * nc:(k + 1) * nc, :]
    ys_ref[...] = y.reshape(tq * nc, LANES)
    for c in range(nc):
        hs_ref[:, c * LANES:(c + 1) * LANES] = (h_ref[:, c * LANES:(c + 1) * LANES]
                                                + ys_ref[pl.ds(c, tq, stride=nc), :])
    h = hs_ref[...]
    ms = jnp.mean(h * h, axis=-1, keepdims=True)
    o_ref[...] = h * lax.rsqrt(ms + EPS) * g_ref[...]


def _combine(yk, h, g_final, tq):
    T, D = h.shape
    nc = yk.shape[1]
    return pl.pallas_call(
        _combine_kernel,
        grid=(T // tq,),
        in_specs=[pl.BlockSpec((tq, TOP_K * nc, LANES), lambda i: (i, 0, 0)),
                  pl.BlockSpec((tq, D), lambda i: (i, 0)),
                  pl.BlockSpec((1, D), lambda i: (0, 0))],
        out_specs=pl.BlockSpec((tq, D), lambda i: (i, 0)),
        out_shape=jax.ShapeDtypeStruct((T, D), F32),
        scratch_shapes=[pltpu.VMEM((tq * nc, LANES), F32), pltpu.VMEM((tq, D), F32)],
        compiler_params=_cparams(("parallel",)),
    )(yk.reshape(T, TOP_K * nc, LANES), h, g_final.reshape(1, D))


def _route(top_idx, gates):
    T = top_idx.shape[0]
    N = T * TOP_K
    nblk = N // MOE_BLOCK
    i32 = jnp.int32
    e_sorted, order, g_sorted = lax.sort(
        (top_idx.reshape(N), jnp.arange(N, dtype=i32), gates.reshape(N)), num_keys=1)
    bounds = jnp.sum(e_sorted[None, :] < jnp.arange(N_EXPERTS + 1, dtype=i32)[:, None], axis=1, dtype=i32)
    starts, ends = bounds[:-1], bounds[1:]
    first_blk = starts // MOE_BLOCK
    n_of = jnp.where(ends > starts, (ends - 1) // MOE_BLOCK - first_blk + 1, 0)
    item_end = jnp.cumsum(n_of)
    item_start = item_end - n_of
    n_items = item_end[-1:]
    n_steps = nblk + N_EXPERTS - 1
    i = jnp.arange(n_steps, dtype=i32)
    ie = jnp.minimum(jnp.sum(item_end[None, :] <= i[:, None], axis=1, dtype=i32), N_EXPERTS - 1)
    ib = jnp.clip(first_blk[ie] + i - item_start[ie], 0, nblk - 1).astype(i32)
    return order, g_sorted, ib, ie, bounds, n_items.astype(i32)


def kernel(x, g_mix, w_in, b_in, lru_conv_w, lru_conv_b, lru_wa, lru_ba, lru_wi, lru_bi, lru_lambda, w_lru_br, hy_conv_w, hy_conv_b, hy_w1, hy_b1, hy_w2, hy_b2, hy_w3, hy_b3, hy_w4, hy_sin_freq, hy_bias, w_hy_br, w_out, g_ffn, w_router, b_router, w_gate_up, b_gate_up, w_down, b_down, g_final):
    B, L, D = x.shape
    T = B * L
    x2 = x.reshape(T, D)
    tm = min(1024, T)
    p_all = _inproj(x2, g_mix, w_in, b_in, tm)
    p3 = p_all.reshape(B, L, p_all.shape[1])
    y_lru = _lru(p3, lru_conv_w, lru_conv_b, lru_wa, lru_ba, lru_wi, lru_bi, lru_lambda, min(512, L))
    y_hy = _hyena(p3, D, 2 * D, hy_conv_w, hy_conv_b, hy_w1, hy_b1, hy_w2, hy_b2, hy_w3, hy_b3, hy_w4,
                  hy_sin_freq, hy_bias, min(1024, L), min(256, D), min(512, D))
    h, hn, top_idx, gates = _merge(x2, y_lru.reshape(T, D), y_hy, p_all, w_lru_br, w_hy_br, w_out,
                                   g_ffn, w_router, b_router, min(512, T))
    order, g_sorted, item_blk, item_exp, bounds, n_items = _route(top_idx, gates)
    yk = _moe_items(hn.reshape(T, D // LANES, LANES), order, g_sorted, item_blk, item_exp, bounds, n_items,
                    w_gate_up, b_gate_up, w_down, b_down)
    out = _combine(yk, h, g_final, min(256, T))
    return out.reshape(B, L, D)
```

```python
import functools
import math

import numpy as np
import jax
import jax.numpy as jnp
from jax import lax
from jax.experimental import pallas as pl
from jax.experimental.pallas import tpu as pltpu

F32 = jnp.float32
BF16 = jnp.bfloat16

EPS = 1e-5
C_RGLRU = 8.0
RNN_BLOCK = 64
HY_EMB = 33
HY_BANDS = 16
HY_FAST_DECAY = 0.3
HY_SLOW_DECAY = 1.5
HY_TARGET = 1e-2
N_EXPERTS = 32
TOP_K = 4
SWIGLU_LIMIT = 7.0
SWIGLU_ALPHA = 1.702
MOE_BLOCK = 512

LANES = 128
SUBLANES = 8
FFT_N2 = 128
VMEM_LIMIT = 56 * 1024 * 1024


def _cparams(sem):
    return pltpu.CompilerParams(dimension_semantics=sem, vmem_limit_bytes=VMEM_LIMIT)


def _inproj_kernel(x_ref, g_ref, w_ref, b_ref, o_ref, u_ref):
    j = pl.program_id(1)

    @pl.when(j == 0)
    def _():
        x = x_ref[...]
        ms = jnp.mean(x * x, axis=-1, keepdims=True)
        u_ref[...] = (x * lax.rsqrt(ms + EPS) * g_ref[...]).astype(BF16)

    def proj():
        return jnp.dot(u_ref[...], w_ref[...], preferred_element_type=F32) + b_ref[...]

    @pl.when(j == 1)
    def _():
        o_ref[...] = jax.nn.gelu(proj())

    @pl.when(j >= 5)
    def _():
        o_ref[...] = jax.nn.sigmoid(proj())

    @pl.when(jnp.logical_and(j != 1, j < 5))
    def _():
        o_ref[...] = proj()


def _inproj(x2, g_mix, w_in, b_in, tm):
    T, D = x2.shape
    n_in = w_in.shape[1]
    return pl.pallas_call(
        _inproj_kernel,
        grid=(T // tm, n_in // D),
        in_specs=[
            pl.BlockSpec((tm, D), lambda i, j: (i, 0)),
            pl.BlockSpec((1, D), lambda i, j: (0, 0)),
            pl.BlockSpec((D, D), lambda i, j: (0, j)),
            pl.BlockSpec((1, D), lambda i, j: (0, j)),
        ],
        out_specs=pl.BlockSpec((tm, D), lambda i, j: (i, j)),
        out_shape=jax.ShapeDtypeStruct((T, n_in), F32),
        scratch_shapes=[pltpu.VMEM((tm, D), BF16)],
        compiler_params=_cparams(("parallel", "arbitrary")),
    )(x2, g_mix.reshape(1, D), w_in.astype(BF16), b_in.reshape(1, n_in))


def _lru_kernel(x_ref, g_ref, cw_ref, cb_ref, wg_ref, bg_ref, lam_ref, o_ref,
                xp_ref, xr_ref, hb_ref, af_ref, bf_ref, ab_ref, bb_ref, *, L, R):
    nchunk = L // R
    ng = R // SUBLANES
    zeros8 = jnp.zeros((SUBLANES, LANES), F32)
    xp_ref[0:SUBLANES, :] = zeros8
    xp_ref[SUBLANES + L:2 * SUBLANES + L, :] = zeros8
    xp_ref[SUBLANES:SUBLANES + L, :] = x_ref[...]

    sub = lax.broadcasted_iota(jnp.int32, (ng, SUBLANES, LANES), 1)
    c8 = [-C_RGLRU * jax.nn.softplus(-lam_ref[d]) for d in range(2)]
    cb = cb_ref[...]
    cw = [cw_ref[k:k + 1, :] for k in range(4)]

    def conv(c, _):
        t0 = pl.multiple_of(c * R, R)
        xr = cb
        for k in range(4):
            xr = xr + cw[k] * xp_ref[pl.ds(t0 + (SUBLANES - 2 + k), R), :]
        xr_ref[pl.ds(t0, R), :] = xr
        return 0

    lax.fori_loop(0, nchunk, conv, 0)

    def prep(t0, d):
        xr = xr_ref[pl.ds(t0, R), :]
        gz = jnp.dot(xr.astype(BF16), wg_ref[d], preferred_element_type=F32) + bg_ref[d]
        r = 0.5 * jnp.tanh(0.5 * gz[:, :LANES]) + 0.5
        i = 0.5 * jnp.tanh(0.5 * gz[:, LANES:]) + 0.5
        log_a = c8[d] * r
        a = jnp.exp(log_a)
        b = jnp.sqrt((1.0 + a * a) * jnp.tanh(-log_a)) * (i * xr)
        return a, b

    def chunk(c, carry):
        t0f = pl.multiple_of(c * R, R)
        t0b = pl.multiple_of((nchunk - 1 - c) * R, R)
        a, b = (v.reshape(ng, SUBLANES, LANES) for v in prep(t0f, 0))
        for s in (1, 2, 4):
            a_sh = pltpu.roll(a, s, 1)
            b_sh = pltpu.roll(b, s, 1)
            m = sub >= s
            b = jnp.where(m, a * b_sh + b, b)
            a = jnp.where(m, a * a_sh, a)
        af_ref[...] = a.reshape(R, LANES)
        bf_ref[...] = b.reshape(R, LANES)
        a, b = (v.reshape(ng, SUBLANES, LANES) for v in prep(t0b, 1))
        for s in (1, 2, 4):
            a_sh = pltpu.roll(a, SUBLANES - s, 1)
            b_sh = pltpu.roll(b, SUBLANES - s, 1)
            m = sub < SUBLANES - s
            b = jnp.where(m, a * b_sh + b, b)
            a = jnp.where(m, a * a_sh, a)
        ab_ref[...] = a.reshape(R, LANES)
        bb_ref[...] = b.reshape(R, LANES)

        def group(j, hc):
            hf, hk = hc
            rf = pl.multiple_of(j * SUBLANES, SUBLANES)
            h8 = bf_ref[pl.ds(rf, SUBLANES), :] + af_ref[pl.ds(rf, SUBLANES), :] * hf
            o_ref[pl.ds(t0f + rf, SUBLANES), :] = h8
            hf = jnp.broadcast_to(h8[SUBLANES - 1:SUBLANES, :], (SUBLANES, LANES))
            rb = pl.multiple_of((ng - 1 - j) * SUBLANES, SUBLANES)
            k8 = bb_ref[pl.ds(rb, SUBLANES), :] + ab_ref[pl.ds(rb, SUBLANES), :] * hk
            hb_ref[pl.ds(t0b + rb, SUBLANES), :] = k8
            hk = jnp.broadcast_to(k8[0:1, :], (SUBLANES, LANES))
            return hf, hk

        return lax.fori_loop(0, ng, group, carry, unroll=8)

    lax.fori_loop(0, nchunk, chunk, (zeros8, zeros8))

    def fin(c, _):
        t0 = pl.multiple_of(c * R, R)
        o_ref[pl.ds(t0, R), :] = g_ref[pl.ds(t0, R), :] * (o_ref[pl.ds(t0, R), :] + hb_ref[pl.ds(t0, R), :])
        return 0

    lax.fori_loop(0, nchunk, fin, 0)


def _lru(p3, conv_w, conv_b, wa, ba, wi, bi, lam, R):
    B, L, _ = p3.shape
    D = conv_w.shape[1]
    nt = D // LANES
    hpt = LANES // RNN_BLOCK

    def blockdiag(w):
        w = w.reshape(2, nt, hpt, RNN_BLOCK, RNN_BLOCK)
        eye = jnp.eye(hpt, dtype=w.dtype)
        return jnp.einsum('dthij,hg->dthigj', w, eye).reshape(2, nt, LANES, LANES)

    wg = jnp.concatenate([blockdiag(wa), blockdiag(wi)], axis=-1).astype(BF16)
    bg = jnp.concatenate([ba.reshape(2, nt, 1, LANES), bi.reshape(2, nt, 1, LANES)], axis=-1)
    lam4 = lam.reshape(2, nt, 1, LANES)
    return pl.pallas_call(
        functools.partial(_lru_kernel, L=L, R=R),
        grid=(B, nt),
        in_specs=[
            pl.BlockSpec((None, L, LANES), lambda b, c: (b, 0, c)),
            pl.BlockSpec((None, L, LANES), lambda b, c: (b, 0, nt + c)),
            pl.BlockSpec((4, LANES), lambda b, c: (0, c)),
            pl.BlockSpec((1, LANES), lambda b, c: (0, c)),
            pl.BlockSpec((2, None, LANES, 2 * LANES), lambda b, c: (0, c, 0, 0)),
            pl.BlockSpec((2, None, 1, 2 * LANES), lambda b, c: (0, c, 0, 0)),
            pl.BlockSpec((2, None, 1, LANES), lambda b, c: (0, c, 0, 0)),
        ],
        out_specs=pl.BlockSpec((None, L, LANES), lambda b, c: (b, 0, c)),
        out_shape=jax.ShapeDtypeStruct((B, L, D), F32),
        scratch_shapes=[
            pltpu.VMEM((L + 2 * SUBLANES, LANES), F32),
            pltpu.VMEM((L, LANES), F32),
            pltpu.VMEM((L, LANES), F32),
            pltpu.VMEM((R, LANES), F32), pltpu.VMEM((R, LANES), F32),
            pltpu.VMEM((R, LANES), F32), pltpu.VMEM((R, LANES), F32),
        ],
        compiler_params=_cparams(("parallel", "parallel")),
    )(p3, p3, conv_w, conv_b.reshape(1, D), wg, bg, lam4)


def _hy_pre_kernel(v_ref, x1_ref, x2_ref, vp_ref, x1p_ref, x2p_ref, vn_ref, x1n_ref, x2n_ref,
                   w_ref, b_ref, z_ref, x1c_ref, *, nt):
    i = pl.program_id(1)
    tl = v_ref.shape[0]
    row = lax.broadcasted_iota(jnp.int32, v_ref.shape, 0)
    first = i == 0
    last = i == nt - 1

    def conv(c, m_ref, p_ref, n_ref):
        x = m_ref[...]
        prev = jnp.where(first, 0.0, p_ref[SUBLANES - 1:SUBLANES, :])
        nxt = jnp.where(last, 0.0, n_ref[0:1, :])
        xm = jnp.where(row == 0, prev, pltpu.roll(x, 1, 0))
        xn = jnp.where(row == tl - 1, nxt, pltpu.roll(x, tl - 1, 0))
        return b_ref[c] + w_ref[c, 0:1, :] * xm + w_ref[c, 1:2, :] * x + w_ref[c, 2:3, :] * xn

    v = conv(0, v_ref, vp_ref, vn_ref)
    x1c = conv(1, x1_ref, x1p_ref, x1n_ref)
    z = conv(2, x2_ref, x2p_ref, x2n_ref) * v
    for s in range(z_ref.shape[0]):
        x1c_ref[s] = x1c[:, s * LANES:(s + 1) * LANES]
        z_ref[s] = z[:, s * LANES:(s + 1) * LANES]


def _hy_pre(p3, conv_w, conv_b, D, col0, tl, ct):
    B, L, n_in = p3.shape
    nt = L // tl
    ncb = D // ct
    g8 = tl // SUBLANES
    p4 = p3.reshape(B, L // SUBLANES, SUBLANES, n_in)
    w3 = conv_w.reshape(3, 3, D).transpose(1, 0, 2)
    b3 = conv_b.reshape(3, 1, D)
    cb0 = col0 // ct

    def main(part):
        return pl.BlockSpec((None, tl, ct), lambda b, i, c: (b, i, cb0 + part * ncb + c))

    def prev(part):
        return pl.BlockSpec((None, None, SUBLANES, ct),
                            lambda b, i, c: (b, jnp.maximum(i * g8 - 1, 0), 0, cb0 + part * ncb + c))

    def nxt(part):
        return pl.BlockSpec((None, None, SUBLANES, ct),
                            lambda b, i, c: (b, jnp.minimum((i + 1) * g8, L // SUBLANES - 1), 0, cb0 + part * ncb + c))

    out = pl.BlockSpec((ct // LANES, None, tl, LANES), lambda b, i, c: (c, b, i, 0))
    return pl.pallas_call(
        functools.partial(_hy_pre_kernel, nt=nt),
        grid=(B, nt, ncb),
        in_specs=[main(0), main(1), main(2), prev(0), prev(1), prev(2), nxt(0), nxt(1), nxt(2),
                  pl.BlockSpec((3, 3, ct), lambda b, i, c: (0, 0, c)),
                  pl.BlockSpec((3, 1, ct), lambda b, i, c: (0, 0, c))],
        out_specs=[out, out],
        out_shape=[jax.ShapeDtypeStruct((D // LANES, B, L, LANES), F32)] * 2,
        compiler_params=_cparams(("parallel", "parallel", "parallel")),
    )(p3, p3, p3, p4, p4, p4, p4, p4, p4, w3, b3)


def _filt_kernel(feat_ref, t_ref, w1_ref, b1_ref, w2_ref, b2_ref, w3_ref, b3_ref, w4f_ref, w4b_ref, fr_ref, dl_ref,
                 k_ref, s_ref, *, nb):
    i = pl.program_id(0)
    hp = lax.Precision.HIGHEST
    fr = fr_ref[...]
    fwd = i < nb
    hh = jnp.sin(fr * (jnp.dot(feat_ref[...], w1_ref[...], precision=hp, preferred_element_type=F32) + b1_ref[...]))
    hh = jnp.sin(fr * (jnp.dot(hh, w2_ref[...], precision=hp, preferred_element_type=F32) + b2_ref[...]))
    hh = jnp.sin(fr * (jnp.dot(hh, w3_ref[...], precision=hp, preferred_element_type=F32) + b3_ref[...]))
    w4 = jnp.where(fwd, w4f_ref[...], w4b_ref[...])
    decay = jnp.exp(-t_ref[...] * jnp.abs(dl_ref[...]))
    k = jnp.dot(hh, w4, precision=hp, preferred_element_type=F32) * decay
    row = lax.broadcasted_iota(jnp.int32, k.shape, 0)
    kb0 = (jnp.dot(hh[0:SUBLANES], w4b_ref[...], precision=hp, preferred_element_type=F32) * decay[0:SUBLANES])[0:1]
    k = jnp.where(jnp.logical_and(row == 0, i == 0), k + kb0, k)
    k = jnp.where(jnp.logical_and(row == 0, i == nb), 0.0, k)
    for s in range(k_ref.shape[0]):
        k_ref[s] = k[:, s * LANES:(s + 1) * LANES]
    part = jnp.sum(jnp.abs(k), axis=0, keepdims=True)

    @pl.when(i == 0)
    def _():
        s_ref[...] = part

    @pl.when(i > 0)
    def _():
        s_ref[...] += part


def _hyena_filter(L, D, w1, b1, w2, b2, w3, b3, w4, freq, tl):
    t = jnp.linspace(0.0, 1.0, L, dtype=F32)[:, None]
    w = 2.0 * math.pi * jnp.arange(L, dtype=F32)[:, None] / L
    bands = jnp.linspace(1e-4, HY_BANDS - 1, HY_BANDS, dtype=F32)[None, :]
    feats = jnp.concatenate([t, jnp.cos(w * bands), -jnp.sin(w * bands)], axis=-1)
    deltas = jnp.linspace(math.log(HY_TARGET) / HY_SLOW_DECAY, math.log(HY_TARGET) / HY_FAST_DECAY, D, dtype=F32)
    rev = lambda a: jnp.roll(jnp.flip(a, 0), 1, 0)
    feats2 = jnp.concatenate([feats, rev(feats)], axis=0)
    t2 = jnp.concatenate([t, rev(t)], axis=0)
    ff = w1.shape[1]
    nb = L // tl
    full = lambda shape: pl.BlockSpec(shape, lambda i: (0,) * len(shape))
    return pl.pallas_call(
        functools.partial(_filt_kernel, nb=nb),
        grid=(2 * nb,),
        in_specs=[pl.BlockSpec((tl, HY_EMB), lambda i: (i, 0)), pl.BlockSpec((tl, 1), lambda i: (i, 0)),
                  full((HY_EMB, ff)), full((1, ff)), full((ff, ff)), full((1, ff)), full((ff, ff)), full((1, ff)),
                  full((ff, D)), full((ff, D)), full((1, ff)), full((1, D))],
        out_specs=[pl.BlockSpec((D // LANES, tl, LANES), lambda i: (0, i, 0)), pl.BlockSpec((1, D), lambda i: (0, 0))],
        out_shape=[jax.ShapeDtypeStruct((D // LANES, 2 * L, LANES), F32), jax.ShapeDtypeStruct((1, D), F32)],
        compiler_params=_cparams(("arbitrary",)),
    )(feats2, t2, w1, b1.reshape(1, ff), w2, b2.reshape(1, ff), w3, b3.reshape(1, ff), w4[:, :D], w4[:, D:],
      freq.reshape(1, ff), deltas.reshape(1, D))


def _angles(rows, cols, n):
    r = (rows.astype(np.int64) * cols.astype(np.int64)) % n
    th = r.astype(np.float64) * (2.0 * math.pi / n)
    return jnp.asarray(np.cos(th), F32), jnp.asarray(np.sin(th), F32)


def _dft_mats(N1, N2):
    N = N1 * N2
    h = N1 // 2
    ar = lambda n: np.arange(n)
    def add(c1, s1, c2, s2):
        return c1 * c2 - s1 * s2, s1 * c2 + c1 * s2

    cp, sp = _angles(ar(N1)[:, None], ar(N1)[None, :], N1)
    cq, sq = _angles(ar(N2)[:, None], ar(N1)[None, :], N)
    cf, sf = add(cp[None], sp[None], cq[:, :, None], sq[:, :, None])
    c, s = cf[:, :, :h], sf[:, :, :h]
    a_data = jnp.concatenate([jnp.concatenate([c, s], -1), jnp.concatenate([-s, c], -1)], 1)
    a_filt = jnp.concatenate([cf, -sf], 1)
    c, s = _angles(ar(N2)[:, None], ar(N2)[None, :], N2)
    m2 = jnp.concatenate([jnp.concatenate([c, s], -1), jnp.concatenate([-s, c], -1)], 0)
    cr, sr = _angles(ar(N1)[:, None], ar(N2)[None, :], N)
    cs, ss = _angles(ar(N2)[:, None], ar(N2)[None, :], N2)
    c, s = add(cr[:, :, None], sr[:, :, None], cs[None], ss[None])
    hm =jnp.concatenate([jnp.concatenate([c, -s], -1), jnp.concatenate([s, c], -1)], 1)
    c, s = _angles(ar(h)[:, None], ar(N1)[None, :], N1)
    i2 = jnp.concatenate([jnp.concatenate([c, -s], -1), jnp.concatenate([s, c], -1)], 0) / N
    return (a_data.astype(BF16), a_filt.astype(BF16), m2.astype(BF16), hm.astype(BF16), i2.astype(BF16))


BF16_ROWS = 16


def _lane_cat(parts):
    return parts[0] if len(parts) == 1 else jnp.concatenate(parts, axis=1)


def _flatten_in(v, dst_ref, region, nrows):
    for hf in range(BF16_ROWS // SUBLANES):
        dst_ref[pl.ds((2 * region + hf) * nrows, nrows), :] = (
            v[:, hf * SUBLANES:(hf + 1) * SUBLANES, :].reshape(nrows, LANES))


def _unflatten_out(src_ref, region, n, nrows):
    return jnp.concatenate(
        [src_ref[pl.ds((2 * region + hf) * nrows, nrows), :].reshape(n, SUBLANES, LANES)
         for hf in range(BF16_ROWS // SUBLANES)], axis=1)


def _tstep(region, j, n, nrows):
    hf, jj = divmod(j, SUBLANES)
    return pl.ds((2 * region + hf) * nrows + jj, n, stride=SUBLANES)


def _fft1_kernel(*refs, scaled):
    a_ref = refs[0]
    inv_s = refs[1][...] if scaled else None
    x_ref, o_ref, xs_ref, os_ref = refs[-4:]
    ns, nin, rows = x_ref.shape[0], x_ref.shape[1], x_ref.shape[2]
    n1 = o_ref.shape[2]
    xr, orr = rows * SUBLANES, n1 * SUBLANES
    for s in range(ns):
        for q in range(nin):
            _flatten_in(x_ref[s, q], xs_ref, s * nin + q, xr)
    for j in range(BF16_ROWS):
        parts = [_lane_cat([xs_ref[_tstep(s * nin + q, j, rows, xr), :] for s in range(ns)]) for q in range(nin)]
        x = parts[0] if nin == 1 else jnp.concatenate(parts, axis=0)
        if scaled:
            x = x * inv_s
        y = jnp.dot(a_ref[j], x.astype(BF16), preferred_element_type=F32)
        for s in range(ns):
            for c in range(2):
                os_ref[_tstep(s * 2 + c, j, n1, orr), :] = y[c * n1:(c + 1) * n1, s * LANES:(s + 1) * LANES]
    for s in range(ns):
        for c in range(2):
            o_ref[s, c] = _unflatten_out(os_ref, s * 2 + c, n1, orr).astype(BF16)


def _fft1_data(z6, a_data, dt):
    S, _, P, h, N2, _ = z6.shape
    N1 = 2 * h
    ns = dt // LANES
    jb = BF16_ROWS
    return pl.pallas_call(
        functools.partial(_fft1_kernel, scaled=False),
        grid=(N2 // jb, P, S // ns),
        in_specs=[pl.BlockSpec((jb, 2 * N1, N1), lambda j, p, c: (j, 0, 0)),
                  pl.BlockSpec((ns, 2, None, h, jb, LANES), lambda j, p, c: (c, 0, p, 0, j, 0))],
        out_specs=pl.BlockSpec((None, ns, 2, N1, jb, LANES), lambda j, p, c: (p, c, 0, 0, j, 0)),
        out_shape=jax.ShapeDtypeStruct((P, S, 2, N1, N2, LANES), BF16),
        scratch_shapes=[pltpu.VMEM((ns * 2 * h * jb, LANES), F32),
                        pltpu.VMEM((ns * 2 * N1 * jb, LANES), F32)],
        compiler_params=_cparams(("parallel", "parallel", "parallel")),
    )(a_data, z6)


def _fft1_filt(k5, inv_s, a_filt, dt):
    S, _, N1, N2, _ = k5.shape
    ns = dt // LANES
    jb = BF16_ROWS
    return pl.pallas_call(
        functools.partial(_fft1_kernel, scaled=True),
        grid=(N2 // jb, S // ns),
        in_specs=[pl.BlockSpec((jb, 2 * N1, N1), lambda j, c: (j, 0, 0)),
                  pl.BlockSpec((1, dt), lambda j, c: (0, c)),
                  pl.BlockSpec((ns, 1, N1, jb, LANES), lambda j, c: (c, 0, 0, j, 0))],
        out_specs=pl.BlockSpec((ns, 2, N1, jb, LANES), lambda j, c: (c, 0, 0, j, 0)),
        out_shape=jax.ShapeDtypeStruct((S, 2, N1, N2, LANES), BF16),
        scratch_shapes=[pltpu.VMEM((ns * N1 * jb, LANES), F32),
                        pltpu.VMEM((ns * 2 * N1 * jb, LANES), F32)],
        compiler_params=_cparams(("parallel", "parallel")),
    )(a_filt, inv_s, k5)


def _fft_mid_kernel(m2_ref, h_ref, y_ref, yk_ref, o_ref, kf_ref):
    S, n2 = y_ref.shape[0], y_ref.shape[2]

    def stacked(ref):
        return jnp.concatenate([_lane_cat([ref[s, c] for s in range(S)]) for c in range(2)], axis=0)

    @pl.when(pl.program_id(1) == 0)
    def _():
        kf_ref[...] = jnp.dot(m2_ref[...], stacked(yk_ref), preferred_element_type=F32)

    x = jnp.dot(m2_ref[...], stacked(y_ref), preferred_element_type=F32)
    xr, xi = x[:n2], x[n2:]
    kr, ki = kf_ref[0:n2, :], kf_ref[n2:, :]
    pr = xr * kr - xi * ki
    pi = xr * ki + xi * kr
    p = jnp.concatenate([pr, pi], axis=0).astype(BF16)
    u = jnp.dot(h_ref[...], p, preferred_element_type=F32).astype(BF16)
    for s in range(S):
        for c in range(2):
            o_ref[s, c] = u[c * n2:(c + 1) * n2, s * LANES:(s + 1) * LANES]


def _fft_mid(y, yk, m2, hm):
    P, S, _, N1, N2, _ = y.shape
    return pl.pallas_call(
        _fft_mid_kernel,
        grid=(N1, P),
        in_specs=[pl.BlockSpec((2 * N2, 2 * N2), lambda k, p: (0, 0)),
                  pl.BlockSpec((None, 2 * N2, 2 * N2), lambda k, p: (k, 0, 0)),
                  pl.BlockSpec((None, S, 2, None, N2, LANES), lambda k, p: (p, 0, 0, k, 0, 0)),
                  pl.BlockSpec((S, 2, None, N2, LANES), lambda k, p: (0, 0, k, 0, 0))],
        out_specs=pl.BlockSpec((None, S, 2, None, N2, LANES), lambda k, p: (p, 0, 0, k, 0, 0)),
        out_shape=jax.ShapeDtypeStruct((P, S, 2, N1, N2, LANES), BF16),
        scratch_shapes=[pltpu.VMEM((2 * N2, S * LANES), F32)],
        compiler_params=_cparams(("parallel", "arbitrary")),
    )(m2, hm, y, yk)


def _fft_inv_kernel(i2_ref, u_ref, z_ref, x1_ref, bias_ref, o_ref, us_ref, zs_ref, xs_ref, os_ref):
    ns, n1 = u_ref.shape[0], u_ref.shape[2]
    h = o_ref.shape[2]
    ur, hr = n1 * SUBLANES, h * SUBLANES
    bias = bias_ref[...]
    for s in range(ns):
        for c in range(2):
            _flatten_in(u_ref[s, c].astype(F32), us_ref, s * 2 + c, ur)
            _flatten_in(z_ref[s, c], zs_ref, s * 2 + c, hr)
            _flatten_in(x1_ref[s, c], xs_ref, s * 2 + c, hr)
    for j in range(BF16_ROWS):
        up = jnp.concatenate([_lane_cat([us_ref[_tstep(s * 2 + c, j, n1, ur), :] for s in range(ns)])
                              for c in range(2)], axis=0)
        y = jnp.dot(i2_ref[...], up.astype(BF16), preferred_element_type=F32)
        for s in range(ns):
            lanes = slice(s * LANES, (s + 1) * LANES)
            for q in range(2):
                t = _tstep(s * 2 + q, j, h, hr)
                os_ref[t, :] = xs_ref[t, :] * (y[q * h:(q + 1) * h, lanes] + bias[:, lanes] * zs_ref[t, :])
    for s in range(ns):
        for q in range(2):
            o_ref[s, q] = _unflatten_out(os_ref, s * 2 + q, h, hr)


def _fft_inv(u, z6, x16, bias, i2, dt):
    P, S, _, N1, N2, _ = u.shape
    h = N1 // 2
    ns = dt // LANES
    jb = BF16_ROWS
    pair = pl.BlockSpec((ns, 2, None, h, jb, LANES), lambda j, p, c: (c, 0, p, 0, j, 0))
    return pl.pallas_call(
        _fft_inv_kernel,
        grid=(N2 // jb, P, S // ns),
        in_specs=[pl.BlockSpec((N1, 2 * N1), lambda j, p, c: (0, 0)),
                  pl.BlockSpec((None, ns, 2, N1, jb, LANES), lambda j, p, c: (p, c, 0, 0, j, 0)),
                  pair, pair,
                  pl.BlockSpec((1, dt), lambda j, p, c: (0, c))],
        out_specs=pair,
        out_shape=jax.ShapeDtypeStruct((S, 2, P, h, N2, LANES), F32),
        scratch_shapes=[pltpu.VMEM((ns * 2 * N1 * jb, LANES), F32)]
        + [pltpu.VMEM((ns * 2 * h * jb, LANES), F32)] * 3,
        compiler_params=_cparams(("parallel", "parallel", "parallel")),
    )(i2, u, z6, x16, bias)


def _hyena(p3, D, col0, conv_w, conv_b, w1, b1, w2, b2, w3, b3, w4, freq, bias, tl, ct, dt):
    B, L, _ = p3.shape
    N2 = FFT_N2
    N1 = 2 * L // N2
    h = N1 // 2
    S = D // LANES
    z, x1c = _hy_pre(p3, conv_w, conv_b, D, col0, tl, ct)
    k_circ, s = _hyena_filter(L, D, w1, b1, w2, b2, w3, b3, w4, freq, min(tl, 512))
    a_data, a_filt, m2, hm, i2 = _dft_mats(N1, N2)
    yk = _fft1_filt(k_circ.reshape(S, 1, N1, N2, LANES), 1.0 / s, a_filt, dt)
    z6 = z.reshape(S, 2, B // 2, h, N2, LANES)
    y = _fft1_data(z6, a_data, dt)
    u = _fft_mid(y, yk, m2, hm)
    out = _fft_inv(u, z6, x1c.reshape(S, 2, B // 2, h, N2, LANES), bias.reshape(1, D), i2, dt)
    return out.reshape(S, B * L, LANES)


def _merge_kernel(x_ref, yl_ref, yh_ref, gl_ref, gh_ref, wl_ref, wh_ref, wo_ref, gf_ref, wr_ref, br_ref,
                  h_ref, hn_ref, idx_ref, gate_ref):
    a = jnp.dot(yl_ref[...].astype(BF16), wl_ref[...], preferred_element_type=F32)
    yh = _lane_cat([yh_ref[s] for s in range(yh_ref.shape[0])])
    b = jnp.dot(yh.astype(BF16), wh_ref[...], preferred_element_type=F32)
    mix = gl_ref[...] * a + gh_ref[...] * b
    h = x_ref[...] + jnp.dot(mix.astype(BF16), wo_ref[...], preferred_element_type=F32)
    h_ref[...] = h
    ms = jnp.mean(h * h, axis=-1, keepdims=True)
    hn = h * lax.rsqrt(ms + EPS) * gf_ref[...]
    nc = hn.shape[1] // LANES
    for c in range(nc):
        hn_ref[pl.ds(c, hn.shape[0], stride=nc), :] = hn[:, c * LANES:(c + 1) * LANES]
    ne = br_ref.shape[1]
    h_hi = hn.astype(BF16)
    h_lo = (hn - h_hi.astype(F32)).astype(BF16)
    a2 = jnp.dot(h_hi, wr_ref[...], preferred_element_type=F32)
    b2 = jnp.dot(h_lo, wr_ref[:, :LANES], preferred_element_type=F32)
    logits = (a2[:, :LANES] + a2[:, LANES:] + b2)[:, :ne] + br_ref[...]
    lane = lax.broadcasted_iota(jnp.int32, logits.shape, 1)
    vals = []
    for k in range(TOP_K):
        m = jnp.max(logits, axis=-1, keepdims=True)
        ix = jnp.min(jnp.where(logits == m, lane, ne), axis=-1, keepdims=True)
        vals.append(m)
        idx_ref[:, k:k + 1] = ix
        logits = jnp.where(lane == ix, -jnp.inf, logits)
    ex = [jnp.exp(v - vals[0]) for v in vals]
    tot = ex[0] + ex[1] + ex[2] + ex[3]
    for k in range(TOP_K):
        gate_ref[:, k:k + 1] = ex[k] / tot


def _merge(x2, y_lru, y_hy, p_all, w_lru_br, w_hy_br, w_out, g_ffn, w_router, b_router, tm):
    T, D = x2.shape
    ne = w_router.shape[1]
    gcol = p_all.shape[1] // D - 2
    row = lambda i: (i, 0)
    full = lambda i: (0, 0)
    w_hi = w_router.astype(BF16)
    w_lo = (w_router - w_hi.astype(F32)).astype(BF16)
    lane_pad = lambda w: jnp.pad(w, ((0, 0), (0, LANES - ne)))
    wr2 = jnp.concatenate([lane_pad(w_hi), lane_pad(w_lo)], axis=1)
    return pl.pallas_call(
        _merge_kernel,
        grid=(T // tm,),
        in_specs=[pl.BlockSpec((tm, D), row), pl.BlockSpec((tm, D), row),
                  pl.BlockSpec((D // LANES, tm, LANES), lambda i: (0, i, 0)),
                  pl.BlockSpec((tm, D), lambda i: (i, gcol)), pl.BlockSpec((tm, D), lambda i: (i, gcol + 1)),
                  pl.BlockSpec((D, D), full), pl.BlockSpec((D, D), full), pl.BlockSpec((D, D), full),
                  pl.BlockSpec((1, D), full), pl.BlockSpec((D, 2 * LANES), full), pl.BlockSpec((1, ne), full)],
        out_specs=[pl.BlockSpec((tm, D), row), pl.BlockSpec((tm * (D // LANES), LANES), row),
                   pl.BlockSpec((tm, TOP_K), row), pl.BlockSpec((tm, TOP_K), row)],
        out_shape=[jax.ShapeDtypeStruct((T, D), F32), jax.ShapeDtypeStruct((T * (D // LANES), LANES), F32),
                   jax.ShapeDtypeStruct((T, TOP_K), jnp.int32), jax.ShapeDtypeStruct((T, TOP_K), F32)],
        compiler_params=_cparams(("parallel",)),
    )(x2, y_lru, y_hy, p_all, p_all, w_lru_br.astype(BF16), w_hy_br.astype(BF16), w_out.astype(BF16),
      g_ffn.reshape(1, D), wr2, b_router.reshape(1, ne))


ROW_DMA_UNROLL = 8


def _for_rows(n, fn):
    def body(g, c):
        for k in range(ROW_DMA_UNROLL):
            fn(g * ROW_DMA_UNROLL + k, k % 2)
        return c

    lax.fori_loop(0, n // ROW_DMA_UNROLL, body, 0)


def _moe_kernel(ib_ref, ie_ref, bnd_ref, ni_ref, ord_ref, nord_ref, hn_hbm, gate_ref, wgu_ref, bgu_ref,
                wd_ref, bd_ref, yk_hbm, xbuf, ybuf, xs_ref, wgu_bf, wd_bf, gsem, ssem, *, nblk):
    i = pl.program_id(0)
    ni = ni_ref[0]
    nrow = xs_ref.shape[0]
    nc = xbuf.shape[1] // nrow
    dff = wd_ref.shape[0]

    def gather_copy(idx_ref, r, slot):
        tok = lax.shift_right_logical(idx_ref[0, 0, r], 2)
        return pltpu.make_async_copy(hn_hbm.at[tok], xbuf.at[slot, pl.ds(r * nc, nc), :], gsem.at[slot])

    def scatter_copy(r, slot):
        return pltpu.make_async_copy(ybuf.at[slot, pl.ds(r * nc, nc), :], yk_hbm.at[ord_ref[0, 0, r]],
                                     ssem.at[slot])

    @pl.when(i < ni)
    def _():
        b = ib_ref[i]
        e = ie_ref[i]
        slot = b % 2
        first = jnp.logical_or(i == 0, ib_ref[jnp.maximum(i - 1, 0)] != b)
        last = jnp.logical_or(i == ni - 1, ib_ref[jnp.minimum(i + 1, pl.num_programs(0) - 1)] != b)

        @pl.when(i == 0)
        def _():
            _for_rows(nrow, lambda r, p: gather_copy(ord_ref, r, 0).start(priority=p))

        @pl.when(first)
        def _():
            @pl.when(b + 1 < nblk)
            def _():
                _for_rows(nrow, lambda r, p: gather_copy(nord_ref, r, 1 - slot).start(priority=p))

            _for_rows(nrow, lambda r, p: gather_copy(ord_ref, r, slot).wait())

            @pl.when(b >= 2)
            def _():
                _for_rows(nrow, lambda r, p: scatter_copy(r, slot).wait())

        @pl.when(jnp.logical_or(i == 0, ie_ref[jnp.maximum(i - 1, 0)] != e))
        def _():
            wgu_bf[...] = wgu_ref[...].astype(BF16)
            wd_bf[...] = wd_ref[...].astype(BF16)

        for c in range(nc):
            xs_ref[:, c * LANES:(c + 1) * LANES] = xbuf[slot, pl.ds(c, nrow, stride=nc), :].astype(BF16)
        gu = jnp.dot(xs_ref[...], wgu_bf[...], preferred_element_type=F32) + bgu_ref[...]
        g = jnp.minimum(gu[:, :dff], SWIGLU_LIMIT)
        u = jnp.clip(gu[:, dff:], -SWIGLU_LIMIT, SWIGLU_LIMIT)
        act = (u + 1.0) * (g * jax.nn.sigmoid(SWIGLU_ALPHA * g))
        y = (jnp.dot(act.astype(BF16), wd_bf[...], preferred_element_type=F32) + bd_ref[...]) * gate_ref[...]
        row = lax.broadcasted_iota(jnp.int32, (nrow, 1), 0) + b * nrow
        mine = jnp.logical_and(row >= bnd_ref[e], row < bnd_ref[e + 1])

        @pl.when(first)
        def _():
            for c in range(nc):
                ybuf[slot, pl.ds(c, nrow, stride=nc), :] = jnp.where(mine, y[:, c * LANES:(c + 1) * LANES], 0.0)

        @pl.when(jnp.logical_not(first))
        def _():
            for c in range(nc):
                rows = pl.ds(c, nrow, stride=nc)
                ybuf[slot, rows, :] = jnp.where(mine, y[:, c * LANES:(c + 1) * LANES], ybuf[slot, rows, :])

        @pl.when(last)
        def _():
            _for_rows(nrow, lambda r, p: scatter_copy(r, slot).start(priority=p))

        @pl.when(i == ni - 1)
        def _():
            _for_rows(nrow, lambda r, p: scatter_copy(r, slot).wait())

            @pl.when(b >= 1)
            def _():
                _for_rows(nrow, lambda r, p: scatter_copy(r, 1 - slot).wait())


def _moe_items(hn, order, g_sorted, item_blk, item_exp, bounds, n_items, w_gate_up, b_gate_up, w_down, b_down):
    T, nc, _ = hn.shape
    D = nc * LANES
    N = order.shape[0]
    nblk = N // MOE_BLOCK
    n_steps = item_blk.shape[0]
    dff = w_down.shape[1]
    order3 = order.reshape(nblk, 1, MOE_BLOCK)
    grid_spec = pltpu.PrefetchScalarGridSpec(
        num_scalar_prefetch=4,
        grid=(n_steps,),
        in_specs=[
            pl.BlockSpec((1, 1, MOE_BLOCK), lambda i, ib, ie, bd, ni: (ib[i], 0, 0), memory_space=pltpu.SMEM),
            pl.BlockSpec((1, 1, MOE_BLOCK), lambda i, ib, ie, bd, ni: (jnp.minimum(ib[i] + 1, nblk - 1), 0, 0),
                         memory_space=pltpu.SMEM),
            pl.BlockSpec(memory_space=pl.ANY),
            pl.BlockSpec((MOE_BLOCK, 1), lambda i, ib, ie, bd, ni: (ib[i], 0)),
            pl.BlockSpec((None, D, 2 * dff), lambda i, ib, ie, bd, ni: (ie[i], 0, 0)),
            pl.BlockSpec((None, 1, 2 * dff), lambda i, ib, ie, bd, ni: (ie[i], 0, 0)),
            pl.BlockSpec((None, dff, D), lambda i, ib, ie, bd, ni: (ie[i], 0, 0)),
            pl.BlockSpec((None, 1, D), lambda i, ib, ie, bd, ni: (ie[i], 0, 0)),
        ],
        out_specs=pl.BlockSpec(memory_space=pl.ANY),
        scratch_shapes=[pltpu.VMEM((2, MOE_BLOCK * nc, LANES), F32), pltpu.VMEM((2, MOE_BLOCK * nc, LANES), F32),
                        pltpu.VMEM((MOE_BLOCK, D), BF16),
                        pltpu.VMEM((D, 2 * dff), BF16), pltpu.VMEM((dff, D), BF16),
                        pltpu.SemaphoreType.DMA((2,)), pltpu.SemaphoreType.DMA((2,))],
    )
    return pl.pallas_call(
        functools.partial(_moe_kernel, nblk=nblk),
        grid_spec=grid_spec,
        out_shape=jax.ShapeDtypeStruct((N, nc, LANES), F32),
        compiler_params=_cparams(("arbitrary",)),
    )(item_blk, item_exp, bounds, n_items, order3, order3, hn, g_sorted.reshape(N, 1),
      w_gate_up, b_gate_up.reshape(N_EXPERTS, 1, 2 * dff), w_down, b_down.reshape(N_EXPERTS, 1, D))


def _combine_kernel(yk_ref, h_ref, g_ref, o_ref, ys_ref, hs_ref):
    tq = h_ref.shape[0]
    nc = yk_ref.shape[1] // TOP_K
    y = yk_ref[:, 0:nc, :]
    for k in range(1, TOP_K):
        y = y + yk_ref[:, k * nc:(k + 1) * nc, :]
    ys_ref[...] = y.reshape(tq * nc, LANES)
    for c in range(nc):
        hs_ref[:, c * LANES:(c + 1) * LANES] = (h_ref[:, c * LANES:(c + 1) * LANES]
                                                + ys_ref[pl.ds(c, tq, stride=nc), :])
    h = hs_ref[...]
    ms = jnp.mean(h * h, axis=-1, keepdims=True)
    o_ref[...] = h * lax.rsqrt(ms + EPS) * g_ref[...]


def _combine(yk, h, g_final, tq):
    T, D = h.shape
    nc = yk.shape[1]
    return pl.pallas_call(
        _combine_kernel,
        grid=(T // tq,),
        in_specs=[pl.BlockSpec((tq, TOP_K * nc, LANES), lambda i: (i, 0, 0)),
                  pl.BlockSpec((tq, D), lambda i: (i, 0)),
                  pl.BlockSpec((1, D), lambda i: (0, 0))],
        out_specs=pl.BlockSpec((tq, D), lambda i: (i, 0)),
        out_shape=jax.ShapeDtypeStruct((T, D), F32),
        scratch_shapes=[pltpu.VMEM((tq * nc, LANES), F32), pltpu.VMEM((tq, D), F32)],
        compiler_params=_cparams(("parallel",)),
    )(yk.reshape(T, TOP_K * nc, LANES), h, g_final.reshape(1, D))


def _route(top_idx, gates):
    T = top_idx.shape[0]
    N = T * TOP_K
    nblk = N // MOE_BLOCK
    i32 = jnp.int32
    e_sorted, order, g_sorted = lax.sort(
        (top_idx.reshape(N), jnp.arange(N, dtype=i32), gates.reshape(N)), num_keys=1)
    bounds = jnp.sum(e_sorted[None, :] < jnp.arange(N_EXPERTS + 1, dtype=i32)[:, None], axis=1, dtype=i32)
    starts, ends = bounds[:-1], bounds[1:]
    first_blk = starts // MOE_BLOCK
    n_of = jnp.where(ends > starts, (ends - 1) // MOE_BLOCK - first_blk + 1, 0)
    item_end = jnp.cumsum(n_of)
    item_start = item_end - n_of
    n_items = item_end[-1:]
    n_steps = nblk + N_EXPERTS - 1
    i = jnp.arange(n_steps, dtype=i32)
    ie = jnp.minimum(jnp.sum(item_end[None, :] <= i[:, None], axis=1, dtype=i32), N_EXPERTS - 1)
    ib = jnp.clip(first_blk[ie] + i - item_start[ie], 0, nblk - 1).astype(i32)
    return order, g_sorted, ib, ie, bounds, n_items.astype(i32)


def kernel(x, g_mix, w_in, b_in, lru_conv_w, lru_conv_b, lru_wa, lru_ba, lru_wi, lru_bi, lru_lambda, w_lru_br, hy_conv_w, hy_conv_b, hy_w1, hy_b1, hy_w2, hy_b2, hy_w3, hy_b3, hy_w4, hy_sin_freq, hy_bias, w_hy_br, w_out, g_ffn, w_router, b_router, w_gate_up, b_gate_up, w_down, b_down, g_final):
    B, L, D = x.shape
    T = B * L
    x2 = x.reshape(T, D)
    tm = min(1024, T)
    p_all = _inproj(x2, g_mix, w_in, b_in, tm)
    p3 = p_all.reshape(B, L, p_all.shape[1])
    y_lru = _lru(p3, lru_conv_w, lru_conv_b, lru_wa, lru_ba, lru_wi, lru_bi, lru_lambda, min(512, L))
    y_hy = _hyena(p3, D, 2 * D, hy_conv_w, hy_conv_b, hy_w1, hy_b1, hy_w2, hy_b2, hy_w3, hy_b3, hy_w4,
                  hy_sin_freq, hy_bias, min(1024, L), min(256, D), min(256, D))
    h, hn, top_idx, gates = _merge(x2, y_lru.reshape(T, D), y_hy, p_all, w_lru_br, w_hy_br, w_out,
                                   g_ffn, w_router, b_router, min(512, T))
    order, g_sorted, item_blk, item_exp, bounds, n_items = _route(top_idx, gates)
    yk = _moe_items(hn.reshape(T, D // LANES, LANES), order, g_sorted, item_blk, item_exp, bounds, n_items,
                    w_gate_up, b_gate_up, w_down, b_down)
    out = _combine(yk, h, g_final, min(256, T))
    return out.reshape(B, L, D)
```

```python
import functools
import math

import numpy as np
import jax
import jax.numpy as jnp
from jax import lax
from jax.experimental import pallas as pl
from jax.experimental.pallas import tpu as pltpu

F32 = jnp.float32
BF16 = jnp.bfloat16

EPS = 1e-5
C_RGLRU = 8.0
RNN_BLOCK = 64
HY_EMB = 33
HY_BANDS = 16
HY_FAST_DECAY = 0.3
HY_SLOW_DECAY = 1.5
HY_TARGET = 1e-2
N_EXPERTS = 32
TOP_K = 4
SWIGLU_LIMIT = 7.0
SWIGLU_ALPHA = 1.702
MOE_BLOCK = 512

LANES = 128
SUBLANES = 8
FFT_N2 = 128
VMEM_LIMIT = 56 * 1024 * 1024


def _cparams(sem):
    return pltpu.CompilerParams(dimension_semantics=sem, vmem_limit_bytes=VMEM_LIMIT)


def _inproj_kernel(x_ref, g_ref, w_ref, b_ref, o_ref, u_ref):
    j = pl.program_id(1)

    @pl.when(j == 0)
    def _():
        x = x_ref[...]
        ms = jnp.mean(x * x, axis=-1, keepdims=True)
        u_ref[...] = (x * lax.rsqrt(ms + EPS) * g_ref[...]).astype(BF16)

    def proj():
        return jnp.dot(u_ref[...], w_ref[...], preferred_element_type=F32) + b_ref[...]

    @pl.when(j == 1)
    def _():
        o_ref[...] = jax.nn.gelu(proj())

    @pl.when(j >= 5)
    def _():
        o_ref[...] = 0.5 * jnp.tanh(0.5 * proj()) + 0.5

    @pl.when(jnp.logical_and(j != 1, j < 5))
    def _():
        o_ref[...] = proj()


def _inproj(x2, g_mix, w_in, b_in, tm):
    T, D = x2.shape
    n_in = w_in.shape[1]
    return pl.pallas_call(
        _inproj_kernel,
        grid=(T // tm, n_in // D),
        in_specs=[
            pl.BlockSpec((tm, D), lambda i, j: (i, 0)),
            pl.BlockSpec((1, D), lambda i, j: (0, 0)),
            pl.BlockSpec((D, D), lambda i, j: (0, j)),
            pl.BlockSpec((1, D), lambda i, j: (0, j)),
        ],
        out_specs=pl.BlockSpec((tm, D), lambda i, j: (i, j)),
        out_shape=jax.ShapeDtypeStruct((T, n_in), F32),
        scratch_shapes=[pltpu.VMEM((tm, D), BF16)],
        compiler_params=_cparams(("parallel", "arbitrary")),
    )(x2, g_mix.reshape(1, D), w_in.astype(BF16), b_in.reshape(1, n_in))


def _compose_groups(a, b, sub, forward):
    for s in (1, 2, 4):
        shift = s if forward else SUBLANES - s
        a_sh = pltpu.roll(a, shift, 1)
        b_sh = pltpu.roll(b, shift, 1)
        m = (sub >= s) if forward else (sub < SUBLANES - s)
        b = jnp.where(m, a * b_sh + b, b)
        a = jnp.where(m, a * a_sh, a)
    return a, b


def _lru_kernel(x_ref, g_ref, cw_ref, cb_ref, wg_ref, bg_ref, lam_ref, o_ref,
                xp_ref, xr_ref, hb_ref, ge_ref, hin_ref, *, L, R):
    nchunk = L // R
    ng = R // SUBLANES
    NG = L // SUBLANES
    nsg = NG // SUBLANES
    zeros8 = jnp.zeros((SUBLANES, LANES), F32)
    xp_ref[0:SUBLANES, :] = zeros8
    xp_ref[SUBLANES + L:2 * SUBLANES + L, :] = zeros8
    xp_ref[SUBLANES:SUBLANES + L, :] = x_ref[...]

    sub = lax.broadcasted_iota(jnp.int32, (ng, SUBLANES, LANES), 1)
    c8 = [-C_RGLRU * jax.nn.softplus(-lam_ref[d]) for d in range(2)]
    cb = cb_ref[...]
    cw = [cw_ref[k:k + 1, :] for k in range(4)]

    def conv(c, _):
        t0 = pl.multiple_of(c * R, R)
        xr = cb
        for k in range(4):
            xr = xr + cw[k] * xp_ref[pl.ds(t0 + (SUBLANES - 2 + k), R), :]
        xr_ref[pl.ds(t0, R), :] = xr
        return 0

    lax.fori_loop(0, nchunk, conv, 0)

    def level1(c, _):
        t0 = pl.multiple_of(c * R, R)
        rows = pl.ds(t0, R)
        xr = xr_ref[rows, :]
        gz = jnp.dot(xr.astype(BF16), wg_ref[...], preferred_element_type=F32) + bg_ref[...]
        for d in range(2):
            base = 2 * d * LANES
            r = 0.5 * jnp.tanh(0.5 * gz[:, base:base + LANES]) + 0.5
            i = 0.5 * jnp.tanh(0.5 * gz[:, base + LANES:base + 2 * LANES]) + 0.5
            log_a = c8[d] * r
            a = jnp.exp(log_a)
            b = jnp.sqrt((1.0 + a * a) * jnp.tanh(-log_a)) * (i * xr)
            a, b = _compose_groups(a.reshape(ng, SUBLANES, LANES), b.reshape(ng, SUBLANES, LANES), sub, d == 0)
            a_ref, b_ref = (xp_ref, o_ref) if d == 0 else (xr_ref, hb_ref)
            a_ref[rows, :] = a.reshape(R, LANES)
            b_ref[rows, :] = b.reshape(R, LANES)
        gsl = pl.ds(pl.multiple_of(c * ng, ng), ng)
        ge_ref[0, gsl, :] = xp_ref[pl.ds(t0 + SUBLANES - 1, ng, stride=SUBLANES), :]
        ge_ref[1, gsl, :] = o_ref[pl.ds(t0 + SUBLANES - 1, ng, stride=SUBLANES), :]
        ge_ref[2, gsl, :] = xr_ref[pl.ds(t0, ng, stride=SUBLANES), :]
        ge_ref[3, gsl, :] = hb_ref[pl.ds(t0, ng, stride=SUBLANES), :]
        return 0

    lax.fori_loop(0, nchunk, level1, 0)

    sub2 = lax.broadcasted_iota(jnp.int32, (nsg, SUBLANES, LANES), 1)
    for d in range(2):
        a, b = _compose_groups(ge_ref[2 * d].reshape(nsg, SUBLANES, LANES),
                               ge_ref[2 * d + 1].reshape(nsg, SUBLANES, LANES), sub2, d == 0)
        ge_ref[2 * d] = a.reshape(NG, LANES)
        ge_ref[2 * d + 1] = b.reshape(NG, LANES)

    hin_ref[0, 0:SUBLANES, :] = zeros8
    hin_ref[1, SUBLANES + NG:2 * SUBLANES + NG, :] = zeros8

    def walk(j, hc):
        hf, hk = hc
        rf = pl.multiple_of(j * SUBLANES, SUBLANES)
        h8 = ge_ref[1, pl.ds(rf, SUBLANES), :] + ge_ref[0, pl.ds(rf, SUBLANES), :] * hf
        hin_ref[0, pl.ds(SUBLANES + rf, SUBLANES), :] = h8
        hf = jnp.broadcast_to(h8[SUBLANES - 1:SUBLANES, :], (SUBLANES, LANES))
        rb = pl.multiple_of((nsg - 1 - j) * SUBLANES, SUBLANES)
        k8 = ge_ref[3, pl.ds(rb, SUBLANES), :] + ge_ref[2, pl.ds(rb, SUBLANES), :] * hk
        hin_ref[1, pl.ds(SUBLANES + rb, SUBLANES), :] = k8
        hk = jnp.broadcast_to(k8[0:1, :], (SUBLANES, LANES))
        return hf, hk

    lax.fori_loop(0, nsg, walk, (zeros8, zeros8), unroll=8)

    def finish(c, _):
        t0 = pl.multiple_of(c * R, R)
        rows = pl.ds(t0, R)
        g0 = c * ng
        hin_f = hin_ref[0, pl.ds(g0 + SUBLANES - 1, ng), :]
        hin_b = hin_ref[1, pl.ds(g0 + SUBLANES + 1, ng), :]
        shape3 = (ng, SUBLANES, LANES)
        h_f = (o_ref[rows, :].reshape(shape3)
               + xp_ref[rows, :].reshape(shape3) * jnp.broadcast_to(hin_f[:, None, :], shape3))
        h_b = (hb_ref[rows, :].reshape(shape3)
               + xr_ref[rows, :].reshape(shape3) * jnp.broadcast_to(hin_b[:, None, :], shape3))
        o_ref[rows, :] = g_ref[rows, :] * (h_f + h_b).reshape(R, LANES)
        return 0

    lax.fori_loop(0, nchunk, finish, 0)


def _lru(p3, conv_w, conv_b, wa, ba, wi, bi, lam, R):
    B, L, _ = p3.shape
    D = conv_w.shape[1]
    nt = D // LANES
    hpt = LANES // RNN_BLOCK

    def blockdiag(w):
        w = w.reshape(2, nt, hpt, RNN_BLOCK, RNN_BLOCK)
        eye = jnp.eye(hpt, dtype=w.dtype)
        return jnp.einsum('dthij,hg->dthigj', w, eye).reshape(2, nt, LANES, LANES)

    wa_d, wi_d = blockdiag(wa), blockdiag(wi)
    wg = jnp.concatenate([wa_d[0], wi_d[0], wa_d[1], wi_d[1]], axis=-1).astype(BF16)
    tile = lambda v: v.reshape(2, nt, 1, LANES)
    bg = jnp.concatenate([tile(ba)[0], tile(bi)[0], tile(ba)[1], tile(bi)[1]], axis=-1)
    lam4 = lam.reshape(2, nt, 1, LANES)
    NG = L // SUBLANES
    return pl.pallas_call(
        functools.partial(_lru_kernel, L=L, R=R),
        grid=(B, nt),
        in_specs=[
            pl.BlockSpec((None, L, LANES), lambda b, c: (b, 0, c)),
            pl.BlockSpec((None, L, LANES), lambda b, c: (b, 0, nt + c)),
            pl.BlockSpec((4, LANES), lambda b, c: (0, c)),
            pl.BlockSpec((1, LANES), lambda b, c: (0, c)),
            pl.BlockSpec((None, LANES, 4 * LANES), lambda b, c: (c, 0, 0)),
            pl.BlockSpec((None, 1, 4 * LANES), lambda b, c: (c, 0, 0)),
            pl.BlockSpec((2, None, 1, LANES), lambda b, c: (0, c, 0, 0)),
        ],
        out_specs=pl.BlockSpec((None, L, LANES), lambda b, c: (b, 0, c)),
        out_shape=jax.ShapeDtypeStruct((B, L, D), F32),
        scratch_shapes=[
            pltpu.VMEM((L + 2 * SUBLANES, LANES), F32),
            pltpu.VMEM((L, LANES), F32),
            pltpu.VMEM((L, LANES), F32),
            pltpu.VMEM((4, NG, LANES), F32),
            pltpu.VMEM((2, NG + 2 * SUBLANES, LANES), F32),
        ],
        compiler_params=_cparams(("parallel", "parallel")),
    )(p3, p3, conv_w, conv_b.reshape(1, D), wg, bg, lam4)


def _hy_pre_kernel(v_ref, x1_ref, x2_ref, vp_ref, x1p_ref, x2p_ref, vn_ref, x1n_ref, x2n_ref,
                   w_ref, b_ref, z_ref, x1c_ref, *, nt):
    i = pl.program_id(1)
    tl = v_ref.shape[0]
    row = lax.broadcasted_iota(jnp.int32, v_ref.shape, 0)
    first = i == 0
    last = i == nt - 1

    def conv(c, m_ref, p_ref, n_ref):
        x = m_ref[...]
        prev = jnp.where(first, 0.0, p_ref[SUBLANES - 1:SUBLANES, :])
        nxt = jnp.where(last, 0.0, n_ref[0:1, :])
        xm = jnp.where(row == 0, prev, pltpu.roll(x, 1, 0))
        xn = jnp.where(row == tl - 1, nxt, pltpu.roll(x, tl - 1, 0))
        return b_ref[c] + w_ref[c, 0:1, :] * xm + w_ref[c, 1:2, :] * x + w_ref[c, 2:3, :] * xn

    v = conv(0, v_ref, vp_ref, vn_ref)
    x1c = conv(1, x1_ref, x1p_ref, x1n_ref)
    z = conv(2, x2_ref, x2p_ref, x2n_ref) * v
    for s in range(z_ref.shape[0]):
        x1c_ref[s] = x1c[:, s * LANES:(s + 1) * LANES]
        z_ref[s] = z[:, s * LANES:(s + 1) * LANES]


def _hy_pre(p3, conv_w, conv_b, D, col0, tl, ct):
    B, L, n_in = p3.shape
    nt = L // tl
    ncb = D // ct
    g8 = tl // SUBLANES
    p4 = p3.reshape(B, L // SUBLANES, SUBLANES, n_in)
    w3 = conv_w.reshape(3, 3, D).transpose(1, 0, 2)
    b3 = conv_b.reshape(3, 1, D)
    cb0 = col0 // ct

    def main(part):
        return pl.BlockSpec((None, tl, ct), lambda b, i, c: (b, i, cb0 + part * ncb + c))

    def prev(part):
        return pl.BlockSpec((None, None, SUBLANES, ct),
                            lambda b, i, c: (b, jnp.maximum(i * g8 - 1, 0), 0, cb0 + part * ncb + c))

    def nxt(part):
        return pl.BlockSpec((None, None, SUBLANES, ct),
                            lambda b, i, c: (b, jnp.minimum((i + 1) * g8, L // SUBLANES - 1), 0, cb0 + part * ncb + c))

    out = pl.BlockSpec((ct // LANES, None, tl, LANES), lambda b, i, c: (c, b, i, 0))
    return pl.pallas_call(
        functools.partial(_hy_pre_kernel, nt=nt),
        grid=(B, nt, ncb),
        in_specs=[main(0), main(1), main(2), prev(0), prev(1), prev(2), nxt(0), nxt(1), nxt(2),
                  pl.BlockSpec((3, 3, ct), lambda b, i, c: (0, 0, c)),
                  pl.BlockSpec((3, 1, ct), lambda b, i, c: (0, 0, c))],
        out_specs=[out, out],
        out_shape=[jax.ShapeDtypeStruct((D // LANES, B, L, LANES), F32)] * 2,
        compiler_params=_cparams(("parallel", "parallel", "parallel")),
    )(p3, p3, p3, p4, p4, p4, p4, p4, p4, w3, b3)


def _filt_kernel(feat_ref, t_ref, w1_ref, b1_ref, w2_ref, b2_ref, w3_ref, b3_ref, w4f_ref, w4b_ref, fr_ref, dl_ref,
                 k_ref, s_ref, *, nb):
    i = pl.program_id(0)
    hp = lax.Precision.HIGHEST
    fr = fr_ref[...]
    fwd = i < nb
    hh = jnp.sin(fr * (jnp.dot(feat_ref[...], w1_ref[...], precision=hp, preferred_element_type=F32) + b1_ref[...]))
    hh = jnp.sin(fr * (jnp.dot(hh, w2_ref[...], precision=hp, preferred_element_type=F32) + b2_ref[...]))
    hh = jnp.sin(fr * (jnp.dot(hh, w3_ref[...], precision=hp, preferred_element_type=F32) + b3_ref[...]))
    w4 = jnp.where(fwd, w4f_ref[...], w4b_ref[...])
    decay = jnp.exp(-t_ref[...] * jnp.abs(dl_ref[...]))
    hb16 = hh.astype(BF16)
    k = jnp.dot(hb16, w4.astype(BF16), preferred_element_type=F32) * decay
    row = lax.broadcasted_iota(jnp.int32, k.shape, 0)
    kb0 = (jnp.dot(hb16[0:BF16_ROWS], w4b_ref[...].astype(BF16), preferred_element_type=F32)
           * decay[0:BF16_ROWS])[0:1]
    k = jnp.where(jnp.logical_and(row == 0, i == 0), k + kb0, k)
    k = jnp.where(jnp.logical_and(row == 0, i == nb), 0.0, k)
    for s in range(k_ref.shape[0]):
        k_ref[s] = k[:, s * LANES:(s + 1) * LANES]
    part = jnp.sum(jnp.abs(k), axis=0, keepdims=True)

    @pl.when(i == 0)
    def _():
        s_ref[...] = part

    @pl.when(i > 0)
    def _():
        s_ref[...] += part


def _hyena_filter(L, D, w1, b1, w2, b2, w3, b3, w4, freq, tl):
    t = jnp.linspace(0.0, 1.0, L, dtype=F32)[:, None]
    w = 2.0 * math.pi * jnp.arange(L, dtype=F32)[:, None] / L
    bands = jnp.linspace(1e-4, HY_BANDS - 1, HY_BANDS, dtype=F32)[None, :]
    feats = jnp.concatenate([t, jnp.cos(w * bands), -jnp.sin(w * bands)], axis=-1)
    deltas = jnp.linspace(math.log(HY_TARGET) / HY_SLOW_DECAY, math.log(HY_TARGET) / HY_FAST_DECAY, D, dtype=F32)
    rev = lambda a: jnp.roll(jnp.flip(a, 0), 1, 0)
    feats2 = jnp.concatenate([feats, rev(feats)], axis=0)
    t2 = jnp.concatenate([t, rev(t)], axis=0)
    ff = w1.shape[1]
    nb = L // tl
    full = lambda shape: pl.BlockSpec(shape, lambda i: (0,) * len(shape))
    return pl.pallas_call(
        functools.partial(_filt_kernel, nb=nb),
        grid=(2 * nb,),
        in_specs=[pl.BlockSpec((tl, HY_EMB), lambda i: (i, 0)), pl.BlockSpec((tl, 1), lambda i: (i, 0)),
                  full((HY_EMB, ff)), full((1, ff)), full((ff, ff)), full((1, ff)), full((ff, ff)), full((1, ff)),
                  full((ff, D)), full((ff, D)), full((1, ff)), full((1, D))],
        out_specs=[pl.BlockSpec((D // LANES, tl, LANES), lambda i: (0, i, 0)), pl.BlockSpec((1, D), lambda i: (0, 0))],
        out_shape=[jax.ShapeDtypeStruct((D // LANES, 2 * L, LANES), F32), jax.ShapeDtypeStruct((1, D), F32)],
        compiler_params=_cparams(("arbitrary",)),
    )(feats2, t2, w1, b1.reshape(1, ff), w2, b2.reshape(1, ff), w3, b3.reshape(1, ff), w4[:, :D], w4[:, D:],
      freq.reshape(1, ff), deltas.reshape(1, D))


def _angles(rows, cols, n):
    r = (rows.astype(np.int64) * cols.astype(np.int64)) % n
    th = r.astype(np.float64) * (2.0 * math.pi / n)
    return jnp.asarray(np.cos(th), F32), jnp.asarray(np.sin(th), F32)


def _dft_mats(N1, N2):
    N = N1 * N2
    h = N1 // 2
    ar = lambda n: np.arange(n)
    def add(c1, s1, c2, s2):
        return c1 * c2 - s1 * s2, s1 * c2 + c1 * s2

    cp, sp = _angles(ar(N1)[:, None], ar(N1)[None, :], N1)
    cq, sq = _angles(ar(N2)[:, None], ar(N1)[None, :], N)
    cf, sf = add(cp[None], sp[None], cq[:, :, None], sq[:, :, None])
    c, s = cf[:, :, :h], sf[:, :, :h]
    a_data = jnp.concatenate([jnp.concatenate([c, s], -1), jnp.concatenate([-s, c], -1)], 1)
    a_filt = jnp.concatenate([cf, -sf], 1)
    c, s = _angles(ar(N2)[:, None], ar(N2)[None, :], N2)
    m2 = jnp.concatenate([jnp.concatenate([c, s], -1), jnp.concatenate([-s, c], -1)], 0)
    cr, sr = _angles(ar(N1)[:, None], ar(N2)[None, :], N)
    cs, ss = _angles(ar(N2)[:, None], ar(N2)[None, :], N2)
    c, s = add(cr[:, :, None], sr[:, :, None], cs[None], ss[None])
    hm =jnp.concatenate([jnp.concatenate([c, -s], -1), jnp.concatenate([s, c], -1)], 1)
    c, s = _angles(ar(h)[:, None], ar(N1)[None, :], N1)
    i2 = jnp.concatenate([jnp.concatenate([c, -s], -1), jnp.concatenate([s, c], -1)], 0) / N
    return (a_data.astype(BF16), a_filt.astype(BF16), m2.astype(BF16), hm.astype(BF16), i2.astype(BF16))


BF16_ROWS = 16


def _lane_cat(parts):
    return parts[0] if len(parts) == 1 else jnp.concatenate(parts, axis=1)


def _flatten_in(v, dst_ref, region, nrows):
    for hf in range(BF16_ROWS // SUBLANES):
        dst_ref[pl.ds((2 * region + hf) * nrows, nrows), :] = (
            v[:, hf * SUBLANES:(hf + 1) * SUBLANES, :].reshape(nrows, LANES))


def _unflatten_out(src_ref, region, n, nrows):
    return jnp.concatenate(
        [src_ref[pl.ds((2 * region + hf) * nrows, nrows), :].reshape(n, SUBLANES, LANES)
         for hf in range(BF16_ROWS // SUBLANES)], axis=1)


def _tstep(region, j, n, nrows):
    hf, jj = divmod(j, SUBLANES)
    return pl.ds((2 * region + hf) * nrows + jj, n, stride=SUBLANES)


def _fft1_kernel(*refs, scaled):
    a_ref = refs[0]
    inv_s = refs[1][...] if scaled else None
    x_ref, o_ref, xs_ref, os_ref = refs[-4:]
    ns, nin, rows = x_ref.shape[0], x_ref.shape[1], x_ref.shape[2]
    n1 = o_ref.shape[2]
    xr, orr = rows * SUBLANES, n1 * SUBLANES
    for s in range(ns):
        for q in range(nin):
            _flatten_in(x_ref[s, q], xs_ref, s * nin + q, xr)
    for j in range(BF16_ROWS):
        parts = [_lane_cat([xs_ref[_tstep(s * nin + q, j, rows, xr), :] for s in range(ns)]) for q in range(nin)]
        x = parts[0] if nin == 1 else jnp.concatenate(parts, axis=0)
        if scaled:
            x = x * inv_s
        y = jnp.dot(a_ref[j], x.astype(BF16), preferred_element_type=F32)
        for s in range(ns):
            for c in range(2):
                os_ref[_tstep(s * 2 + c, j, n1, orr), :] = y[c * n1:(c + 1) * n1, s * LANES:(s + 1) * LANES]
    for s in range(ns):
        for c in range(2):
            o_ref[s, c] = _unflatten_out(os_ref, s * 2 + c, n1, orr).astype(BF16)


def _fft1_data(z6, a_data, dt):
    S, _, P, h, N2, _ = z6.shape
    N1 = 2 * h
    ns = dt // LANES
    jb = BF16_ROWS
    return pl.pallas_call(
        functools.partial(_fft1_kernel, scaled=False),
        grid=(N2 // jb, P, S // ns),
        in_specs=[pl.BlockSpec((jb, 2 * N1, N1), lambda j, p, c: (j, 0, 0)),
                  pl.BlockSpec((ns, 2, None, h, jb, LANES), lambda j, p, c: (c, 0, p, 0, j, 0))],
        out_specs=pl.BlockSpec((None, ns, 2, N1, jb, LANES), lambda j, p, c: (p, c, 0, 0, j, 0)),
        out_shape=jax.ShapeDtypeStruct((P, S, 2, N1, N2, LANES), BF16),
        scratch_shapes=[pltpu.VMEM((ns * 2 * h * jb, LANES), F32),
                        pltpu.VMEM((ns * 2 * N1 * jb, LANES), F32)],
        compiler_params=_cparams(("parallel", "parallel", "parallel")),
    )(a_data, z6)


def _fft1_filt(k5, inv_s, a_filt, dt):
    S, _, N1, N2, _ = k5.shape
    ns = dt // LANES
    jb = BF16_ROWS
    return pl.pallas_call(
        functools.partial(_fft1_kernel, scaled=True),
        grid=(N2 // jb, S // ns),
        in_specs=[pl.BlockSpec((jb, 2 * N1, N1), lambda j, c: (j, 0, 0)),
                  pl.BlockSpec((1, dt), lambda j, c: (0, c)),
                  pl.BlockSpec((ns, 1, N1, jb, LANES), lambda j, c: (c, 0, 0, j, 0))],
        out_specs=pl.BlockSpec((ns, 2, N1, jb, LANES), lambda j, c: (c, 0, 0, j, 0)),
        out_shape=jax.ShapeDtypeStruct((S, 2, N1, N2, LANES), BF16),
        scratch_shapes=[pltpu.VMEM((ns * N1 * jb, LANES), F32),
                        pltpu.VMEM((ns * 2 * N1 * jb, LANES), F32)],
        compiler_params=_cparams(("parallel", "parallel")),
    )(a_filt, inv_s, k5)


def _fft_mid_kernel(m2_ref, h_ref, y_ref, yk_ref, o_ref, kf_ref):
    S, n2 = y_ref.shape[0], y_ref.shape[2]

    def stacked(ref):
        return jnp.concatenate([_lane_cat([ref[s, c] for s in range(S)]) for c in range(2)], axis=0)

    @pl.when(pl.program_id(1) == 0)
    def _():
        kf_ref[...] = jnp.dot(m2_ref[...], stacked(yk_ref), preferred_element_type=F32)

    x = jnp.dot(m2_ref[...], stacked(y_ref), preferred_element_type=F32)
    xr, xi = x[:n2], x[n2:]
    kr, ki = kf_ref[0:n2, :], kf_ref[n2:, :]
    pr = xr * kr - xi * ki
    pi = xr * ki + xi * kr
    p = jnp.concatenate([pr, pi], axis=0).astype(BF16)
    u = jnp.dot(h_ref[...], p, preferred_element_type=F32).astype(BF16)
    for s in range(S):
        for c in range(2):
            o_ref[s, c] = u[c * n2:(c + 1) * n2, s * LANES:(s + 1) * LANES]


def _fft_mid(y, yk, m2, hm):
    P, S, _, N1, N2, _ = y.shape
    return pl.pallas_call(
        _fft_mid_kernel,
        grid=(N1, P),
        in_specs=[pl.BlockSpec((2 * N2, 2 * N2), lambda k, p: (0, 0)),
                  pl.BlockSpec((None, 2 * N2, 2 * N2), lambda k, p: (k, 0, 0)),
                  pl.BlockSpec((None, S, 2, None, N2, LANES), lambda k, p: (p, 0, 0, k, 0, 0)),
                  pl.BlockSpec((S, 2, None, N2, LANES), lambda k, p: (0, 0, k, 0, 0))],
        out_specs=pl.BlockSpec((None, S, 2, None, N2, LANES), lambda k, p: (p, 0, 0, k, 0, 0)),
        out_shape=jax.ShapeDtypeStruct((P, S, 2, N1, N2, LANES), BF16),
        scratch_shapes=[pltpu.VMEM((2 * N2, S * LANES), F32)],
        compiler_params=_cparams(("parallel", "arbitrary")),
    )(m2, hm, y, yk)


def _fft_inv_kernel(i2_ref, u_ref, z_ref, x1_ref, bias_ref, o_ref, us_ref, zs_ref, xs_ref, os_ref):
    ns, n1 = u_ref.shape[0], u_ref.shape[2]
    h = o_ref.shape[2]
    ur, hr = n1 * SUBLANES, h * SUBLANES
    bias = bias_ref[...]
    for s in range(ns):
        for c in range(2):
            _flatten_in(u_ref[s, c].astype(F32), us_ref, s * 2 + c, ur)
            _flatten_in(z_ref[s, c], zs_ref, s * 2 + c, hr)
            _flatten_in(x1_ref[s, c], xs_ref, s * 2 + c, hr)
    for j in range(BF16_ROWS):
        up = jnp.concatenate([_lane_cat([us_ref[_tstep(s * 2 + c, j, n1, ur), :] for s in range(ns)])
                              for c in range(2)], axis=0)
        y = jnp.dot(i2_ref[...], up.astype(BF16), preferred_element_type=F32)
        for s in range(ns):
            lanes = slice(s * LANES, (s + 1) * LANES)
            for q in range(2):
                t = _tstep(s * 2 + q, j, h, hr)
                os_ref[t, :] = xs_ref[t, :] * (y[q * h:(q + 1) * h, lanes] + bias[:, lanes] * zs_ref[t, :])
    for s in range(ns):
        for q in range(2):
            o_ref[s, q] = _unflatten_out(os_ref, s * 2 + q, h, hr)


def _fft_inv(u, z6, x16, bias, i2, dt):
    P, S, _, N1, N2, _ = u.shape
    h = N1 // 2
    ns = dt // LANES
    jb = BF16_ROWS
    pair = pl.BlockSpec((ns, 2, None, h, jb, LANES), lambda j, p, c: (c, 0, p, 0, j, 0))
    return pl.pallas_call(
        _fft_inv_kernel,
        grid=(N2 // jb, P, S // ns),
        in_specs=[pl.BlockSpec((N1, 2 * N1), lambda j, p, c: (0, 0)),
                  pl.BlockSpec((None, ns, 2, N1, jb, LANES), lambda j, p, c: (p, c, 0, 0, j, 0)),
                  pair, pair,
                  pl.BlockSpec((1, dt), lambda j, p, c: (0, c))],
        out_specs=pair,
        out_shape=jax.ShapeDtypeStruct((S, 2, P, h, N2, LANES), F32),
        scratch_shapes=[pltpu.VMEM((ns * 2 * N1 * jb, LANES), F32)]
        + [pltpu.VMEM((ns * 2 * h * jb, LANES), F32)] * 3,
        compiler_params=_cparams(("parallel", "parallel", "parallel")),
    )(i2, u, z6, x16, bias)


def _hyena(p3, D, col0, conv_w, conv_b, w1, b1, w2, b2, w3, b3, w4, freq, bias, tl, ct, dt):
    B, L, _ = p3.shape
    N2 = FFT_N2
    N1 = 2 * L // N2
    h = N1 // 2
    S = D // LANES
    z, x1c = _hy_pre(p3, conv_w, conv_b, D, col0, tl, ct)
    k_circ, s = _hyena_filter(L, D, w1, b1, w2, b2, w3, b3, w4, freq, min(tl, 512))
    a_data, a_filt, m2, hm, i2 = _dft_mats(N1, N2)
    yk = _fft1_filt(k_circ.reshape(S, 1, N1, N2, LANES), 1.0 / s, a_filt, dt)
    z6 = z.reshape(S, 2, B // 2, h, N2, LANES)
    y = _fft1_data(z6, a_data, dt)
    u = _fft_mid(y, yk, m2, hm)
    out = _fft_inv(u, z6, x1c.reshape(S, 2, B // 2, h, N2, LANES), bias.reshape(1, D), i2, dt)
    return out.reshape(S, B * L, LANES)


def _merge_kernel(x_ref, yl_ref, yh_ref, gl_ref, gh_ref, wl_ref, wh_ref, wo_ref, gf_ref, wr_ref, br_ref,
                  h_ref, hn_ref, idx_ref, gate_ref):
    a = jnp.dot(yl_ref[...].astype(BF16), wl_ref[...], preferred_element_type=F32)
    yh = _lane_cat([yh_ref[s] for s in range(yh_ref.shape[0])])
    b = jnp.dot(yh.astype(BF16), wh_ref[...], preferred_element_type=F32)
    mix = gl_ref[...] * a + gh_ref[...] * b
    h = x_ref[...] + jnp.dot(mix.astype(BF16), wo_ref[...], preferred_element_type=F32)
    h_ref[...] = h
    ms = jnp.mean(h * h, axis=-1, keepdims=True)
    hn = h * lax.rsqrt(ms + EPS) * gf_ref[...]
    nc = hn.shape[1] // LANES
    for c in range(nc):
        hn_ref[pl.ds(c, hn.shape[0], stride=nc), :] = hn[:, c * LANES:(c + 1) * LANES]
    ne = br_ref.shape[1]
    h_hi = hn.astype(BF16)
    h_lo = (hn - h_hi.astype(F32)).astype(BF16)
    a2 = jnp.dot(h_hi, wr_ref[...], preferred_element_type=F32)
    b2 = jnp.dot(h_lo, wr_ref[:, :LANES], preferred_element_type=F32)
    logits = (a2[:, :LANES] + a2[:, LANES:] + b2)[:, :ne] + br_ref[...]
    lane = lax.broadcasted_iota(jnp.int32, logits.shape, 1)
    vals = []
    for k in range(TOP_K):
        m = jnp.max(logits, axis=-1, keepdims=True)
        ix = jnp.min(jnp.where(logits == m, lane, ne), axis=-1, keepdims=True)
        vals.append(m)
        idx_ref[:, k:k + 1] = ix
        logits = jnp.where(lane == ix, -jnp.inf, logits)
    ex = [jnp.exp(v - vals[0]) for v in vals]
    tot = ex[0] + ex[1] + ex[2] + ex[3]
    for k in range(TOP_K):
        gate_ref[:, k:k + 1] = ex[k] / tot


def _merge(x2, y_lru, y_hy, p_all, w_lru_br, w_hy_br, w_out, g_ffn, w_router, b_router, tm):
    T, D = x2.shape
    ne = w_router.shape[1]
    gcol = p_all.shape[1] // D - 2
    row = lambda i: (i, 0)
    full = lambda i: (0, 0)
    w_hi = w_router.astype(BF16)
    w_lo = (w_router - w_hi.astype(F32)).astype(BF16)
    lane_pad = lambda w: jnp.pad(w, ((0, 0), (0, LANES - ne)))
    wr2 = jnp.concatenate([lane_pad(w_hi), lane_pad(w_lo)], axis=1)
    return pl.pallas_call(
        _merge_kernel,
        grid=(T // tm,),
        in_specs=[pl.BlockSpec((tm, D), row), pl.BlockSpec((tm, D), row),
                  pl.BlockSpec((D // LANES, tm, LANES), lambda i: (0, i, 0)),
                  pl.BlockSpec((tm, D), lambda i: (i, gcol)), pl.BlockSpec((tm, D), lambda i: (i, gcol + 1)),
                  pl.BlockSpec((D, D), full), pl.BlockSpec((D, D), full), pl.BlockSpec((D, D), full),
                  pl.BlockSpec((1, D), full), pl.BlockSpec((D, 2 * LANES), full), pl.BlockSpec((1, ne), full)],
        out_specs=[pl.BlockSpec((tm, D), row), pl.BlockSpec((tm * (D // LANES), LANES), row),
                   pl.BlockSpec((tm, TOP_K), row), pl.BlockSpec((tm, TOP_K), row)],
        out_shape=[jax.ShapeDtypeStruct((T, D), F32), jax.ShapeDtypeStruct((T * (D // LANES), LANES), F32),
                   jax.ShapeDtypeStruct((T, TOP_K), jnp.int32), jax.ShapeDtypeStruct((T, TOP_K), F32)],
        compiler_params=_cparams(("parallel",)),
    )(x2, y_lru, y_hy, p_all, p_all, w_lru_br.astype(BF16), w_hy_br.astype(BF16), w_out.astype(BF16),
      g_ffn.reshape(1, D), wr2, b_router.reshape(1, ne))


ROW_DMA_UNROLL = 8


def _for_rows(n, fn):
    def body(g, c):
        for k in range(ROW_DMA_UNROLL):
            fn(g * ROW_DMA_UNROLL + k, k % 2)
        return c

    lax.fori_loop(0, n // ROW_DMA_UNROLL, body, 0)


def _moe_kernel(ib_ref, ie_ref, bnd_ref, ni_ref, ord_ref, nord_ref, hn_hbm, gate_ref, wgu_ref, bgu_ref,
                wd_ref, bd_ref, yk_hbm, xbuf, ybuf, xs_ref, wgu_bf, wd_bf, gsem, ssem, *, nblk):
    i = pl.program_id(0)
    ni = ni_ref[0]
    nrow = xs_ref.shape[0]
    nc = xbuf.shape[1] // nrow
    dff = wd_ref.shape[0]

    def gather_copy(idx_ref, r, slot):
        tok = lax.shift_right_logical(idx_ref[0, 0, r], 2)
        return pltpu.make_async_copy(hn_hbm.at[tok], xbuf.at[slot, pl.ds(r * nc, nc), :], gsem.at[slot])

    def scatter_copy(r, slot):
        return pltpu.make_async_copy(ybuf.at[slot, pl.ds(r * nc, nc), :], yk_hbm.at[ord_ref[0, 0, r]],
                                     ssem.at[slot])

    @pl.when(i < ni)
    def _():
        b = ib_ref[i]
        e = ie_ref[i]
        slot = b % 2
        first = jnp.logical_or(i == 0, ib_ref[jnp.maximum(i - 1, 0)] != b)
        last = jnp.logical_or(i == ni - 1, ib_ref[jnp.minimum(i + 1, pl.num_programs(0) - 1)] != b)

        @pl.when(i == 0)
        def _():
            _for_rows(nrow, lambda r, p: gather_copy(ord_ref, r, 0).start(priority=p))

        @pl.when(first)
        def _():
            @pl.when(b + 1 < nblk)
            def _():
                _for_rows(nrow, lambda r, p: gather_copy(nord_ref, r, 1 - slot).start(priority=p))

            _for_rows(nrow, lambda r, p: gather_copy(ord_ref, r, slot).wait())

            @pl.when(b >= 2)
            def _():
                _for_rows(nrow, lambda r, p: scatter_copy(r, slot).wait())

        @pl.when(jnp.logical_or(i == 0, ie_ref[jnp.maximum(i - 1, 0)] != e))
        def _():
            wgu_bf[...] = wgu_ref[...].astype(BF16)
            wd_bf[...] = wd_ref[...].astype(BF16)

        for c in range(nc):
            xs_ref[:, c * LANES:(c + 1) * LANES] = xbuf[slot, pl.ds(c, nrow, stride=nc), :].astype(BF16)
        gu = jnp.dot(xs_ref[...], wgu_bf[...], preferred_element_type=F32) + bgu_ref[...]
        g = jnp.minimum(gu[:, :dff], SWIGLU_LIMIT)
        u = jnp.clip(gu[:, dff:], -SWIGLU_LIMIT, SWIGLU_LIMIT)
        act = (u + 1.0) * (g * (0.5 * jnp.tanh((0.5 * SWIGLU_ALPHA) * g) + 0.5))
        y = (jnp.dot(act.astype(BF16), wd_bf[...], preferred_element_type=F32) + bd_ref[...]) * gate_ref[...]
        row = lax.broadcasted_iota(jnp.int32, (nrow, 1), 0) + b * nrow
        mine = jnp.logical_and(row >= bnd_ref[e], row < bnd_ref[e + 1])

        @pl.when(first)
        def _():
            for c in range(nc):
                ybuf[slot, pl.ds(c, nrow, stride=nc), :] = jnp.where(mine, y[:, c * LANES:(c + 1) * LANES], 0.0)

        @pl.when(jnp.logical_not(first))
        def _():
            for c in range(nc):
                rows = pl.ds(c, nrow, stride=nc)
                ybuf[slot, rows, :] = jnp.where(mine, y[:, c * LANES:(c + 1) * LANES], ybuf[slot, rows, :])

        @pl.when(last)
        def _():
            _for_rows(nrow, lambda r, p: scatter_copy(r, slot).start(priority=p))

        @pl.when(i == ni - 1)
        def _():
            _for_rows(nrow, lambda r, p: scatter_copy(r, slot).wait())

            @pl.when(b >= 1)
            def _():
                _for_rows(nrow, lambda r, p: scatter_copy(r, 1 - slot).wait())


def _moe_items(hn, order, g_sorted, item_blk, item_exp, bounds, n_items, w_gate_up, b_gate_up, w_down, b_down):
    T, nc, _ = hn.shape
    D = nc * LANES
    N = order.shape[0]
    nblk = N // MOE_BLOCK
    n_steps = item_blk.shape[0]
    dff = w_down.shape[1]
    order3 = order.reshape(nblk, 1, MOE_BLOCK)
    grid_spec = pltpu.PrefetchScalarGridSpec(
        num_scalar_prefetch=4,
        grid=(n_steps,),
        in_specs=[
            pl.BlockSpec((1, 1, MOE_BLOCK), lambda i, ib, ie, bd, ni: (ib[i], 0, 0), memory_space=pltpu.SMEM),
            pl.BlockSpec((1, 1, MOE_BLOCK), lambda i, ib, ie, bd, ni: (jnp.minimum(ib[i] + 1, nblk - 1), 0, 0),
                         memory_space=pltpu.SMEM),
            pl.BlockSpec(memory_space=pl.ANY),
            pl.BlockSpec((MOE_BLOCK, 1), lambda i, ib, ie, bd, ni: (ib[i], 0)),
            pl.BlockSpec((None, D, 2 * dff), lambda i, ib, ie, bd, ni: (ie[i], 0, 0)),
            pl.BlockSpec((None, 1, 2 * dff), lambda i, ib, ie, bd, ni: (ie[i], 0, 0)),
            pl.BlockSpec((None, dff, D), lambda i, ib, ie, bd, ni: (ie[i], 0, 0)),
            pl.BlockSpec((None, 1, D), lambda i, ib, ie, bd, ni: (ie[i], 0, 0)),
        ],
        out_specs=pl.BlockSpec(memory_space=pl.ANY),
        scratch_shapes=[pltpu.VMEM((2, MOE_BLOCK * nc, LANES), F32), pltpu.VMEM((2, MOE_BLOCK * nc, LANES), F32),
                        pltpu.VMEM((MOE_BLOCK, D), BF16),
                        pltpu.VMEM((D, 2 * dff), BF16), pltpu.VMEM((dff, D), BF16),
                        pltpu.SemaphoreType.DMA((2,)), pltpu.SemaphoreType.DMA((2,))],
    )
    return pl.pallas_call(
        functools.partial(_moe_kernel, nblk=nblk),
        grid_spec=grid_spec,
        out_shape=jax.ShapeDtypeStruct((N, nc, LANES), F32),
        compiler_params=_cparams(("arbitrary",)),
    )(item_blk, item_exp, bounds, n_items, order3, order3, hn, g_sorted.reshape(N, 1),
      w_gate_up, b_gate_up.reshape(N_EXPERTS, 1, 2 * dff), w_down, b_down.reshape(N_EXPERTS, 1, D))


def _combine_kernel(yk_ref, h_ref, g_ref, o_ref, ys_ref, hs_ref):
    tq = h_ref.shape[0]
    nc = yk_ref.shape[1] // TOP_K
    y = yk_ref[:, 0:nc, :]
    for k in range(1, TOP_K):
        y = y + yk_ref[:, k * nc:(k + 1) * nc, :]
    ys_ref[...] = y.reshape(tq * nc, LANES)
    for c in range(nc):
        hs_ref[:, c * LANES:(c + 1) * LANES] = (h_ref[:, c * LANES:(c + 1) * LANES]
                                                + ys_ref[pl.ds(c, tq, stride=nc), :])
    h = hs_ref[...]
    ms = jnp.mean(h * h, axis=-1, keepdims=True)
    o_ref[...] = h * lax.rsqrt(ms + EPS) * g_ref[...]


def _combine(yk, h, g_final, tq):
    T, D = h.shape
    nc = yk.shape[1]
    return pl.pallas_call(
        _combine_kernel,
        grid=(T // tq,),
        in_specs=[pl.BlockSpec((tq, TOP_K * nc, LANES), lambda i: (i, 0, 0)),
                  pl.BlockSpec((tq, D), lambda i: (i, 0)),
                  pl.BlockSpec((1, D), lambda i: (0, 0))],
        out_specs=pl.BlockSpec((tq, D), lambda i: (i, 0)),
        out_shape=jax.ShapeDtypeStruct((T, D), F32),
        scratch_shapes=[pltpu.VMEM((tq * nc, LANES), F32), pltpu.VMEM((tq, D), F32)],
        compiler_params=_cparams(("parallel",)),
    )(yk.reshape(T, TOP_K * nc, LANES), h, g_final.reshape(1, D))


def _route(top_idx, gates):
    T = top_idx.shape[0]
    N = T * TOP_K
    nblk = N // MOE_BLOCK
    i32 = jnp.int32
    e_sorted, order, g_sorted = lax.sort(
        (top_idx.reshape(N), jnp.arange(N, dtype=i32), gates.reshape(N)), num_keys=1)
    bounds = jnp.sum(e_sorted[None, :] < jnp.arange(N_EXPERTS + 1, dtype=i32)[:, None], axis=1, dtype=i32)
    starts, ends = bounds[:-1], bounds[1:]
    first_blk = starts // MOE_BLOCK
    n_of = jnp.where(ends > starts, (ends - 1) // MOE_BLOCK - first_blk + 1, 0)
    item_end = jnp.cumsum(n_of)
    item_start = item_end - n_of
    n_items = item_end[-1:]
    n_steps = nblk + N_EXPERTS - 1
    i = jnp.arange(n_steps, dtype=i32)
    ie = jnp.minimum(jnp.sum(item_end[None, :] <= i[:, None], axis=1, dtype=i32), N_EXPERTS - 1)
    ib = jnp.clip(first_blk[ie] + i - item_start[ie], 0, nblk - 1).astype(i32)
    return order, g_sorted, ib, ie, bounds, n_items.astype(i32)


def kernel(x, g_mix, w_in, b_in, lru_conv_w, lru_conv_b, lru_wa, lru_ba, lru_wi, lru_bi, lru_lambda, w_lru_br, hy_conv_w, hy_conv_b, hy_w1, hy_b1, hy_w2, hy_b2, hy_w3, hy_b3, hy_w4, hy_sin_freq, hy_bias, w_hy_br, w_out, g_ffn, w_router, b_router, w_gate_up, b_gate_up, w_down, b_down, g_final):
    B, L, D = x.shape
    T = B * L
    x2 = x.reshape(T, D)
    tm = min(1024, T)
    p_all = _inproj(x2, g_mix, w_in, b_in, tm)
    p3 = p_all.reshape(B, L, p_all.shape[1])
    y_lru = _lru(p3, lru_conv_w, lru_conv_b, lru_wa, lru_ba, lru_wi, lru_bi, lru_lambda, min(512, L))
    y_hy = _hyena(p3, D, 2 * D, hy_conv_w, hy_conv_b, hy_w1, hy_b1, hy_w2, hy_b2, hy_w3, hy_b3, hy_w4,
                  hy_sin_freq, hy_bias, min(1024, L), min(256, D), min(256, D))
    h, hn, top_idx, gates = _merge(x2, y_lru.reshape(T, D), y_hy, p_all, w_lru_br, w_hy_br, w_out,
                                   g_ffn, w_router, b_router, min(512, T))
    order, g_sorted, item_blk, item_exp, bounds, n_items = _route(top_idx, gates)
    yk = _moe_items(hn.reshape(T, D // LANES, LANES), order, g_sorted, item_blk, item_exp, bounds, n_items,
                    w_gate_up, b_gate_up, w_down, b_down)
    out = _combine(yk, h, g_final, min(256, T))
    return out.reshape(B, L, D)
```

```python
import functools
import math

import numpy as np
import jax
import jax.numpy as jnp
from jax import lax
from jax.experimental import pallas as pl
from jax.experimental.pallas import tpu as pltpu

F32 = jnp.float32
BF16 = jnp.bfloat16

EPS = 1e-5
C_RGLRU = 8.0
RNN_BLOCK = 64
HY_EMB = 33
HY_BANDS = 16
HY_FAST_DECAY = 0.3
HY_SLOW_DECAY = 1.5
HY_TARGET = 1e-2
N_EXPERTS = 32
TOP_K = 4
SWIGLU_LIMIT = 7.0
SWIGLU_ALPHA = 1.702
MOE_BLOCK = 512

LANES = 128
SUBLANES = 8
FFT_N2 = 128
VMEM_LIMIT = 56 * 1024 * 1024


def _cparams(sem):
    return pltpu.CompilerParams(dimension_semantics=sem, vmem_limit_bytes=VMEM_LIMIT)


def _inproj_kernel(x_ref, g_ref, w_ref, b_ref, o_ref, u_ref):
    j = pl.program_id(1)

    @pl.when(j == 0)
    def _():
        x = x_ref[...]
        ms = jnp.mean(x * x, axis=-1, keepdims=True)
        u_ref[...] = (x * lax.rsqrt(ms + EPS) * g_ref[...]).astype(BF16)

    def proj():
        return jnp.dot(u_ref[...], w_ref[...], preferred_element_type=F32) + b_ref[...]

    @pl.when(j == 1)
    def _():
        o_ref[...] = jax.nn.gelu(proj())

    @pl.when(j >= 5)
    def _():
        o_ref[...] = 0.5 * jnp.tanh(0.5 * proj()) + 0.5

    @pl.when(jnp.logical_and(j != 1, j < 5))
    def _():
        o_ref[...] = proj()


def _inproj(x2, g_mix, w_in, b_in, tm):
    T, D = x2.shape
    n_in = w_in.shape[1]
    return pl.pallas_call(
        _inproj_kernel,
        grid=(T // tm, n_in // D),
        in_specs=[
            pl.BlockSpec((tm, D), lambda i, j: (i, 0)),
            pl.BlockSpec((1, D), lambda i, j: (0, 0)),
            pl.BlockSpec((D, D), lambda i, j: (0, j)),
            pl.BlockSpec((1, D), lambda i, j: (0, j)),
        ],
        out_specs=pl.BlockSpec((tm, D), lambda i, j: (i, j)),
        out_shape=jax.ShapeDtypeStruct((T, n_in), F32),
        scratch_shapes=[pltpu.VMEM((tm, D), BF16)],
        compiler_params=_cparams(("parallel", "arbitrary")),
    )(x2, g_mix.reshape(1, D), w_in.astype(BF16), b_in.reshape(1, n_in))


def _compose_groups(a, b, sub, forward):
    for s in (1, 2, 4):
        shift = s if forward else SUBLANES - s
        a_sh = pltpu.roll(a, shift, 1)
        b_sh = pltpu.roll(b, shift, 1)
        m = (sub >= s) if forward else (sub < SUBLANES - s)
        b = jnp.where(m, a * b_sh + b, b)
        a = jnp.where(m, a * a_sh, a)
    return a, b


def _lru_kernel(x_ref, g_ref, cw_ref, cb_ref, wg_ref, bg_ref, lam_ref, o_ref,
                xp_ref, xr_ref, hb_ref, ge_ref, hin_ref, *, L, R):
    nchunk = L // R
    ng = R // SUBLANES
    NG = L // SUBLANES
    nsg = NG // SUBLANES
    zeros8 = jnp.zeros((SUBLANES, LANES), F32)
    xp_ref[0:SUBLANES, :] = zeros8
    xp_ref[SUBLANES + L:2 * SUBLANES + L, :] = zeros8
    xp_ref[SUBLANES:SUBLANES + L, :] = x_ref[...]

    sub = lax.broadcasted_iota(jnp.int32, (ng, SUBLANES, LANES), 1)
    c8 = [-C_RGLRU * jax.nn.softplus(-lam_ref[d]) for d in range(2)]
    cb = cb_ref[...]
    cw = [cw_ref[k:k + 1, :] for k in range(4)]

    def conv(c, _):
        t0 = pl.multiple_of(c * R, R)
        xr = cb
        for k in range(4):
            xr = xr + cw[k] * xp_ref[pl.ds(t0 + (SUBLANES - 2 + k), R), :]
        xr_ref[pl.ds(t0, R), :] = xr
        return 0

    lax.fori_loop(0, nchunk, conv, 0)

    def level1(c, _):
        t0 = pl.multiple_of(c * R, R)
        rows = pl.ds(t0, R)
        xr = xr_ref[rows, :]
        gz = jnp.dot(xr.astype(BF16), wg_ref[...], preferred_element_type=F32) + bg_ref[...]
        for d in range(2):
            base = 2 * d * LANES
            r = 0.5 * jnp.tanh(0.5 * gz[:, base:base + LANES]) + 0.5
            i = 0.5 * jnp.tanh(0.5 * gz[:, base + LANES:base + 2 * LANES]) + 0.5
            log_a = c8[d] * r
            a = jnp.exp(log_a)
            b = jnp.sqrt((1.0 + a * a) * jnp.tanh(-log_a)) * (i * xr)
            a, b = _compose_groups(a.reshape(ng, SUBLANES, LANES), b.reshape(ng, SUBLANES, LANES), sub, d == 0)
            a_ref, b_ref = (xp_ref, o_ref) if d == 0 else (xr_ref, hb_ref)
            a_ref[rows, :] = a.reshape(R, LANES)
            b_ref[rows, :] = b.reshape(R, LANES)
        gsl = pl.ds(pl.multiple_of(c * ng, ng), ng)
        ge_ref[0, gsl, :] = xp_ref[pl.ds(t0 + SUBLANES - 1, ng, stride=SUBLANES), :]
        ge_ref[1, gsl, :] = o_ref[pl.ds(t0 + SUBLANES - 1, ng, stride=SUBLANES), :]
        ge_ref[2, gsl, :] = xr_ref[pl.ds(t0, ng, stride=SUBLANES), :]
        ge_ref[3, gsl, :] = hb_ref[pl.ds(t0, ng, stride=SUBLANES), :]
        return 0

    lax.fori_loop(0, nchunk, level1, 0)

    sub2 = lax.broadcasted_iota(jnp.int32, (nsg, SUBLANES, LANES), 1)
    for d in range(2):
        a, b = _compose_groups(ge_ref[2 * d].reshape(nsg, SUBLANES, LANES),
                               ge_ref[2 * d + 1].reshape(nsg, SUBLANES, LANES), sub2, d == 0)
        ge_ref[2 * d] = a.reshape(NG, LANES)
        ge_ref[2 * d + 1] = b.reshape(NG, LANES)

    hin_ref[0, 0:SUBLANES, :] = zeros8
    hin_ref[1, SUBLANES + NG:2 * SUBLANES + NG, :] = zeros8

    def walk(j, hc):
        hf, hk = hc
        rf = pl.multiple_of(j * SUBLANES, SUBLANES)
        h8 = ge_ref[1, pl.ds(rf, SUBLANES), :] + ge_ref[0, pl.ds(rf, SUBLANES), :] * hf
        hin_ref[0, pl.ds(SUBLANES + rf, SUBLANES), :] = h8
        hf = jnp.broadcast_to(h8[SUBLANES - 1:SUBLANES, :], (SUBLANES, LANES))
        rb = pl.multiple_of((nsg - 1 - j) * SUBLANES, SUBLANES)
        k8 = ge_ref[3, pl.ds(rb, SUBLANES), :] + ge_ref[2, pl.ds(rb, SUBLANES), :] * hk
        hin_ref[1, pl.ds(SUBLANES + rb, SUBLANES), :] = k8
        hk = jnp.broadcast_to(k8[0:1, :], (SUBLANES, LANES))
        return hf, hk

    lax.fori_loop(0, nsg, walk, (zeros8, zeros8), unroll=8)

    def finish(c, _):
        t0 = pl.multiple_of(c * R, R)
        rows = pl.ds(t0, R)
        g0 = c * ng
        hin_f = hin_ref[0, pl.ds(g0 + SUBLANES - 1, ng), :]
        hin_b = hin_ref[1, pl.ds(g0 + SUBLANES + 1, ng), :]
        shape3 = (ng, SUBLANES, LANES)
        h_f = (o_ref[rows, :].reshape(shape3)
               + xp_ref[rows, :].reshape(shape3) * jnp.broadcast_to(hin_f[:, None, :], shape3))
        h_b = (hb_ref[rows, :].reshape(shape3)
               + xr_ref[rows, :].reshape(shape3) * jnp.broadcast_to(hin_b[:, None, :], shape3))
        o_ref[rows, :] = g_ref[rows, :] * (h_f + h_b).reshape(R, LANES)
        return 0

    lax.fori_loop(0, nchunk, finish, 0)


def _lru(p3, conv_w, conv_b, wa, ba, wi, bi, lam, R):
    B, L, _ = p3.shape
    D = conv_w.shape[1]
    nt = D // LANES
    hpt = LANES // RNN_BLOCK

    def blockdiag(w):
        w = w.reshape(2, nt, hpt, RNN_BLOCK, RNN_BLOCK)
        eye = jnp.eye(hpt, dtype=w.dtype)
        return jnp.einsum('dthij,hg->dthigj', w, eye).reshape(2, nt, LANES, LANES)

    wa_d, wi_d = blockdiag(wa), blockdiag(wi)
    wg = jnp.concatenate([wa_d[0], wi_d[0], wa_d[1], wi_d[1]], axis=-1).astype(BF16)
    tile = lambda v: v.reshape(2, nt, 1, LANES)
    bg = jnp.concatenate([tile(ba)[0], tile(bi)[0], tile(ba)[1], tile(bi)[1]], axis=-1)
    lam4 = lam.reshape(2, nt, 1, LANES)
    NG = L // SUBLANES
    return pl.pallas_call(
        functools.partial(_lru_kernel, L=L, R=R),
        grid=(B, nt),
        in_specs=[
            pl.BlockSpec((None, L, LANES), lambda b, c: (b, 0, c)),
            pl.BlockSpec((None, L, LANES), lambda b, c: (b, 0, nt + c)),
            pl.BlockSpec((4, LANES), lambda b, c: (0, c)),
            pl.BlockSpec((1, LANES), lambda b, c: (0, c)),
            pl.BlockSpec((None, LANES, 4 * LANES), lambda b, c: (c, 0, 0)),
            pl.BlockSpec((None, 1, 4 * LANES), lambda b, c: (c, 0, 0)),
            pl.BlockSpec((2, None, 1, LANES), lambda b, c: (0, c, 0, 0)),
        ],
        out_specs=pl.BlockSpec((None, L, LANES), lambda b, c: (b, 0, c)),
        out_shape=jax.ShapeDtypeStruct((B, L, D), F32),
        scratch_shapes=[
            pltpu.VMEM((L + 2 * SUBLANES, LANES), F32),
            pltpu.VMEM((L, LANES), F32),
            pltpu.VMEM((L, LANES), F32),
            pltpu.VMEM((4, NG, LANES), F32),
            pltpu.VMEM((2, NG + 2 * SUBLANES, LANES), F32),
        ],
        compiler_params=_cparams(("parallel", "parallel")),
    )(p3, p3, conv_w, conv_b.reshape(1, D), wg, bg, lam4)


def _hy_pre_kernel(v_ref, x1_ref, x2_ref, vp_ref, x1p_ref, x2p_ref, vn_ref, x1n_ref, x2n_ref,
                   w_ref, b_ref, z_ref, x1c_ref, *, nt):
    i = pl.program_id(1)
    tl = v_ref.shape[0]
    row = lax.broadcasted_iota(jnp.int32, v_ref.shape, 0)
    first = i == 0
    last = i == nt - 1

    def conv(c, m_ref, p_ref, n_ref):
        x = m_ref[...]
        prev = jnp.where(first, 0.0, p_ref[SUBLANES - 1:SUBLANES, :])
        nxt = jnp.where(last, 0.0, n_ref[0:1, :])
        xm = jnp.where(row == 0, prev, pltpu.roll(x, 1, 0))
        xn = jnp.where(row == tl - 1, nxt, pltpu.roll(x, tl - 1, 0))
        return b_ref[c] + w_ref[c, 0:1, :] * xm + w_ref[c, 1:2, :] * x + w_ref[c, 2:3, :] * xn

    v = conv(0, v_ref, vp_ref, vn_ref)
    x1c = conv(1, x1_ref, x1p_ref, x1n_ref)
    z = conv(2, x2_ref, x2p_ref, x2n_ref) * v
    for s in range(z_ref.shape[0]):
        x1c_ref[s] = x1c[:, s * LANES:(s + 1) * LANES]
        z_ref[s] = z[:, s * LANES:(s + 1) * LANES]


def _hy_pre(p3, conv_w, conv_b, D, col0, tl, ct):
    B, L, n_in = p3.shape
    nt = L // tl
    ncb = D // ct
    g8 = tl // SUBLANES
    p4 = p3.reshape(B, L // SUBLANES, SUBLANES, n_in)
    w3 = conv_w.reshape(3, 3, D).transpose(1, 0, 2)
    b3 = conv_b.reshape(3, 1, D)
    cb0 = col0 // ct

    def main(part):
        return pl.BlockSpec((None, tl, ct), lambda b, i, c: (b, i, cb0 + part * ncb + c))

    def prev(part):
        return pl.BlockSpec((None, None, SUBLANES, ct),
                            lambda b, i, c: (b, jnp.maximum(i * g8 - 1, 0), 0, cb0 + part * ncb + c))

    def nxt(part):
        return pl.BlockSpec((None, None, SUBLANES, ct),
                            lambda b, i, c: (b, jnp.minimum((i + 1) * g8, L // SUBLANES - 1), 0, cb0 + part * ncb + c))

    out = pl.BlockSpec((ct // LANES, None, tl, LANES), lambda b, i, c: (c, b, i, 0))
    return pl.pallas_call(
        functools.partial(_hy_pre_kernel, nt=nt),
        grid=(B, nt, ncb),
        in_specs=[main(0), main(1), main(2), prev(0), prev(1), prev(2), nxt(0), nxt(1), nxt(2),
                  pl.BlockSpec((3, 3, ct), lambda b, i, c: (0, 0, c)),
                  pl.BlockSpec((3, 1, ct), lambda b, i, c: (0, 0, c))],
        out_specs=[out, out],
        out_shape=[jax.ShapeDtypeStruct((D // LANES, B, L, LANES), F32)] * 2,
        compiler_params=_cparams(("parallel", "parallel", "parallel")),
    )(p3, p3, p3, p4, p4, p4, p4, p4, p4, w3, b3)


def _filt_kernel(feat_ref, t_ref, w1_ref, b1_ref, w2_ref, b2_ref, w3_ref, b3_ref, w4f_ref, w4b_ref, fr_ref, dl_ref,
                 k_ref, s_ref, *, nb):
    i = pl.program_id(0)
    hp = lax.Precision.HIGHEST
    fr = fr_ref[...]
    fwd = i < nb
    hh = jnp.sin(fr * (jnp.dot(feat_ref[...], w1_ref[...], precision=hp, preferred_element_type=F32) + b1_ref[...]))
    hh = jnp.sin(fr * (jnp.dot(hh, w2_ref[...], precision=hp, preferred_element_type=F32) + b2_ref[...]))
    hh = jnp.sin(fr * (jnp.dot(hh, w3_ref[...], precision=hp, preferred_element_type=F32) + b3_ref[...]))
    w4 = jnp.where(fwd, w4f_ref[...], w4b_ref[...])
    decay = jnp.exp(-t_ref[...] * jnp.abs(dl_ref[...]))
    hb16 = hh.astype(BF16)
    k = jnp.dot(hb16, w4.astype(BF16), preferred_element_type=F32) * decay
    row = lax.broadcasted_iota(jnp.int32, k.shape, 0)
    kb0 = (jnp.dot(hb16[0:BF16_ROWS], w4b_ref[...].astype(BF16), preferred_element_type=F32)
           * decay[0:BF16_ROWS])[0:1]
    k = jnp.where(jnp.logical_and(row == 0, i == 0), k + kb0, k)
    k = jnp.where(jnp.logical_and(row == 0, i == nb), 0.0, k)
    for s in range(k_ref.shape[0]):
        k_ref[s] = k[:, s * LANES:(s + 1) * LANES]
    part = jnp.sum(jnp.abs(k), axis=0, keepdims=True)

    @pl.when(i == 0)
    def _():
        s_ref[...] = part

    @pl.when(i > 0)
    def _():
        s_ref[...] += part


def _hyena_filter(L, D, w1, b1, w2, b2, w3, b3, w4, freq, tl):
    t = jnp.linspace(0.0, 1.0, L, dtype=F32)[:, None]
    w = 2.0 * math.pi * jnp.arange(L, dtype=F32)[:, None] / L
    bands = jnp.linspace(1e-4, HY_BANDS - 1, HY_BANDS, dtype=F32)[None, :]
    feats = jnp.concatenate([t, jnp.cos(w * bands), -jnp.sin(w * bands)], axis=-1)
    deltas = jnp.linspace(math.log(HY_TARGET) / HY_SLOW_DECAY, math.log(HY_TARGET) / HY_FAST_DECAY, D, dtype=F32)
    rev = lambda a: jnp.roll(jnp.flip(a, 0), 1, 0)
    feats2 = jnp.concatenate([feats, rev(feats)], axis=0)
    t2 = jnp.concatenate([t, rev(t)], axis=0)
    ff = w1.shape[1]
    nb = L // tl
    full = lambda shape: pl.BlockSpec(shape, lambda i: (0,) * len(shape))
    return pl.pallas_call(
        functools.partial(_filt_kernel, nb=nb),
        grid=(2 * nb,),
        in_specs=[pl.BlockSpec((tl, HY_EMB), lambda i: (i, 0)), pl.BlockSpec((tl, 1), lambda i: (i, 0)),
                  full((HY_EMB, ff)), full((1, ff)), full((ff, ff)), full((1, ff)), full((ff, ff)), full((1, ff)),
                  full((ff, D)), full((ff, D)), full((1, ff)), full((1, D))],
        out_specs=[pl.BlockSpec((D // LANES, tl, LANES), lambda i: (0, i, 0)), pl.BlockSpec((1, D), lambda i: (0, 0))],
        out_shape=[jax.ShapeDtypeStruct((D // LANES, 2 * L, LANES), F32), jax.ShapeDtypeStruct((1, D), F32)],
        compiler_params=_cparams(("arbitrary",)),
    )(feats2, t2, w1, b1.reshape(1, ff), w2, b2.reshape(1, ff), w3, b3.reshape(1, ff), w4[:, :D], w4[:, D:],
      freq.reshape(1, ff), deltas.reshape(1, D))


def _angles(rows, cols, n):
    r = (rows.astype(np.int64) * cols.astype(np.int64)) % n
    th = r.astype(np.float64) * (2.0 * math.pi / n)
    return jnp.asarray(np.cos(th), F32), jnp.asarray(np.sin(th), F32)


def _dft_mats(N1, N2):
    N = N1 * N2
    h = N1 // 2
    ar = lambda n: np.arange(n)
    def add(c1, s1, c2, s2):
        return c1 * c2 - s1 * s2, s1 * c2 + c1 * s2

    cp, sp = _angles(ar(N1)[:, None], ar(N1)[None, :], N1)
    cq, sq = _angles(ar(N2)[:, None], ar(N1)[None, :], N)
    cf, sf = add(cp[None], sp[None], cq[:, :, None], sq[:, :, None])
    c, s = cf[:, :, :h], sf[:, :, :h]
    a_data = jnp.concatenate([jnp.concatenate([c, s], -1), jnp.concatenate([-s, c], -1)], 1)
    a_filt = jnp.concatenate([cf, -sf], 1)
    c, s = _angles(ar(N2)[:, None], ar(N2)[None, :], N2)
    m2 = jnp.concatenate([jnp.concatenate([c, s], -1), jnp.concatenate([-s, c], -1)], 0)
    cr, sr = _angles(ar(N1)[:, None], ar(N2)[None, :], N)
    cs, ss = _angles(ar(N2)[:, None], ar(N2)[None, :], N2)
    c, s = add(cr[:, :, None], sr[:, :, None], cs[None], ss[None])
    hm =jnp.concatenate([jnp.concatenate([c, -s], -1), jnp.concatenate([s, c], -1)], 1)
    c, s = _angles(ar(h)[:, None], ar(N1)[None, :], N1)
    i2 = jnp.concatenate([jnp.concatenate([c, -s], -1), jnp.concatenate([s, c], -1)], 0) / N
    return (a_data.astype(BF16), a_filt.astype(BF16), m2.astype(BF16), hm.astype(BF16), i2.astype(BF16))


BF16_ROWS = 16


def _lane_cat(parts):
    return parts[0] if len(parts) == 1 else jnp.concatenate(parts, axis=1)


def _flatten_in(v, dst_ref, region, nrows):
    for hf in range(BF16_ROWS // SUBLANES):
        dst_ref[pl.ds((2 * region + hf) * nrows, nrows), :] = (
            v[:, hf * SUBLANES:(hf + 1) * SUBLANES, :].reshape(nrows, LANES))


def _unflatten_out(src_ref, region, n, nrows):
    return jnp.concatenate(
        [src_ref[pl.ds((2 * region + hf) * nrows, nrows), :].reshape(n, SUBLANES, LANES)
         for hf in range(BF16_ROWS // SUBLANES)], axis=1)


def _tstep(region, j, n, nrows):
    hf, jj = divmod(j, SUBLANES)
    return pl.ds((2 * region + hf) * nrows + jj, n, stride=SUBLANES)


def _fft1_kernel(*refs, scaled):
    a_ref = refs[0]
    inv_s = refs[1][...] if scaled else None
    x_ref, o_ref, xs_ref, os_ref = refs[-4:]
    ns, nin, rows = x_ref.shape[0], x_ref.shape[1], x_ref.shape[2]
    n1 = o_ref.shape[2]
    xr, orr = rows * SUBLANES, n1 * SUBLANES
    for s in range(ns):
        for q in range(nin):
            _flatten_in(x_ref[s, q], xs_ref, s * nin + q, xr)
    for j in range(BF16_ROWS):
        parts = [_lane_cat([xs_ref[_tstep(s * nin + q, j, rows, xr), :] for s in range(ns)]) for q in range(nin)]
        x = parts[0] if nin == 1 else jnp.concatenate(parts, axis=0)
        if scaled:
            x = x * inv_s
        y = jnp.dot(a_ref[j], x.astype(BF16), preferred_element_type=F32)
        for s in range(ns):
            for c in range(2):
                os_ref[_tstep(s * 2 + c, j, n1, orr), :] = y[c * n1:(c + 1) * n1, s * LANES:(s + 1) * LANES]
    for s in range(ns):
        for c in range(2):
            o_ref[s, c] = _unflatten_out(os_ref, s * 2 + c, n1, orr).astype(BF16)


def _fft1_data(z6, a_data, dt):
    S, _, P, h, N2, _ = z6.shape
    N1 = 2 * h
    ns = dt // LANES
    jb = BF16_ROWS
    return pl.pallas_call(
        functools.partial(_fft1_kernel, scaled=False),
        grid=(N2 // jb, P, S // ns),
        in_specs=[pl.BlockSpec((jb, 2 * N1, N1), lambda j, p, c: (j, 0, 0)),
                  pl.BlockSpec((ns, 2, None, h, jb, LANES), lambda j, p, c: (c, 0, p, 0, j, 0))],
        out_specs=pl.BlockSpec((None, ns, 2, N1, jb, LANES), lambda j, p, c: (p, c, 0, 0, j, 0)),
        out_shape=jax.ShapeDtypeStruct((P, S, 2, N1, N2, LANES), BF16),
        scratch_shapes=[pltpu.VMEM((ns * 2 * h * jb, LANES), F32),
                        pltpu.VMEM((ns * 2 * N1 * jb, LANES), F32)],
        compiler_params=_cparams(("parallel", "parallel", "parallel")),
    )(a_data, z6)


def _fft1_filt(k5, inv_s, a_filt, dt):
    S, _, N1, N2, _ = k5.shape
    ns = dt // LANES
    jb = BF16_ROWS
    return pl.pallas_call(
        functools.partial(_fft1_kernel, scaled=True),
        grid=(N2 // jb, S // ns),
        in_specs=[pl.BlockSpec((jb, 2 * N1, N1), lambda j, c: (j, 0, 0)),
                  pl.BlockSpec((1, dt), lambda j, c: (0, c)),
                  pl.BlockSpec((ns, 1, N1, jb, LANES), lambda j, c: (c, 0, 0, j, 0))],
        out_specs=pl.BlockSpec((ns, 2, N1, jb, LANES), lambda j, c: (c, 0, 0, j, 0)),
        out_shape=jax.ShapeDtypeStruct((S, 2, N1, N2, LANES), BF16),
        scratch_shapes=[pltpu.VMEM((ns * N1 * jb, LANES), F32),
                        pltpu.VMEM((ns * 2 * N1 * jb, LANES), F32)],
        compiler_params=_cparams(("parallel", "parallel")),
    )(a_filt, inv_s, k5)


FFT_MID_K1 = 2


def _fft_mid_kernel(m2_ref, h_ref, y_ref, yk_ref, o_ref, kf_ref):
    S, kb, n2 = y_ref.shape[0], y_ref.shape[2], y_ref.shape[3]

    def stacked(ref, kk):
        return jnp.concatenate([_lane_cat([ref[s, c, kk] for s in range(S)]) for c in range(2)], axis=0)

    @pl.when(pl.program_id(1) == 0)
    def _():
        for kk in range(kb):
            kf_ref[kk] = jnp.dot(m2_ref[...], stacked(yk_ref, kk), preferred_element_type=F32)

    for kk in range(kb):
        x = jnp.dot(m2_ref[...], stacked(y_ref, kk), preferred_element_type=F32)
        xr, xi = x[:n2], x[n2:]
        kr, ki = kf_ref[kk, 0:n2, :], kf_ref[kk, n2:, :]
        pr = xr * kr - xi * ki
        pi = xr * ki + xi * kr
        p = jnp.concatenate([pr, pi], axis=0).astype(BF16)
        u = jnp.dot(h_ref[kk], p, preferred_element_type=F32).astype(BF16)
        for s in range(S):
            for c in range(2):
                o_ref[s, c, kk] = u[c * n2:(c + 1) * n2, s * LANES:(s + 1) * LANES]


def _fft_mid(y, yk, m2, hm):
    P, S, _, N1, N2, _ = y.shape
    kb = FFT_MID_K1
    return pl.pallas_call(
        _fft_mid_kernel,
        grid=(N1 // kb, P),
        in_specs=[pl.BlockSpec((2 * N2, 2 * N2), lambda k, p: (0, 0)),
                  pl.BlockSpec((kb, 2 * N2, 2 * N2), lambda k, p: (k, 0, 0)),
                  pl.BlockSpec((None, S, 2, kb, N2, LANES), lambda k, p: (p, 0, 0, k, 0, 0)),
                  pl.BlockSpec((S, 2, kb, N2, LANES), lambda k, p: (0, 0, k, 0, 0))],
        out_specs=pl.BlockSpec((None, S, 2, kb, N2, LANES), lambda k, p: (p, 0, 0, k, 0, 0)),
        out_shape=jax.ShapeDtypeStruct((P, S, 2, N1, N2, LANES), BF16),
        scratch_shapes=[pltpu.VMEM((kb, 2 * N2, S * LANES), F32)],
        compiler_params=_cparams(("parallel", "arbitrary")),
    )(m2, hm, y, yk)


def _fft_inv_kernel(i2_ref, u_ref, z_ref, x1_ref, bias_ref, o_ref, us_ref, zs_ref, xs_ref, os_ref):
    ns, n1 = u_ref.shape[0], u_ref.shape[2]
    h = o_ref.shape[2]
    ur, hr = n1 * SUBLANES, h * SUBLANES
    bias = bias_ref[...]
    for s in range(ns):
        for c in range(2):
            _flatten_in(u_ref[s, c].astype(F32), us_ref, s * 2 + c, ur)
            _flatten_in(z_ref[s, c], zs_ref, s * 2 + c, hr)
            _flatten_in(x1_ref[s, c], xs_ref, s * 2 + c, hr)
    for j in range(BF16_ROWS):
        up = jnp.concatenate([_lane_cat([us_ref[_tstep(s * 2 + c, j, n1, ur), :] for s in range(ns)])
                              for c in range(2)], axis=0)
        y = jnp.dot(i2_ref[...], up.astype(BF16), preferred_element_type=F32)
        for s in range(ns):
            lanes = slice(s * LANES, (s + 1) * LANES)
            for q in range(2):
                t = _tstep(s * 2 + q, j, h, hr)
                os_ref[t, :] = xs_ref[t, :] * (y[q * h:(q + 1) * h, lanes] + bias[:, lanes] * zs_ref[t, :])
    for s in range(ns):
        for q in range(2):
            o_ref[s, q] = _unflatten_out(os_ref, s * 2 + q, h, hr)


def _fft_inv(u, z6, x16, bias, i2, dt):
    P, S, _, N1, N2, _ = u.shape
    h = N1 // 2
    ns = dt // LANES
    jb = BF16_ROWS
    pair = pl.BlockSpec((ns, 2, None, h, jb, LANES), lambda j, p, c: (c, 0, p, 0, j, 0))
    return pl.pallas_call(
        _fft_inv_kernel,
        grid=(N2 // jb, P, S // ns),
        in_specs=[pl.BlockSpec((N1, 2 * N1), lambda j, p, c: (0, 0)),
                  pl.BlockSpec((None, ns, 2, N1, jb, LANES), lambda j, p, c: (p, c, 0, 0, j, 0)),
                  pair, pair,
                  pl.BlockSpec((1, dt), lambda j, p, c: (0, c))],
        out_specs=pair,
        out_shape=jax.ShapeDtypeStruct((S, 2, P, h, N2, LANES), F32),
        scratch_shapes=[pltpu.VMEM((ns * 2 * N1 * jb, LANES), F32)]
        + [pltpu.VMEM((ns * 2 * h * jb, LANES), F32)] * 3,
        compiler_params=_cparams(("parallel", "parallel", "parallel")),
    )(i2, u, z6, x16, bias)


def _hyena(p3, D, col0, conv_w, conv_b, w1, b1, w2, b2, w3, b3, w4, freq, bias, tl, ct, dt):
    B, L, _ = p3.shape
    N2 = FFT_N2
    N1 = 2 * L // N2
    h = N1 // 2
    S = D // LANES
    z, x1c = _hy_pre(p3, conv_w, conv_b, D, col0, tl, ct)
    k_circ, s = _hyena_filter(L, D, w1, b1, w2, b2, w3, b3, w4, freq, min(tl, 512))
    a_data, a_filt, m2, hm, i2 = _dft_mats(N1, N2)
    yk = _fft1_filt(k_circ.reshape(S, 1, N1, N2, LANES), 1.0 / s, a_filt, dt)
    z6 = z.reshape(S, 2, B // 2, h, N2, LANES)
    y = _fft1_data(z6, a_data, dt)
    u = _fft_mid(y, yk, m2, hm)
    out = _fft_inv(u, z6, x1c.reshape(S, 2, B // 2, h, N2, LANES), bias.reshape(1, D), i2, dt)
    return out.reshape(S, B * L, LANES)


def _merge_kernel(x_ref, yl_ref, yh_ref, gl_ref, gh_ref, wl_ref, wh_ref, wo_ref, gf_ref, wr_ref, br_ref,
                  h_ref, hn_ref, idx_ref, gate_ref):
    a = jnp.dot(yl_ref[...].astype(BF16), wl_ref[...], preferred_element_type=F32)
    yh = _lane_cat([yh_ref[s] for s in range(yh_ref.shape[0])])
    b = jnp.dot(yh.astype(BF16), wh_ref[...], preferred_element_type=F32)
    mix = gl_ref[...] * a + gh_ref[...] * b
    h = x_ref[...] + jnp.dot(mix.astype(BF16), wo_ref[...], preferred_element_type=F32)
    h_ref[...] = h
    ms = jnp.mean(h * h, axis=-1, keepdims=True)
    hn = h * lax.rsqrt(ms + EPS) * gf_ref[...]
    nc = hn.shape[1] // LANES
    for c in range(nc):
        hn_ref[pl.ds(c, hn.shape[0], stride=nc), :] = hn[:, c * LANES:(c + 1) * LANES]
    ne = br_ref.shape[1]
    h_hi = hn.astype(BF16)
    h_lo = (hn - h_hi.astype(F32)).astype(BF16)
    a2 = jnp.dot(h_hi, wr_ref[...], preferred_element_type=F32)
    b2 = jnp.dot(h_lo, wr_ref[:, :LANES], preferred_element_type=F32)
    logits = (a2[:, :LANES] + a2[:, LANES:] + b2)[:, :ne] + br_ref[...]
    lane = lax.broadcasted_iota(jnp.int32, logits.shape, 1)
    vals = []
    for k in range(TOP_K):
        m = jnp.max(logits, axis=-1, keepdims=True)
        ix = jnp.min(jnp.where(logits == m, lane, ne), axis=-1, keepdims=True)
        vals.append(m)
        idx_ref[:, k:k + 1] = ix
        logits = jnp.where(lane == ix, -jnp.inf, logits)
    ex = [jnp.exp(v - vals[0]) for v in vals]
    tot = ex[0] + ex[1] + ex[2] + ex[3]
    for k in range(TOP_K):
        gate_ref[:, k:k + 1] = ex[k] / tot


def _merge(x2, y_lru, y_hy, p_all, w_lru_br, w_hy_br, w_out, g_ffn, w_router, b_router, tm):
    T, D = x2.shape
    ne = w_router.shape[1]
    gcol = p_all.shape[1] // D - 2
    row = lambda i: (i, 0)
    full = lambda i: (0, 0)
    w_hi = w_router.astype(BF16)
    w_lo = (w_router - w_hi.astype(F32)).astype(BF16)
    lane_pad = lambda w: jnp.pad(w, ((0, 0), (0, LANES - ne)))
    wr2 = jnp.concatenate([lane_pad(w_hi), lane_pad(w_lo)], axis=1)
    return pl.pallas_call(
        _merge_kernel,
        grid=(T // tm,),
        in_specs=[pl.BlockSpec((tm, D), row), pl.BlockSpec((tm, D), row),
                  pl.BlockSpec((D // LANES, tm, LANES), lambda i: (0, i, 0)),
                  pl.BlockSpec((tm, D), lambda i: (i, gcol)), pl.BlockSpec((tm, D), lambda i: (i, gcol + 1)),
                  pl.BlockSpec((D, D), full), pl.BlockSpec((D, D), full), pl.BlockSpec((D, D), full),
                  pl.BlockSpec((1, D), full), pl.BlockSpec((D, 2 * LANES), full), pl.BlockSpec((1, ne), full)],
        out_specs=[pl.BlockSpec((tm, D), row), pl.BlockSpec((tm * (D // LANES), LANES), row),
                   pl.BlockSpec((tm, TOP_K), row), pl.BlockSpec((tm, TOP_K), row)],
        out_shape=[jax.ShapeDtypeStruct((T, D), F32), jax.ShapeDtypeStruct((T * (D // LANES), LANES), F32),
                   jax.ShapeDtypeStruct((T, TOP_K), jnp.int32), jax.ShapeDtypeStruct((T, TOP_K), F32)],
        compiler_params=_cparams(("parallel",)),
    )(x2, y_lru, y_hy, p_all, p_all, w_lru_br.astype(BF16), w_hy_br.astype(BF16), w_out.astype(BF16),
      g_ffn.reshape(1, D), wr2, b_router.reshape(1, ne))


ROW_DMA_UNROLL = 8


def _for_rows(n, fn):
    def body(g, c):
        for k in range(ROW_DMA_UNROLL):
            fn(g * ROW_DMA_UNROLL + k, k % 2)
        return c

    lax.fori_loop(0, n // ROW_DMA_UNROLL, body, 0)


def _moe_kernel(ib_ref, ie_ref, bnd_ref, ni_ref, ord_ref, nord_ref, hn_hbm, gate_ref, wgu_ref, bgu_ref,
                wd_ref, bd_ref, yk_hbm, xbuf, ybuf, xs_ref, wgu_bf, wd_bf, gsem, ssem, *, nblk):
    i = pl.program_id(0)
    ni = ni_ref[0]
    nrow = xs_ref.shape[0]
    nc = xbuf.shape[1] // nrow
    dff = wd_ref.shape[0]

    def gather_copy(idx_ref, r, slot):
        tok = lax.shift_right_logical(idx_ref[0, 0, r], 2)
        return pltpu.make_async_copy(hn_hbm.at[tok], xbuf.at[slot, pl.ds(r * nc, nc), :], gsem.at[slot])

    def scatter_copy(r, slot):
        return pltpu.make_async_copy(ybuf.at[slot, pl.ds(r * nc, nc), :], yk_hbm.at[ord_ref[0, 0, r]],
                                     ssem.at[slot])

    @pl.when(i < ni)
    def _():
        b = ib_ref[i]
        e = ie_ref[i]
        slot = b % 2
        first = jnp.logical_or(i == 0, ib_ref[jnp.maximum(i - 1, 0)] != b)
        last = jnp.logical_or(i == ni - 1, ib_ref[jnp.minimum(i + 1, pl.num_programs(0) - 1)] != b)

        @pl.when(i == 0)
        def _():
            _for_rows(nrow, lambda r, p: gather_copy(ord_ref, r, 0).start(priority=p))

        @pl.when(first)
        def _():
            @pl.when(b + 1 < nblk)
            def _():
                _for_rows(nrow, lambda r, p: gather_copy(nord_ref, r, 1 - slot).start(priority=p))

            _for_rows(nrow, lambda r, p: gather_copy(ord_ref, r, slot).wait())

            @pl.when(b >= 2)
            def _():
                _for_rows(nrow, lambda r, p: scatter_copy(r, slot).wait())

        @pl.when(jnp.logical_or(i == 0, ie_ref[jnp.maximum(i - 1, 0)] != e))
        def _():
            wgu_bf[...] = wgu_ref[...].astype(BF16)
            wd_bf[...] = wd_ref[...].astype(BF16)

        for c in range(nc):
            xs_ref[:, c * LANES:(c + 1) * LANES] = xbuf[slot, pl.ds(c, nrow, stride=nc), :].astype(BF16)
        gu = jnp.dot(xs_ref[...], wgu_bf[...], preferred_element_type=F32) + bgu_ref[...]
        g = jnp.minimum(gu[:, :dff], SWIGLU_LIMIT)
        u = jnp.clip(gu[:, dff:], -SWIGLU_LIMIT, SWIGLU_LIMIT)
        act = (u + 1.0) * (g * (0.5 * jnp.tanh((0.5 * SWIGLU_ALPHA) * g) + 0.5))
        y = (jnp.dot(act.astype(BF16), wd_bf[...], preferred_element_type=F32) + bd_ref[...]) * gate_ref[...]
        row = lax.broadcasted_iota(jnp.int32, (nrow, 1), 0) + b * nrow
        mine = jnp.logical_and(row >= bnd_ref[e], row < bnd_ref[e + 1])

        @pl.when(first)
        def _():
            for c in range(nc):
                ybuf[slot, pl.ds(c, nrow, stride=nc), :] = jnp.where(mine, y[:, c * LANES:(c + 1) * LANES], 0.0)

        @pl.when(jnp.logical_not(first))
        def _():
            for c in range(nc):
                rows = pl.ds(c, nrow, stride=nc)
                ybuf[slot, rows, :] = jnp.where(mine, y[:, c * LANES:(c + 1) * LANES], ybuf[slot, rows, :])

        @pl.when(last)
        def _():
            _for_rows(nrow, lambda r, p: scatter_copy(r, slot).start(priority=p))

        @pl.when(i == ni - 1)
        def _():
            _for_rows(nrow, lambda r, p: scatter_copy(r, slot).wait())

            @pl.when(b >= 1)
            def _():
                _for_rows(nrow, lambda r, p: scatter_copy(r, 1 - slot).wait())


def _moe_items(hn, order, g_sorted, item_blk, item_exp, bounds, n_items, w_gate_up, b_gate_up, w_down, b_down):
    T, nc, _ = hn.shape
    D = nc * LANES
    N = order.shape[0]
    nblk = N // MOE_BLOCK
    n_steps = item_blk.shape[0]
    dff = w_down.shape[1]
    order3 = order.reshape(nblk, 1, MOE_BLOCK)
    grid_spec = pltpu.PrefetchScalarGridSpec(
        num_scalar_prefetch=4,
        grid=(n_steps,),
        in_specs=[
            pl.BlockSpec((1, 1, MOE_BLOCK), lambda i, ib, ie, bd, ni: (ib[i], 0, 0), memory_space=pltpu.SMEM),
            pl.BlockSpec((1, 1, MOE_BLOCK), lambda i, ib, ie, bd, ni: (jnp.minimum(ib[i] + 1, nblk - 1), 0, 0),
                         memory_space=pltpu.SMEM),
            pl.BlockSpec(memory_space=pl.ANY),
            pl.BlockSpec((MOE_BLOCK, 1), lambda i, ib, ie, bd, ni: (ib[i], 0)),
            pl.BlockSpec((None, D, 2 * dff), lambda i, ib, ie, bd, ni: (ie[i], 0, 0)),
            pl.BlockSpec((None, 1, 2 * dff), lambda i, ib, ie, bd, ni: (ie[i], 0, 0)),
            pl.BlockSpec((None, dff, D), lambda i, ib, ie, bd, ni: (ie[i], 0, 0)),
            pl.BlockSpec((None, 1, D), lambda i, ib, ie, bd, ni: (ie[i], 0, 0)),
        ],
        out_specs=pl.BlockSpec(memory_space=pl.ANY),
        scratch_shapes=[pltpu.VMEM((2, MOE_BLOCK * nc, LANES), F32), pltpu.VMEM((2, MOE_BLOCK * nc, LANES), F32),
                        pltpu.VMEM((MOE_BLOCK, D), BF16),
                        pltpu.VMEM((D, 2 * dff), BF16), pltpu.VMEM((dff, D), BF16),
                        pltpu.SemaphoreType.DMA((2,)), pltpu.SemaphoreType.DMA((2,))],
    )
    return pl.pallas_call(
        functools.partial(_moe_kernel, nblk=nblk),
        grid_spec=grid_spec,
        out_shape=jax.ShapeDtypeStruct((N, nc, LANES), F32),
        compiler_params=_cparams(("arbitrary",)),
    )(item_blk, item_exp, bounds, n_items, order3, order3, hn, g_sorted.reshape(N, 1),
      w_gate_up, b_gate_up.reshape(N_EXPERTS, 1, 2 * dff), w_down, b_down.reshape(N_EXPERTS, 1, D))


def _combine_kernel(yk_ref, h_ref, g_ref, o_ref, ys_ref, hs_ref):
    tq = h_ref.shape[0]
    nc = yk_ref.shape[1] // TOP_K
    y = yk_ref[:, 0:nc, :]
    for k in range(1, TOP_K):
        y = y + yk_ref[:, k * nc:(k + 1) * nc, :]
    ys_ref[...] = y.reshape(tq * nc, LANES)
    for c in range(nc):
        hs_ref[:, c * LANES:(c + 1) * LANES] = (h_ref[:, c * LANES:(c + 1) * LANES]
                                                + ys_ref[pl.ds(c, tq, stride=nc), :])
    h = hs_ref[...]
    ms = jnp.mean(h * h, axis=-1, keepdims=True)
    o_ref[...] = h * lax.rsqrt(ms + EPS) * g_ref[...]


def _combine(yk, h, g_final, tq):
    T, D = h.shape
    nc = yk.shape[1]
    return pl.pallas_call(
        _combine_kernel,
        grid=(T // tq,),
        in_specs=[pl.BlockSpec((tq, TOP_K * nc, LANES), lambda i: (i, 0, 0)),
                  pl.BlockSpec((tq, D), lambda i: (i, 0)),
                  pl.BlockSpec((1, D), lambda i: (0, 0))],
        out_specs=pl.BlockSpec((tq, D), lambda i: (i, 0)),
        out_shape=jax.ShapeDtypeStruct((T, D), F32),
        scratch_shapes=[pltpu.VMEM((tq * nc, LANES), F32), pltpu.VMEM((tq, D), F32)],
        compiler_params=_cparams(("parallel",)),
    )(yk.reshape(T, TOP_K * nc, LANES), h, g_final.reshape(1, D))


def _route(top_idx, gates):
    T = top_idx.shape[0]
    N = T * TOP_K
    nblk = N // MOE_BLOCK
    i32 = jnp.int32
    e_sorted, order, g_sorted = lax.sort(
        (top_idx.reshape(N), jnp.arange(N, dtype=i32), gates.reshape(N)), num_keys=1)
    bounds = jnp.sum(e_sorted[None, :] < jnp.arange(N_EXPERTS + 1, dtype=i32)[:, None], axis=1, dtype=i32)
    starts, ends = bounds[:-1], bounds[1:]
    first_blk = starts // MOE_BLOCK
    n_of = jnp.where(ends > starts, (ends - 1) // MOE_BLOCK - first_blk + 1, 0)
    item_end = jnp.cumsum(n_of)
    item_start = item_end - n_of
    n_items = item_end[-1:]
    n_steps = nblk + N_EXPERTS - 1
    i = jnp.arange(n_steps, dtype=i32)
    ie = jnp.minimum(jnp.sum(item_end[None, :] <= i[:, None], axis=1, dtype=i32), N_EXPERTS - 1)
    ib = jnp.clip(first_blk[ie] + i - item_start[ie], 0, nblk - 1).astype(i32)
    return order, g_sorted, ib, ie, bounds, n_items.astype(i32)


def kernel(x, g_mix, w_in, b_in, lru_conv_w, lru_conv_b, lru_wa, lru_ba, lru_wi, lru_bi, lru_lambda, w_lru_br, hy_conv_w, hy_conv_b, hy_w1, hy_b1, hy_w2, hy_b2, hy_w3, hy_b3, hy_w4, hy_sin_freq, hy_bias, w_hy_br, w_out, g_ffn, w_router, b_router, w_gate_up, b_gate_up, w_down, b_down, g_final):
    B, L, D = x.shape
    T = B * L
    x2 = x.reshape(T, D)
    tm = min(2048, T)
    p_all = _inproj(x2, g_mix, w_in, b_in, tm)
    p3 = p_all.reshape(B, L, p_all.shape[1])
    y_lru = _lru(p3, lru_conv_w, lru_conv_b, lru_wa, lru_ba, lru_wi, lru_bi, lru_lambda, min(512, L))
    y_hy = _hyena(p3, D, 2 * D, hy_conv_w, hy_conv_b, hy_w1, hy_b1, hy_w2, hy_b2, hy_w3, hy_b3, hy_w4,
                  hy_sin_freq, hy_bias, min(1024, L), min(256, D), min(256, D))
    h, hn, top_idx, gates = _merge(x2, y_lru.reshape(T, D), y_hy, p_all, w_lru_br, w_hy_br, w_out,
                                   g_ffn, w_router, b_router, min(512, T))
    order, g_sorted, item_blk, item_exp, bounds, n_items = _route(top_idx, gates)
    yk = _moe_items(hn.reshape(T, D // LANES, LANES), order, g_sorted, item_blk, item_exp, bounds, n_items,
                    w_gate_up, b_gate_up, w_down, b_down)
    out = _combine(yk, h, g_final, min(256, T))
    return out.reshape(B, L, D)
```

```python
import functools
import math

import numpy as np
import jax
import jax.numpy as jnp
from jax import lax
from jax.experimental import pallas as pl
from jax.experimental.pallas import tpu as pltpu

F32 = jnp.float32
BF16 = jnp.bfloat16

EPS = 1e-5
C_RGLRU = 8.0
RNN_BLOCK = 64
HY_EMB = 33
HY_BANDS = 16
HY_FAST_DECAY = 0.3
HY_SLOW_DECAY = 1.5
HY_TARGET = 1e-2
N_EXPERTS = 32
TOP_K = 4
SWIGLU_LIMIT = 7.0
SWIGLU_ALPHA = 1.702
MOE_BLOCK = 512

LANES = 128
SUBLANES = 8
FFT_N2 = 128
VMEM_LIMIT = 56 * 1024 * 1024


def _cparams(sem):
    return pltpu.CompilerParams(dimension_semantics=sem, vmem_limit_bytes=VMEM_LIMIT)


def _inproj_kernel(x_ref, g_ref, w_ref, b_ref, o_ref, u_ref):
    j = pl.program_id(1)

    @pl.when(j == 0)
    def _():
        x = x_ref[...]
        ms = jnp.mean(x * x, axis=-1, keepdims=True)
        u_ref[...] = (x * lax.rsqrt(ms + EPS) * g_ref[...]).astype(BF16)

    def proj():
        return jnp.dot(u_ref[...], w_ref[...], preferred_element_type=F32) + b_ref[...]

    @pl.when(j == 1)
    def _():
        o_ref[...] = jax.nn.gelu(proj())

    @pl.when(j >= 5)
    def _():
        o_ref[...] = 0.5 * jnp.tanh(0.5 * proj()) + 0.5

    @pl.when(jnp.logical_and(j != 1, j < 5))
    def _():
        o_ref[...] = proj()


def _inproj(x2, g_mix, w_in, b_in, tm):
    T, D = x2.shape
    n_in = w_in.shape[1]
    return pl.pallas_call(
        _inproj_kernel,
        grid=(T // tm, n_in // D),
        in_specs=[
            pl.BlockSpec((tm, D), lambda i, j: (i, 0)),
            pl.BlockSpec((1, D), lambda i, j: (0, 0)),
            pl.BlockSpec((D, D), lambda i, j: (0, j)),
            pl.BlockSpec((1, D), lambda i, j: (0, j)),
        ],
        out_specs=pl.BlockSpec((tm, D), lambda i, j: (i, j)),
        out_shape=jax.ShapeDtypeStruct((T, n_in), F32),
        scratch_shapes=[pltpu.VMEM((tm, D), BF16)],
        compiler_params=_cparams(("parallel", "arbitrary")),
    )(x2, g_mix.reshape(1, D), w_in.astype(BF16), b_in.reshape(1, n_in))


def _compose_groups(a, b, sub, forward):
    for s in (1, 2, 4):
        shift = s if forward else SUBLANES - s
        a_sh = pltpu.roll(a, shift, 1)
        b_sh = pltpu.roll(b, shift, 1)
        m = (sub >= s) if forward else (sub < SUBLANES - s)
        b = jnp.where(m, a * b_sh + b, b)
        a = jnp.where(m, a * a_sh, a)
    return a, b


def _lru_kernel(x_ref, g_ref, cw_ref, cb_ref, wg_ref, bg_ref, lam_ref, o_ref,
                xp_ref, xr_ref, hb_ref, ge_ref, hin_ref, *, L, R):
    nchunk = L // R
    ng = R // SUBLANES
    NG = L // SUBLANES
    nsg = NG // SUBLANES
    zeros8 = jnp.zeros((SUBLANES, LANES), F32)
    xp_ref[0:SUBLANES, :] = zeros8
    xp_ref[SUBLANES + L:2 * SUBLANES + L, :] = zeros8
    xp_ref[SUBLANES:SUBLANES + L, :] = x_ref[...]

    sub = lax.broadcasted_iota(jnp.int32, (ng, SUBLANES, LANES), 1)
    c8 = [-C_RGLRU * jax.nn.softplus(-lam_ref[d]) for d in range(2)]
    cb = cb_ref[...]
    cw = [cw_ref[k:k + 1, :] for k in range(4)]

    def conv(c, _):
        t0 = pl.multiple_of(c * R, R)
        xr = cb
        for k in range(4):
            xr = xr + cw[k] * xp_ref[pl.ds(t0 + (SUBLANES - 2 + k), R), :]
        xr_ref[pl.ds(t0, R), :] = xr
        return 0

    lax.fori_loop(0, nchunk, conv, 0)

    def level1(c, _):
        t0 = pl.multiple_of(c * R, R)
        rows = pl.ds(t0, R)
        xr = xr_ref[rows, :]
        gz = jnp.dot(xr.astype(BF16), wg_ref[...], preferred_element_type=F32) + bg_ref[...]
        for d in range(2):
            base = 2 * d * LANES
            r = 0.5 * jnp.tanh(0.5 * gz[:, base:base + LANES]) + 0.5
            i = 0.5 * jnp.tanh(0.5 * gz[:, base + LANES:base + 2 * LANES]) + 0.5
            log_a = c8[d] * r
            a = jnp.exp(log_a)
            b = jnp.sqrt((1.0 + a * a) * jnp.tanh(-log_a)) * (i * xr)
            a, b = _compose_groups(a.reshape(ng, SUBLANES, LANES), b.reshape(ng, SUBLANES, LANES), sub, d == 0)
            a_ref, b_ref = (xp_ref, o_ref) if d == 0 else (xr_ref, hb_ref)
            a_ref[rows, :] = a.reshape(R, LANES)
            b_ref[rows, :] = b.reshape(R, LANES)
        gsl = pl.ds(pl.multiple_of(c * ng, ng), ng)
        ge_ref[0, gsl, :] = xp_ref[pl.ds(t0 + SUBLANES - 1, ng, stride=SUBLANES), :]
        ge_ref[1, gsl, :] = o_ref[pl.ds(t0 + SUBLANES - 1, ng, stride=SUBLANES), :]
        ge_ref[2, gsl, :] = xr_ref[pl.ds(t0, ng, stride=SUBLANES), :]
        ge_ref[3, gsl, :] = hb_ref[pl.ds(t0, ng, stride=SUBLANES), :]
        return 0

    lax.fori_loop(0, nchunk, level1, 0)

    sub2 = lax.broadcasted_iota(jnp.int32, (nsg, SUBLANES, LANES), 1)
    for d in range(2):
        a, b = _compose_groups(ge_ref[2 * d].reshape(nsg, SUBLANES, LANES),
                               ge_ref[2 * d + 1].reshape(nsg, SUBLANES, LANES), sub2, d == 0)
        ge_ref[2 * d] = a.reshape(NG, LANES)
        ge_ref[2 * d + 1] = b.reshape(NG, LANES)

    hin_ref[0, 0:SUBLANES, :] = zeros8
    hin_ref[1, SUBLANES + NG:2 * SUBLANES + NG, :] = zeros8

    def walk(j, hc):
        hf, hk = hc
        rf = pl.multiple_of(j * SUBLANES, SUBLANES)
        h8 = ge_ref[1, pl.ds(rf, SUBLANES), :] + ge_ref[0, pl.ds(rf, SUBLANES), :] * hf
        hin_ref[0, pl.ds(SUBLANES + rf, SUBLANES), :] = h8
        hf = jnp.broadcast_to(h8[SUBLANES - 1:SUBLANES, :], (SUBLANES, LANES))
        rb = pl.multiple_of((nsg - 1 - j) * SUBLANES, SUBLANES)
        k8 = ge_ref[3, pl.ds(rb, SUBLANES), :] + ge_ref[2, pl.ds(rb, SUBLANES), :] * hk
        hin_ref[1, pl.ds(SUBLANES + rb, SUBLANES), :] = k8
        hk = jnp.broadcast_to(k8[0:1, :], (SUBLANES, LANES))
        return hf, hk

    lax.fori_loop(0, nsg, walk, (zeros8, zeros8), unroll=8)

    def finish(c, _):
        t0 = pl.multiple_of(c * R, R)
        rows = pl.ds(t0, R)
        g0 = c * ng
        hin_f = hin_ref[0, pl.ds(g0 + SUBLANES - 1, ng), :]
        hin_b = hin_ref[1, pl.ds(g0 + SUBLANES + 1, ng), :]
        shape3 = (ng, SUBLANES, LANES)
        h_f = (o_ref[rows, :].reshape(shape3)
               + xp_ref[rows, :].reshape(shape3) * jnp.broadcast_to(hin_f[:, None, :], shape3))
        h_b = (hb_ref[rows, :].reshape(shape3)
               + xr_ref[rows, :].reshape(shape3) * jnp.broadcast_to(hin_b[:, None, :], shape3))
        o_ref[rows, :] = g_ref[rows, :] * (h_f + h_b).reshape(R, LANES)
        return 0

    lax.fori_loop(0, nchunk, finish, 0)


def _lru(p3, conv_w, conv_b, wa, ba, wi, bi, lam, R):
    B, L, _ = p3.shape
    D = conv_w.shape[1]
    nt = D // LANES
    hpt = LANES // RNN_BLOCK

    def blockdiag(w):
        w = w.reshape(2, nt, hpt, RNN_BLOCK, RNN_BLOCK)
        eye = jnp.eye(hpt, dtype=w.dtype)
        return jnp.einsum('dthij,hg->dthigj', w, eye).reshape(2, nt, LANES, LANES)

    wa_d, wi_d = blockdiag(wa), blockdiag(wi)
    wg = jnp.concatenate([wa_d[0], wi_d[0], wa_d[1], wi_d[1]], axis=-1).astype(BF16)
    tile = lambda v: v.reshape(2, nt, 1, LANES)
    bg = jnp.concatenate([tile(ba)[0], tile(bi)[0], tile(ba)[1], tile(bi)[1]], axis=-1)
    lam4 = lam.reshape(2, nt, 1, LANES)
    NG = L // SUBLANES
    return pl.pallas_call(
        functools.partial(_lru_kernel, L=L, R=R),
        grid=(B, nt),
        in_specs=[
            pl.BlockSpec((None, L, LANES), lambda b, c: (b, 0, c)),
            pl.BlockSpec((None, L, LANES), lambda b, c: (b, 0, nt + c)),
            pl.BlockSpec((4, LANES), lambda b, c: (0, c)),
            pl.BlockSpec((1, LANES), lambda b, c: (0, c)),
            pl.BlockSpec((None, LANES, 4 * LANES), lambda b, c: (c, 0, 0)),
            pl.BlockSpec((None, 1, 4 * LANES), lambda b, c: (c, 0, 0)),
            pl.BlockSpec((2, None, 1, LANES), lambda b, c: (0, c, 0, 0)),
        ],
        out_specs=pl.BlockSpec((None, L, LANES), lambda b, c: (b, 0, c)),
        out_shape=jax.ShapeDtypeStruct((B, L, D), F32),
        scratch_shapes=[
            pltpu.VMEM((L + 2 * SUBLANES, LANES), F32),
            pltpu.VMEM((L, LANES), F32),
            pltpu.VMEM((L, LANES), F32),
            pltpu.VMEM((4, NG, LANES), F32),
            pltpu.VMEM((2, NG + 2 * SUBLANES, LANES), F32),
        ],
        compiler_params=_cparams(("parallel", "parallel")),
    )(p3, p3, conv_w, conv_b.reshape(1, D), wg, bg, lam4)


def _hy_pre_kernel(v_ref, x1_ref, x2_ref, vp_ref, x1p_ref, x2p_ref, vn_ref, x1n_ref, x2n_ref,
                   w_ref, b_ref, z_ref, x1c_ref, *, nt):
    i = pl.program_id(1)
    tl = v_ref.shape[0]
    row = lax.broadcasted_iota(jnp.int32, v_ref.shape, 0)
    first = i == 0
    last = i == nt - 1

    def conv(c, m_ref, p_ref, n_ref):
        x = m_ref[...]
        prev = jnp.where(first, 0.0, p_ref[SUBLANES - 1:SUBLANES, :])
        nxt = jnp.where(last, 0.0, n_ref[0:1, :])
        xm = jnp.where(row == 0, prev, pltpu.roll(x, 1, 0))
        xn = jnp.where(row == tl - 1, nxt, pltpu.roll(x, tl - 1, 0))
        return b_ref[c] + w_ref[c, 0:1, :] * xm + w_ref[c, 1:2, :] * x + w_ref[c, 2:3, :] * xn

    v = conv(0, v_ref, vp_ref, vn_ref)
    x1c = conv(1, x1_ref, x1p_ref, x1n_ref)
    z = conv(2, x2_ref, x2p_ref, x2n_ref) * v
    for s in range(z_ref.shape[0]):
        x1c_ref[s] = x1c[:, s * LANES:(s + 1) * LANES]
        z_ref[s] = z[:, s * LANES:(s + 1) * LANES]


def _hy_pre(p3, conv_w, conv_b, D, col0, tl, ct):
    B, L, n_in = p3.shape
    nt = L // tl
    ncb = D // ct
    g8 = tl // SUBLANES
    p4 = p3.reshape(B, L // SUBLANES, SUBLANES, n_in)
    w3 = conv_w.reshape(3, 3, D).transpose(1, 0, 2)
    b3 = conv_b.reshape(3, 1, D)
    cb0 = col0 // ct

    def main(part):
        return pl.BlockSpec((None, tl, ct), lambda b, i, c: (b, i, cb0 + part * ncb + c))

    def prev(part):
        return pl.BlockSpec((None, None, SUBLANES, ct),
                            lambda b, i, c: (b, jnp.maximum(i * g8 - 1, 0), 0, cb0 + part * ncb + c))

    def nxt(part):
        return pl.BlockSpec((None, None, SUBLANES, ct),
                            lambda b, i, c: (b, jnp.minimum((i + 1) * g8, L // SUBLANES - 1), 0, cb0 + part * ncb + c))

    out = pl.BlockSpec((ct // LANES, None, tl, LANES), lambda b, i, c: (c, b, i, 0))
    return pl.pallas_call(
        functools.partial(_hy_pre_kernel, nt=nt),
        grid=(B, nt, ncb),
        in_specs=[main(0), main(1), main(2), prev(0), prev(1), prev(2), nxt(0), nxt(1), nxt(2),
                  pl.BlockSpec((3, 3, ct), lambda b, i, c: (0, 0, c)),
                  pl.BlockSpec((3, 1, ct), lambda b, i, c: (0, 0, c))],
        out_specs=[out, out],
        out_shape=[jax.ShapeDtypeStruct((D // LANES, B, L, LANES), F32)] * 2,
        compiler_params=_cparams(("parallel", "parallel", "parallel")),
    )(p3, p3, p3, p4, p4, p4, p4, p4, p4, w3, b3)


def _filt_kernel(feat_ref, t_ref, w1_ref, b1_ref, w2_ref, b2_ref, w3_ref, b3_ref, w4f_ref, w4b_ref, fr_ref, dl_ref,
                 k_ref, s_ref, *, nb):
    i = pl.program_id(0)
    hp = lax.Precision.HIGHEST
    fr = fr_ref[...]
    fwd = i < nb
    hh = jnp.sin(fr * (jnp.dot(feat_ref[...], w1_ref[...], precision=hp, preferred_element_type=F32) + b1_ref[...]))
    hh = jnp.sin(fr * (jnp.dot(hh, w2_ref[...], precision=hp, preferred_element_type=F32) + b2_ref[...]))
    hh = jnp.sin(fr * (jnp.dot(hh, w3_ref[...], precision=hp, preferred_element_type=F32) + b3_ref[...]))
    w4 = jnp.where(fwd, w4f_ref[...], w4b_ref[...])
    decay = jnp.exp(-t_ref[...] * jnp.abs(dl_ref[...]))
    hb16 = hh.astype(BF16)
    k = jnp.dot(hb16, w4.astype(BF16), preferred_element_type=F32) * decay
    row = lax.broadcasted_iota(jnp.int32, k.shape, 0)
    kb0 = (jnp.dot(hb16[0:BF16_ROWS], w4b_ref[...].astype(BF16), preferred_element_type=F32)
           * decay[0:BF16_ROWS])[0:1]
    k = jnp.where(jnp.logical_and(row == 0, i == 0), k + kb0, k)
    k = jnp.where(jnp.logical_and(row == 0, i == nb), 0.0, k)
    for s in range(k_ref.shape[0]):
        k_ref[s] = k[:, s * LANES:(s + 1) * LANES]
    part = jnp.sum(jnp.abs(k), axis=0, keepdims=True)

    @pl.when(i == 0)
    def _():
        s_ref[...] = part

    @pl.when(i > 0)
    def _():
        s_ref[...] += part


def _hyena_filter(L, D, w1, b1, w2, b2, w3, b3, w4, freq, tl):
    t = jnp.linspace(0.0, 1.0, L, dtype=F32)[:, None]
    w = 2.0 * math.pi * jnp.arange(L, dtype=F32)[:, None] / L
    bands = jnp.linspace(1e-4, HY_BANDS - 1, HY_BANDS, dtype=F32)[None, :]
    feats = jnp.concatenate([t, jnp.cos(w * bands), -jnp.sin(w * bands)], axis=-1)
    deltas = jnp.linspace(math.log(HY_TARGET) / HY_SLOW_DECAY, math.log(HY_TARGET) / HY_FAST_DECAY, D, dtype=F32)
    rev = lambda a: jnp.roll(jnp.flip(a, 0), 1, 0)
    feats2 = jnp.concatenate([feats, rev(feats)], axis=0)
    t2 = jnp.concatenate([t, rev(t)], axis=0)
    ff = w1.shape[1]
    nb = L // tl
    full = lambda shape: pl.BlockSpec(shape, lambda i: (0,) * len(shape))
    return pl.pallas_call(
        functools.partial(_filt_kernel, nb=nb),
        grid=(2 * nb,),
        in_specs=[pl.BlockSpec((tl, HY_EMB), lambda i: (i, 0)), pl.BlockSpec((tl, 1), lambda i: (i, 0)),
                  full((HY_EMB, ff)), full((1, ff)), full((ff, ff)), full((1, ff)), full((ff, ff)), full((1, ff)),
                  full((ff, D)), full((ff, D)), full((1, ff)), full((1, D))],
        out_specs=[pl.BlockSpec((D // LANES, tl, LANES), lambda i: (0, i, 0)), pl.BlockSpec((1, D), lambda i: (0, 0))],
        out_shape=[jax.ShapeDtypeStruct((D // LANES, 2 * L, LANES), F32), jax.ShapeDtypeStruct((1, D), F32)],
        compiler_params=_cparams(("arbitrary",)),
    )(feats2, t2, w1, b1.reshape(1, ff), w2, b2.reshape(1, ff), w3, b3.reshape(1, ff), w4[:, :D], w4[:, D:],
      freq.reshape(1, ff), deltas.reshape(1, D))


def _angles(rows, cols, n):
    r = (rows.astype(np.int64) * cols.astype(np.int64)) % n
    th = r.astype(np.float64) * (2.0 * math.pi / n)
    return jnp.asarray(np.cos(th), F32), jnp.asarray(np.sin(th), F32)


def _dft_mats(N1, N2):
    N = N1 * N2
    h = N1 // 2
    ar = lambda n: np.arange(n)
    def add(c1, s1, c2, s2):
        return c1 * c2 - s1 * s2, s1 * c2 + c1 * s2

    cp, sp = _angles(ar(N1)[:, None], ar(N1)[None, :], N1)
    cq, sq = _angles(ar(N2)[:, None], ar(N1)[None, :], N)
    cf, sf = add(cp[None], sp[None], cq[:, :, None], sq[:, :, None])
    c, s = cf[:, :, :h], sf[:, :, :h]
    a_data = jnp.concatenate([jnp.concatenate([c, s], -1), jnp.concatenate([-s, c], -1)], 1)
    a_filt = jnp.concatenate([cf, -sf], 1)
    c, s = _angles(ar(N2)[:, None], ar(N2)[None, :], N2)
    m2 = jnp.concatenate([jnp.concatenate([c, s], -1), jnp.concatenate([-s, c], -1)], 0)
    cr, sr = _angles(ar(N1)[:, None], ar(N2)[None, :], N)
    cs, ss = _angles(ar(N2)[:, None], ar(N2)[None, :], N2)
    c, s = add(cr[:, :, None], sr[:, :, None], cs[None], ss[None])
    hm =jnp.concatenate([jnp.concatenate([c, -s], -1), jnp.concatenate([s, c], -1)], 1)
    c, s = _angles(ar(h)[:, None], ar(N1)[None, :], N1)
    i2 = jnp.concatenate([jnp.concatenate([c, -s], -1), jnp.concatenate([s, c], -1)], 0) / N
    return (a_data.astype(BF16), a_filt.astype(BF16), m2.astype(BF16), hm.astype(BF16), i2.astype(BF16))


BF16_ROWS = 16


def _lane_cat(parts):
    return parts[0] if len(parts) == 1 else jnp.concatenate(parts, axis=1)


def _flatten_in(v, dst_ref, region, nrows):
    for hf in range(BF16_ROWS // SUBLANES):
        dst_ref[pl.ds((2 * region + hf) * nrows, nrows), :] = (
            v[:, hf * SUBLANES:(hf + 1) * SUBLANES, :].reshape(nrows, LANES))


def _unflatten_out(src_ref, region, n, nrows):
    return jnp.concatenate(
        [src_ref[pl.ds((2 * region + hf) * nrows, nrows), :].reshape(n, SUBLANES, LANES)
         for hf in range(BF16_ROWS // SUBLANES)], axis=1)


def _tstep(region, j, n, nrows):
    hf, jj = divmod(j, SUBLANES)
    return pl.ds((2 * region + hf) * nrows + jj, n, stride=SUBLANES)


def _fft1_kernel(*refs, scaled):
    a_ref = refs[0]
    inv_s = refs[1][...] if scaled else None
    x_ref, o_ref, xs_ref, os_ref = refs[-4:]
    ns, nin, rows = x_ref.shape[0], x_ref.shape[1], x_ref.shape[2]
    n1 = o_ref.shape[2]
    xr, orr = rows * SUBLANES, n1 * SUBLANES
    for s in range(ns):
        for q in range(nin):
            _flatten_in(x_ref[s, q], xs_ref, s * nin + q, xr)
    for j in range(BF16_ROWS):
        parts = [_lane_cat([xs_ref[_tstep(s * nin + q, j, rows, xr), :] for s in range(ns)]) for q in range(nin)]
        x = parts[0] if nin == 1 else jnp.concatenate(parts, axis=0)
        if scaled:
            x = x * inv_s
        y = jnp.dot(a_ref[j], x.astype(BF16), preferred_element_type=F32)
        for s in range(ns):
            for c in range(2):
                os_ref[_tstep(s * 2 + c, j, n1, orr), :] = y[c * n1:(c + 1) * n1, s * LANES:(s + 1) * LANES]
    for s in range(ns):
        for c in range(2):
            o_ref[s, c] = _unflatten_out(os_ref, s * 2 + c, n1, orr).astype(BF16)


def _fft1_data(z6, a_data, dt):
    S, _, P, h, N2, _ = z6.shape
    N1 = 2 * h
    ns = dt // LANES
    jb = BF16_ROWS
    return pl.pallas_call(
        functools.partial(_fft1_kernel, scaled=False),
        grid=(N2 // jb, P, S // ns),
        in_specs=[pl.BlockSpec((jb, 2 * N1, N1), lambda j, p, c: (j, 0, 0)),
                  pl.BlockSpec((ns, 2, None, h, jb, LANES), lambda j, p, c: (c, 0, p, 0, j, 0))],
        out_specs=pl.BlockSpec((None, ns, 2, N1, jb, LANES), lambda j, p, c: (p, c, 0, 0, j, 0)),
        out_shape=jax.ShapeDtypeStruct((P, S, 2, N1, N2, LANES), BF16),
        scratch_shapes=[pltpu.VMEM((ns * 2 * h * jb, LANES), F32),
                        pltpu.VMEM((ns * 2 * N1 * jb, LANES), F32)],
        compiler_params=_cparams(("parallel", "parallel", "parallel")),
    )(a_data, z6)


def _fft1_filt(k5, inv_s, a_filt, dt):
    S, _, N1, N2, _ = k5.shape
    ns = dt // LANES
    jb = BF16_ROWS
    return pl.pallas_call(
        functools.partial(_fft1_kernel, scaled=True),
        grid=(N2 // jb, S // ns),
        in_specs=[pl.BlockSpec((jb, 2 * N1, N1), lambda j, c: (j, 0, 0)),
                  pl.BlockSpec((1, dt), lambda j, c: (0, c)),
                  pl.BlockSpec((ns, 1, N1, jb, LANES), lambda j, c: (c, 0, 0, j, 0))],
        out_specs=pl.BlockSpec((ns, 2, N1, jb, LANES), lambda j, c: (c, 0, 0, j, 0)),
        out_shape=jax.ShapeDtypeStruct((S, 2, N1, N2, LANES), BF16),
        scratch_shapes=[pltpu.VMEM((ns * N1 * jb, LANES), F32),
                        pltpu.VMEM((ns * 2 * N1 * jb, LANES), F32)],
        compiler_params=_cparams(("parallel", "parallel")),
    )(a_filt, inv_s, k5)


FFT_MID_K1 = 4


def _fft_mid_kernel(m2_ref, h_ref, y_ref, yk_ref, o_ref, kf_ref):
    S, kb, n2 = y_ref.shape[0], y_ref.shape[2], y_ref.shape[3]

    def stacked(ref, kk):
        return jnp.concatenate([_lane_cat([ref[s, c, kk] for s in range(S)]) for c in range(2)], axis=0)

    @pl.when(pl.program_id(1) == 0)
    def _():
        for kk in range(kb):
            kf_ref[kk] = jnp.dot(m2_ref[...], stacked(yk_ref, kk), preferred_element_type=F32)

    for kk in range(kb):
        x = jnp.dot(m2_ref[...], stacked(y_ref, kk), preferred_element_type=F32)
        xr, xi = x[:n2], x[n2:]
        kr, ki = kf_ref[kk, 0:n2, :], kf_ref[kk, n2:, :]
        pr = xr * kr - xi * ki
        pi = xr * ki + xi * kr
        p = jnp.concatenate([pr, pi], axis=0).astype(BF16)
        u = jnp.dot(h_ref[kk], p, preferred_element_type=F32).astype(BF16)
        for s in range(S):
            for c in range(2):
                o_ref[s, c, kk] = u[c * n2:(c + 1) * n2, s * LANES:(s + 1) * LANES]


def _fft_mid(y, yk, m2, hm):
    P, S, _, N1, N2, _ = y.shape
    kb = FFT_MID_K1
    return pl.pallas_call(
        _fft_mid_kernel,
        grid=(N1 // kb, P),
        in_specs=[pl.BlockSpec((2 * N2, 2 * N2), lambda k, p: (0, 0)),
                  pl.BlockSpec((kb, 2 * N2, 2 * N2), lambda k, p: (k, 0, 0)),
                  pl.BlockSpec((None, S, 2, kb, N2, LANES), lambda k, p: (p, 0, 0, k, 0, 0)),
                  pl.BlockSpec((S, 2, kb, N2, LANES), lambda k, p: (0, 0, k, 0, 0))],
        out_specs=pl.BlockSpec((None, S, 2, kb, N2, LANES), lambda k, p: (p, 0, 0, k, 0, 0)),
        out_shape=jax.ShapeDtypeStruct((P, S, 2, N1, N2, LANES), BF16),
        scratch_shapes=[pltpu.VMEM((kb, 2 * N2, S * LANES), F32)],
        compiler_params=_cparams(("parallel", "arbitrary")),
    )(m2, hm, y, yk)


def _fft_inv_kernel(i2_ref, u_ref, z_ref, x1_ref, bias_ref, o_ref, us_ref, zs_ref, xs_ref, os_ref):
    ns, n1 = u_ref.shape[0], u_ref.shape[2]
    h = o_ref.shape[2]
    ur, hr = n1 * SUBLANES, h * SUBLANES
    bias = bias_ref[...]
    for s in range(ns):
        for c in range(2):
            _flatten_in(u_ref[s, c].astype(F32), us_ref, s * 2 + c, ur)
            _flatten_in(z_ref[s, c], zs_ref, s * 2 + c, hr)
            _flatten_in(x1_ref[s, c], xs_ref, s * 2 + c, hr)
    for j in range(BF16_ROWS):
        up = jnp.concatenate([_lane_cat([us_ref[_tstep(s * 2 + c, j, n1, ur), :] for s in range(ns)])
                              for c in range(2)], axis=0)
        y = jnp.dot(i2_ref[...], up.astype(BF16), preferred_element_type=F32)
        for s in range(ns):
            lanes = slice(s * LANES, (s + 1) * LANES)
            for q in range(2):
                t = _tstep(s * 2 + q, j, h, hr)
                os_ref[t, :] = xs_ref[t, :] * (y[q * h:(q + 1) * h, lanes] + bias[:, lanes] * zs_ref[t, :])
    for s in range(ns):
        for q in range(2):
            o_ref[s, q] = _unflatten_out(os_ref, s * 2 + q, h, hr)


def _fft_inv(u, z6, x16, bias, i2, dt):
    P, S, _, N1, N2, _ = u.shape
    h = N1 // 2
    ns = dt // LANES
    jb = BF16_ROWS
    pair = pl.BlockSpec((ns, 2, None, h, jb, LANES), lambda j, p, c: (c, 0, p, 0, j, 0))
    return pl.pallas_call(
        _fft_inv_kernel,
        grid=(N2 // jb, P, S // ns),
        in_specs=[pl.BlockSpec((N1, 2 * N1), lambda j, p, c: (0, 0)),
                  pl.BlockSpec((None, ns, 2, N1, jb, LANES), lambda j, p, c: (p, c, 0, 0, j, 0)),
                  pair, pair,
                  pl.BlockSpec((1, dt), lambda j, p, c: (0, c))],
        out_specs=pair,
        out_shape=jax.ShapeDtypeStruct((S, 2, P, h, N2, LANES), F32),
        scratch_shapes=[pltpu.VMEM((ns * 2 * N1 * jb, LANES), F32)]
        + [pltpu.VMEM((ns * 2 * h * jb, LANES), F32)] * 3,
        compiler_params=_cparams(("parallel", "parallel", "parallel")),
    )(i2, u, z6, x16, bias)


def _hyena(p3, D, col0, conv_w, conv_b, w1, b1, w2, b2, w3, b3, w4, freq, bias, tl, ct, dt):
    B, L, _ = p3.shape
    N2 = FFT_N2
    N1 = 2 * L // N2
    h = N1 // 2
    S = D // LANES
    z, x1c = _hy_pre(p3, conv_w, conv_b, D, col0, tl, ct)
    k_circ, s = _hyena_filter(L, D, w1, b1, w2, b2, w3, b3, w4, freq, min(tl, 512))
    a_data, a_filt, m2, hm, i2 = _dft_mats(N1, N2)
    yk = _fft1_filt(k_circ.reshape(S, 1, N1, N2, LANES), 1.0 / s, a_filt, dt)
    z6 = z.reshape(S, 2, B // 2, h, N2, LANES)
    y = _fft1_data(z6, a_data, dt)
    u = _fft_mid(y, yk, m2, hm)
    out = _fft_inv(u, z6, x1c.reshape(S, 2, B // 2, h, N2, LANES), bias.reshape(1, D), i2, dt)
    return out.reshape(S, B * L, LANES)


def _merge_kernel(x_ref, yl_ref, yh_ref, gl_ref, gh_ref, wl_ref, wh_ref, wo_ref, gf_ref, wr_ref, br_ref,
                  h_ref, hn_ref, idx_ref, gate_ref):
    a = jnp.dot(yl_ref[...].astype(BF16), wl_ref[...], preferred_element_type=F32)
    yh = _lane_cat([yh_ref[s] for s in range(yh_ref.shape[0])])
    b = jnp.dot(yh.astype(BF16), wh_ref[...], preferred_element_type=F32)
    mix = gl_ref[...] * a + gh_ref[...] * b
    h = x_ref[...] + jnp.dot(mix.astype(BF16), wo_ref[...], preferred_element_type=F32)
    h_ref[...] = h
    ms = jnp.mean(h * h, axis=-1, keepdims=True)
    hn = h * lax.rsqrt(ms + EPS) * gf_ref[...]
    nc = hn.shape[1] // LANES
    for c in range(nc):
        hn_ref[pl.ds(c, hn.shape[0], stride=nc), :] = hn[:, c * LANES:(c + 1) * LANES]
    ne = br_ref.shape[1]
    h_hi = hn.astype(BF16)
    h_lo = (hn - h_hi.astype(F32)).astype(BF16)
    a2 = jnp.dot(h_hi, wr_ref[...], preferred_element_type=F32)
    b2 = jnp.dot(h_lo, wr_ref[:, :LANES], preferred_element_type=F32)
    logits = (a2[:, :LANES] + a2[:, LANES:] + b2)[:, :ne] + br_ref[...]
    lane = lax.broadcasted_iota(jnp.int32, logits.shape, 1)
    vals = []
    for k in range(TOP_K):
        m = jnp.max(logits, axis=-1, keepdims=True)
        ix = jnp.min(jnp.where(logits == m, lane, ne), axis=-1, keepdims=True)
        vals.append(m)
        idx_ref[:, k:k + 1] = ix
        logits = jnp.where(lane == ix, -jnp.inf, logits)
    ex = [jnp.exp(v - vals[0]) for v in vals]
    tot = ex[0] + ex[1] + ex[2] + ex[3]
    for k in range(TOP_K):
        gate_ref[:, k:k + 1] = ex[k] / tot


def _merge(x2, y_lru, y_hy, p_all, w_lru_br, w_hy_br, w_out, g_ffn, w_router, b_router, tm):
    T, D = x2.shape
    ne = w_router.shape[1]
    gcol = p_all.shape[1] // D - 2
    row = lambda i: (i, 0)
    full = lambda i: (0, 0)
    w_hi = w_router.astype(BF16)
    w_lo = (w_router - w_hi.astype(F32)).astype(BF16)
    lane_pad = lambda w: jnp.pad(w, ((0, 0), (0, LANES - ne)))
    wr2 = jnp.concatenate([lane_pad(w_hi), lane_pad(w_lo)], axis=1)
    return pl.pallas_call(
        _merge_kernel,
        grid=(T // tm,),
        in_specs=[pl.BlockSpec((tm, D), row), pl.BlockSpec((tm, D), row),
                  pl.BlockSpec((D // LANES, tm, LANES), lambda i: (0, i, 0)),
                  pl.BlockSpec((tm, D), lambda i: (i, gcol)), pl.BlockSpec((tm, D), lambda i: (i, gcol + 1)),
                  pl.BlockSpec((D, D), full), pl.BlockSpec((D, D), full), pl.BlockSpec((D, D), full),
                  pl.BlockSpec((1, D), full), pl.BlockSpec((D, 2 * LANES), full), pl.BlockSpec((1, ne), full)],
        out_specs=[pl.BlockSpec((tm, D), row), pl.BlockSpec((tm * (D // LANES), LANES), row),
                   pl.BlockSpec((tm, TOP_K), row), pl.BlockSpec((tm, TOP_K), row)],
        out_shape=[jax.ShapeDtypeStruct((T, D), F32), jax.ShapeDtypeStruct((T * (D // LANES), LANES), F32),
                   jax.ShapeDtypeStruct((T, TOP_K), jnp.int32), jax.ShapeDtypeStruct((T, TOP_K), F32)],
        compiler_params=_cparams(("parallel",)),
    )(x2, y_lru, y_hy, p_all, p_all, w_lru_br.astype(BF16), w_hy_br.astype(BF16), w_out.astype(BF16),
      g_ffn.reshape(1, D), wr2, b_router.reshape(1, ne))


ROW_DMA_UNROLL = 8


def _for_rows(n, fn):
    def body(g, c):
        for k in range(ROW_DMA_UNROLL):
            fn(g * ROW_DMA_UNROLL + k, k % 2)
        return c

    lax.fori_loop(0, n // ROW_DMA_UNROLL, body, 0)


def _moe_kernel(ib_ref, ie_ref, bnd_ref, ni_ref, ord_ref, nord_ref, hn_hbm, gate_ref, wgu_ref, bgu_ref,
                wd_ref, bd_ref, yk_hbm, xbuf, ybuf, xs_ref, wgu_bf, wd_bf, gsem, ssem, *, nblk):
    i = pl.program_id(0)
    ni = ni_ref[0]
    nrow = xs_ref.shape[0]
    nc = xbuf.shape[1] // nrow
    dff = wd_ref.shape[0]

    def gather_copy(idx_ref, r, slot):
        tok = lax.shift_right_logical(idx_ref[0, 0, r], 2)
        return pltpu.make_async_copy(hn_hbm.at[tok], xbuf.at[slot, pl.ds(r * nc, nc), :], gsem.at[slot])

    def scatter_copy(r, slot):
        return pltpu.make_async_copy(ybuf.at[slot, pl.ds(r * nc, nc), :], yk_hbm.at[ord_ref[0, 0, r]],
                                     ssem.at[slot])

    @pl.when(i < ni)
    def _():
        b = ib_ref[i]
        e = ie_ref[i]
        slot = b % 2
        first = jnp.logical_or(i == 0, ib_ref[jnp.maximum(i - 1, 0)] != b)
        last = jnp.logical_or(i == ni - 1, ib_ref[jnp.minimum(i + 1, pl.num_programs(0) - 1)] != b)

        @pl.when(i == 0)
        def _():
            _for_rows(nrow, lambda r, p: gather_copy(ord_ref, r, 0).start(priority=p))

        @pl.when(first)
        def _():
            @pl.when(b + 1 < nblk)
            def _():
                _for_rows(nrow, lambda r, p: gather_copy(nord_ref, r, 1 - slot).start(priority=p))

            _for_rows(nrow, lambda r, p: gather_copy(ord_ref, r, slot).wait())

            @pl.when(b >= 2)
            def _():
                _for_rows(nrow, lambda r, p: scatter_copy(r, slot).wait())

        @pl.when(jnp.logical_or(i == 0, ie_ref[jnp.maximum(i - 1, 0)] != e))
        def _():
            wgu_bf[...] = wgu_ref[...].astype(BF16)
            wd_bf[...] = wd_ref[...].astype(BF16)

        for c in range(nc):
            xs_ref[:, c * LANES:(c + 1) * LANES] = xbuf[slot, pl.ds(c, nrow, stride=nc), :].astype(BF16)
        gu = jnp.dot(xs_ref[...], wgu_bf[...], preferred_element_type=F32) + bgu_ref[...]
        g = jnp.minimum(gu[:, :dff], SWIGLU_LIMIT)
        u = jnp.clip(gu[:, dff:], -SWIGLU_LIMIT, SWIGLU_LIMIT)
        act = (u + 1.0) * (g * (0.5 * jnp.tanh((0.5 * SWIGLU_ALPHA) * g) + 0.5))
        y = (jnp.dot(act.astype(BF16), wd_bf[...], preferred_element_type=F32) + bd_ref[...]) * gate_ref[...]
        row = lax.broadcasted_iota(jnp.int32, (nrow, 1), 0) + b * nrow
        mine = jnp.logical_and(row >= bnd_ref[e], row < bnd_ref[e + 1])

        @pl.when(first)
        def _():
            for c in range(nc):
                ybuf[slot, pl.ds(c, nrow, stride=nc), :] = jnp.where(mine, y[:, c * LANES:(c + 1) * LANES], 0.0)

        @pl.when(jnp.logical_not(first))
        def _():
            for c in range(nc):
                rows = pl.ds(c, nrow, stride=nc)
                ybuf[slot, rows, :] = jnp.where(mine, y[:, c * LANES:(c + 1) * LANES], ybuf[slot, rows, :])

        @pl.when(last)
        def _():
            _for_rows(nrow, lambda r, p: scatter_copy(r, slot).start(priority=p))

        @pl.when(i == ni - 1)
        def _():
            _for_rows(nrow, lambda r, p: scatter_copy(r, slot).wait())

            @pl.when(b >= 1)
            def _():
                _for_rows(nrow, lambda r, p: scatter_copy(r, 1 - slot).wait())


def _moe_items(hn, order, g_sorted, item_blk, item_exp, bounds, n_items, w_gate_up, b_gate_up, w_down, b_down):
    T, nc, _ = hn.shape
    D = nc * LANES
    N = order.shape[0]
    nblk = N // MOE_BLOCK
    n_steps = item_blk.shape[0]
    dff = w_down.shape[1]
    order3 = order.reshape(nblk, 1, MOE_BLOCK)
    grid_spec = pltpu.PrefetchScalarGridSpec(
        num_scalar_prefetch=4,
        grid=(n_steps,),
        in_specs=[
            pl.BlockSpec((1, 1, MOE_BLOCK), lambda i, ib, ie, bd, ni: (ib[i], 0, 0), memory_space=pltpu.SMEM),
            pl.BlockSpec((1, 1, MOE_BLOCK), lambda i, ib, ie, bd, ni: (jnp.minimum(ib[i] + 1, nblk - 1), 0, 0),
                         memory_space=pltpu.SMEM),
            pl.BlockSpec(memory_space=pl.ANY),
            pl.BlockSpec((MOE_BLOCK, 1), lambda i, ib, ie, bd, ni: (ib[i], 0)),
            pl.BlockSpec((None, D, 2 * dff), lambda i, ib, ie, bd, ni: (ie[i], 0, 0)),
            pl.BlockSpec((None, 1, 2 * dff), lambda i, ib, ie, bd, ni: (ie[i], 0, 0)),
            pl.BlockSpec((None, dff, D), lambda i, ib, ie, bd, ni: (ie[i], 0, 0)),
            pl.BlockSpec((None, 1, D), lambda i, ib, ie, bd, ni: (ie[i], 0, 0)),
        ],
        out_specs=pl.BlockSpec(memory_space=pl.ANY),
        scratch_shapes=[pltpu.VMEM((2, MOE_BLOCK * nc, LANES), F32), pltpu.VMEM((2, MOE_BLOCK * nc, LANES), F32),
                        pltpu.VMEM((MOE_BLOCK, D), BF16),
                        pltpu.VMEM((D, 2 * dff), BF16), pltpu.VMEM((dff, D), BF16),
                        pltpu.SemaphoreType.DMA((2,)), pltpu.SemaphoreType.DMA((2,))],
    )
    return pl.pallas_call(
        functools.partial(_moe_kernel, nblk=nblk),
        grid_spec=grid_spec,
        out_shape=jax.ShapeDtypeStruct((N, nc, LANES), F32),
        compiler_params=_cparams(("arbitrary",)),
    )(item_blk, item_exp, bounds, n_items, order3, order3, hn, g_sorted.reshape(N, 1),
      w_gate_up, b_gate_up.reshape(N_EXPERTS, 1, 2 * dff), w_down, b_down.reshape(N_EXPERTS, 1, D))


def _combine_kernel(yk_ref, h_ref, g_ref, o_ref, ys_ref, hs_ref):
    tq = h_ref.shape[0]
    nc = yk_ref.shape[1] // TOP_K
    y = yk_ref[:, 0:nc, :]
    for k in range(1, TOP_K):
        y = y + yk_ref[:, k * nc:(k + 1) * nc, :]
    ys_ref[...] = y.reshape(tq * nc, LANES)
    for c in range(nc):
        hs_ref[:, c * LANES:(c + 1) * LANES] = (h_ref[:, c * LANES:(c + 1) * LANES]
                                                + ys_ref[pl.ds(c, tq, stride=nc), :])
    h = hs_ref[...]
    ms = jnp.mean(h * h, axis=-1, keepdims=True)
    o_ref[...] = h * lax.rsqrt(ms + EPS) * g_ref[...]


def _combine(yk, h, g_final, tq):
    T, D = h.shape
    nc = yk.shape[1]
    return pl.pallas_call(
        _combine_kernel,
        grid=(T // tq,),
        in_specs=[pl.BlockSpec((tq, TOP_K * nc, LANES), lambda i: (i, 0, 0)),
                  pl.BlockSpec((tq, D), lambda i: (i, 0)),
                  pl.BlockSpec((1, D), lambda i: (0, 0))],
        out_specs=pl.BlockSpec((tq, D), lambda i: (i, 0)),
        out_shape=jax.ShapeDtypeStruct((T, D), F32),
        scratch_shapes=[pltpu.VMEM((tq * nc, LANES), F32), pltpu.VMEM((tq, D), F32)],
        compiler_params=_cparams(("parallel",)),
    )(yk.reshape(T, TOP_K * nc, LANES), h, g_final.reshape(1, D))


def _route(top_idx, gates):
    T = top_idx.shape[0]
    N = T * TOP_K
    nblk = N // MOE_BLOCK
    i32 = jnp.int32
    e_sorted, order, g_sorted = lax.sort(
        (top_idx.reshape(N), jnp.arange(N, dtype=i32), gates.reshape(N)), num_keys=1)
    bounds = jnp.sum(e_sorted[None, :] < jnp.arange(N_EXPERTS + 1, dtype=i32)[:, None], axis=1, dtype=i32)
    starts, ends = bounds[:-1], bounds[1:]
    first_blk = starts // MOE_BLOCK
    n_of = jnp.where(ends > starts, (ends - 1) // MOE_BLOCK - first_blk + 1, 0)
    item_end = jnp.cumsum(n_of)
    item_start = item_end - n_of
    n_items = item_end[-1:]
    n_steps = nblk + N_EXPERTS - 1
    i = jnp.arange(n_steps, dtype=i32)
    ie = jnp.minimum(jnp.sum(item_end[None, :] <= i[:, None], axis=1, dtype=i32), N_EXPERTS - 1)
    ib = jnp.clip(first_blk[ie] + i - item_start[ie], 0, nblk - 1).astype(i32)
    return order, g_sorted, ib, ie, bounds, n_items.astype(i32)


def kernel(x, g_mix, w_in, b_in, lru_conv_w, lru_conv_b, lru_wa, lru_ba, lru_wi, lru_bi, lru_lambda, w_lru_br, hy_conv_w, hy_conv_b, hy_w1, hy_b1, hy_w2, hy_b2, hy_w3, hy_b3, hy_w4, hy_sin_freq, hy_bias, w_hy_br, w_out, g_ffn, w_router, b_router, w_gate_up, b_gate_up, w_down, b_down, g_final):
    B, L, D = x.shape
    T = B * L
    x2 = x.reshape(T, D)
    tm = min(2048, T)
    p_all = _inproj(x2, g_mix, w_in, b_in, tm)
    p3 = p_all.reshape(B, L, p_all.shape[1])
    y_lru = _lru(p3, lru_conv_w, lru_conv_b, lru_wa, lru_ba, lru_wi, lru_bi, lru_lambda, min(512, L))
    y_hy = _hyena(p3, D, 2 * D, hy_conv_w, hy_conv_b, hy_w1, hy_b1, hy_w2, hy_b2, hy_w3, hy_b3, hy_w4,
                  hy_sin_freq, hy_bias, min(1024, L), min(512, D), min(256, D))
    h, hn, top_idx, gates = _merge(x2, y_lru.reshape(T, D), y_hy, p_all, w_lru_br, w_hy_br, w_out,
                                   g_ffn, w_router, b_router, min(512, T))
    order, g_sorted, item_blk, item_exp, bounds, n_items = _route(top_idx, gates)
    yk = _moe_items(hn.reshape(T, D // LANES, LANES), order, g_sorted, item_blk, item_exp, bounds, n_items,
                    w_gate_up, b_gate_up, w_down, b_down)
    out = _combine(yk, h, g_final, min(512, T))
    return out.reshape(B, L, D)
```

```python
import functools
import math

import numpy as np
import jax
import jax.numpy as jnp
from jax import lax
from jax.experimental import pallas as pl
from jax.experimental.pallas import tpu as pltpu

F32 = jnp.float32
BF16 = jnp.bfloat16

EPS = 1e-5
C_RGLRU = 8.0
RNN_BLOCK = 64
HY_EMB = 33
HY_BANDS = 16
HY_FAST_DECAY = 0.3
HY_SLOW_DECAY = 1.5
HY_TARGET = 1e-2
N_EXPERTS = 32
TOP_K = 4
SWIGLU_LIMIT = 7.0
SWIGLU_ALPHA = 1.702
MOE_BLOCK = 512

LANES = 128
SUBLANES = 8
FFT_N2 = 128
VMEM_LIMIT = 56 * 1024 * 1024


def _cparams(sem):
    return pltpu.CompilerParams(dimension_semantics=sem, vmem_limit_bytes=VMEM_LIMIT)


def _inproj_kernel(x_ref, g_ref, w_ref, b_ref, o_ref, u_ref):
    j = pl.program_id(1)

    @pl.when(j == 0)
    def _():
        x = x_ref[...]
        ms = jnp.mean(x * x, axis=-1, keepdims=True)
        u_ref[...] = (x * lax.rsqrt(ms + EPS) * g_ref[...]).astype(BF16)

    def proj():
        return jnp.dot(u_ref[...], w_ref[...], preferred_element_type=F32) + b_ref[...]

    @pl.when(j == 1)
    def _():
        o_ref[...] = jax.nn.gelu(proj())

    @pl.when(j >= 5)
    def _():
        o_ref[...] = 0.5 * jnp.tanh(0.5 * proj()) + 0.5

    @pl.when(jnp.logical_and(j != 1, j < 5))
    def _():
        o_ref[...] = proj()


def _inproj(x2, g_mix, w_in, b_in, tm):
    T, D = x2.shape
    n_in = w_in.shape[1]
    return pl.pallas_call(
        _inproj_kernel,
        grid=(T // tm, n_in // D),
        in_specs=[
            pl.BlockSpec((tm, D), lambda i, j: (i, 0)),
            pl.BlockSpec((1, D), lambda i, j: (0, 0)),
            pl.BlockSpec((D, D), lambda i, j: (0, j)),
            pl.BlockSpec((1, D), lambda i, j: (0, j)),
        ],
        out_specs=pl.BlockSpec((tm, D), lambda i, j: (i, j)),
        out_shape=jax.ShapeDtypeStruct((T, n_in), F32),
        scratch_shapes=[pltpu.VMEM((tm, D), BF16)],
        compiler_params=_cparams(("parallel", "arbitrary")),
    )(x2, g_mix.reshape(1, D), w_in.astype(BF16), b_in.reshape(1, n_in))


def _compose_groups(a, b, sub, forward):
    for s in (1, 2, 4):
        shift = s if forward else SUBLANES - s
        a_sh = pltpu.roll(a, shift, 1)
        b_sh = pltpu.roll(b, shift, 1)
        m = (sub >= s) if forward else (sub < SUBLANES - s)
        b = jnp.where(m, a * b_sh + b, b)
        a = jnp.where(m, a * a_sh, a)
    return a, b


def _lru_kernel(x_ref, g_ref, cw_ref, cb_ref, wg_ref, bg_ref, lam_ref, o_ref,
                xp_ref, xr_ref, hb_ref, ge_ref, hin_ref, *, L, R):
    nchunk = L // R
    ng = R // SUBLANES
    NG = L // SUBLANES
    nsg = NG // SUBLANES
    zeros8 = jnp.zeros((SUBLANES, LANES), F32)
    xp_ref[0:SUBLANES, :] = zeros8
    xp_ref[SUBLANES + L:2 * SUBLANES + L, :] = zeros8
    xp_ref[SUBLANES:SUBLANES + L, :] = x_ref[...]

    sub = lax.broadcasted_iota(jnp.int32, (ng, SUBLANES, LANES), 1)
    c8 = [-C_RGLRU * jax.nn.softplus(-lam_ref[d]) for d in range(2)]
    cb = cb_ref[...]
    cw = [cw_ref[k:k + 1, :] for k in range(4)]

    def conv(c, _):
        t0 = pl.multiple_of(c * R, R)
        xr = cb
        for k in range(4):
            xr = xr + cw[k] * xp_ref[pl.ds(t0 + (SUBLANES - 2 + k), R), :]
        xr_ref[pl.ds(t0, R), :] = xr
        return 0

    lax.fori_loop(0, nchunk, conv, 0)

    def level1(c, _):
        t0 = pl.multiple_of(c * R, R)
        rows = pl.ds(t0, R)
        xr = xr_ref[rows, :]
        gz = jnp.dot(xr.astype(BF16), wg_ref[...], preferred_element_type=F32) + bg_ref[...]
        for d in range(2):
            base = 2 * d * LANES
            r = 0.5 * jnp.tanh(0.5 * gz[:, base:base + LANES]) + 0.5
            i = 0.5 * jnp.tanh(0.5 * gz[:, base + LANES:base + 2 * LANES]) + 0.5
            log_a = c8[d] * r
            a = jnp.exp(log_a)
            b = jnp.sqrt((1.0 + a * a) * jnp.tanh(-log_a)) * (i * xr)
            a, b = _compose_groups(a.reshape(ng, SUBLANES, LANES), b.reshape(ng, SUBLANES, LANES), sub, d == 0)
            a_ref, b_ref = (xp_ref, o_ref) if d == 0 else (xr_ref, hb_ref)
            a_ref[rows, :] = a.reshape(R, LANES)
            b_ref[rows, :] = b.reshape(R, LANES)
        gsl = pl.ds(pl.multiple_of(c * ng, ng), ng)
        ge_ref[0, gsl, :] = xp_ref[pl.ds(t0 + SUBLANES - 1, ng, stride=SUBLANES), :]
        ge_ref[1, gsl, :] = o_ref[pl.ds(t0 + SUBLANES - 1, ng, stride=SUBLANES), :]
        ge_ref[2, gsl, :] = xr_ref[pl.ds(t0, ng, stride=SUBLANES), :]
        ge_ref[3, gsl, :] = hb_ref[pl.ds(t0, ng, stride=SUBLANES), :]
        return 0

    lax.fori_loop(0, nchunk, level1, 0)

    sub2 = lax.broadcasted_iota(jnp.int32, (nsg, SUBLANES, LANES), 1)
    for d in range(2):
        a, b = _compose_groups(ge_ref[2 * d].reshape(nsg, SUBLANES, LANES),
                               ge_ref[2 * d + 1].reshape(nsg, SUBLANES, LANES), sub2, d == 0)
        ge_ref[2 * d] = a.reshape(NG, LANES)
        ge_ref[2 * d + 1] = b.reshape(NG, LANES)

    hin_ref[0, 0:SUBLANES, :] = zeros8
    hin_ref[1, SUBLANES + NG:2 * SUBLANES + NG, :] = zeros8

    def walk(j, hc):
        hf, hk = hc
        rf = pl.multiple_of(j * SUBLANES, SUBLANES)
        h8 = ge_ref[1, pl.ds(rf, SUBLANES), :] + ge_ref[0, pl.ds(rf, SUBLANES), :] * hf
        hin_ref[0, pl.ds(SUBLANES + rf, SUBLANES), :] = h8
        hf = jnp.broadcast_to(h8[SUBLANES - 1:SUBLANES, :], (SUBLANES, LANES))
        rb = pl.multiple_of((nsg - 1 - j) * SUBLANES, SUBLANES)
        k8 = ge_ref[3, pl.ds(rb, SUBLANES), :] + ge_ref[2, pl.ds(rb, SUBLANES), :] * hk
        hin_ref[1, pl.ds(SUBLANES + rb, SUBLANES), :] = k8
        hk = jnp.broadcast_to(k8[0:1, :], (SUBLANES, LANES))
        return hf, hk

    lax.fori_loop(0, nsg, walk, (zeros8, zeros8), unroll=8)

    def finish(c, _):
        t0 = pl.multiple_of(c * R, R)
        rows = pl.ds(t0, R)
        g0 = c * ng
        hin_f = hin_ref[0, pl.ds(g0 + SUBLANES - 1, ng), :]
        hin_b = hin_ref[1, pl.ds(g0 + SUBLANES + 1, ng), :]
        shape3 = (ng, SUBLANES, LANES)
        h_f = (o_ref[rows, :].reshape(shape3)
               + xp_ref[rows, :].reshape(shape3) * jnp.broadcast_to(hin_f[:, None, :], shape3))
        h_b = (hb_ref[rows, :].reshape(shape3)
               + xr_ref[rows, :].reshape(shape3) * jnp.broadcast_to(hin_b[:, None, :], shape3))
        o_ref[rows, :] = g_ref[rows, :] * (h_f + h_b).reshape(R, LANES)
        return 0

    lax.fori_loop(0, nchunk, finish, 0)


def _lru(p3, conv_w, conv_b, wa, ba, wi, bi, lam, R):
    B, L, _ = p3.shape
    D = conv_w.shape[1]
    nt = D // LANES
    hpt = LANES // RNN_BLOCK

    def blockdiag(w):
        w = w.reshape(2, nt, hpt, RNN_BLOCK, RNN_BLOCK)
        eye = jnp.eye(hpt, dtype=w.dtype)
        return jnp.einsum('dthij,hg->dthigj', w, eye).reshape(2, nt, LANES, LANES)

    wa_d, wi_d = blockdiag(wa), blockdiag(wi)
    wg = jnp.concatenate([wa_d[0], wi_d[0], wa_d[1], wi_d[1]], axis=-1).astype(BF16)
    tile = lambda v: v.reshape(2, nt, 1, LANES)
    bg = jnp.concatenate([tile(ba)[0], tile(bi)[0], tile(ba)[1], tile(bi)[1]], axis=-1)
    lam4 = lam.reshape(2, nt, 1, LANES)
    NG = L // SUBLANES
    return pl.pallas_call(
        functools.partial(_lru_kernel, L=L, R=R),
        grid=(B, nt),
        in_specs=[
            pl.BlockSpec((None, L, LANES), lambda b, c: (b, 0, c)),
            pl.BlockSpec((None, L, LANES), lambda b, c: (b, 0, nt + c)),
            pl.BlockSpec((4, LANES), lambda b, c: (0, c)),
            pl.BlockSpec((1, LANES), lambda b, c: (0, c)),
            pl.BlockSpec((None, LANES, 4 * LANES), lambda b, c: (c, 0, 0)),
            pl.BlockSpec((None, 1, 4 * LANES), lambda b, c: (c, 0, 0)),
            pl.BlockSpec((2, None, 1, LANES), lambda b, c: (0, c, 0, 0)),
        ],
        out_specs=pl.BlockSpec((None, L, LANES), lambda b, c: (b, 0, c)),
        out_shape=jax.ShapeDtypeStruct((B, L, D), F32),
        scratch_shapes=[
            pltpu.VMEM((L + 2 * SUBLANES, LANES), F32),
            pltpu.VMEM((L, LANES), F32),
            pltpu.VMEM((L, LANES), F32),
            pltpu.VMEM((4, NG, LANES), F32),
            pltpu.VMEM((2, NG + 2 * SUBLANES, LANES), F32),
        ],
        compiler_params=_cparams(("parallel", "parallel")),
    )(p3, p3, conv_w, conv_b.reshape(1, D), wg, bg, lam4)


def _hy_pre_kernel(v_ref, x1_ref, x2_ref, vp_ref, x1p_ref, x2p_ref, vn_ref, x1n_ref, x2n_ref,
                   w_ref, b_ref, z_ref, x1c_ref, *, nt):
    i = pl.program_id(1)
    tl = v_ref.shape[0]
    row = lax.broadcasted_iota(jnp.int32, v_ref.shape, 0)
    first = i == 0
    last = i == nt - 1

    def conv(c, m_ref, p_ref, n_ref):
        x = m_ref[...]
        prev = jnp.where(first, 0.0, p_ref[SUBLANES - 1:SUBLANES, :])
        nxt = jnp.where(last, 0.0, n_ref[0:1, :])
        xm = jnp.where(row == 0, prev, pltpu.roll(x, 1, 0))
        xn = jnp.where(row == tl - 1, nxt, pltpu.roll(x, tl - 1, 0))
        return b_ref[c] + w_ref[c, 0:1, :] * xm + w_ref[c, 1:2, :] * x + w_ref[c, 2:3, :] * xn

    v = conv(0, v_ref, vp_ref, vn_ref)
    x1c = conv(1, x1_ref, x1p_ref, x1n_ref)
    z = conv(2, x2_ref, x2p_ref, x2n_ref) * v
    for s in range(z_ref.shape[0]):
        x1c_ref[s] = x1c[:, s * LANES:(s + 1) * LANES]
        z_ref[s] = z[:, s * LANES:(s + 1) * LANES]


def _hy_pre(p3, conv_w, conv_b, D, col0, tl, ct):
    B, L, n_in = p3.shape
    nt = L // tl
    ncb = D // ct
    g8 = tl // SUBLANES
    p4 = p3.reshape(B, L // SUBLANES, SUBLANES, n_in)
    w3 = conv_w.reshape(3, 3, D).transpose(1, 0, 2)
    b3 = conv_b.reshape(3, 1, D)
    cb0 = col0 // ct

    def main(part):
        return pl.BlockSpec((None, tl, ct), lambda b, i, c: (b, i, cb0 + part * ncb + c))

    def prev(part):
        return pl.BlockSpec((None, None, SUBLANES, ct),
                            lambda b, i, c: (b, jnp.maximum(i * g8 - 1, 0), 0, cb0 + part * ncb + c))

    def nxt(part):
        return pl.BlockSpec((None, None, SUBLANES, ct),
                            lambda b, i, c: (b, jnp.minimum((i + 1) * g8, L // SUBLANES - 1), 0, cb0 + part * ncb + c))

    out = pl.BlockSpec((ct // LANES, None, tl, LANES), lambda b, i, c: (c, b, i, 0))
    return pl.pallas_call(
        functools.partial(_hy_pre_kernel, nt=nt),
        grid=(B, nt, ncb),
        in_specs=[main(0), main(1), main(2), prev(0), prev(1), prev(2), nxt(0), nxt(1), nxt(2),
                  pl.BlockSpec((3, 3, ct), lambda b, i, c: (0, 0, c)),
                  pl.BlockSpec((3, 1, ct), lambda b, i, c: (0, 0, c))],
        out_specs=[out, out],
        out_shape=[jax.ShapeDtypeStruct((D // LANES, B, L, LANES), F32)] * 2,
        compiler_params=_cparams(("parallel", "parallel", "parallel")),
    )(p3, p3, p3, p4, p4, p4, p4, p4, p4, w3, b3)


def _filt_kernel(feat_ref, t_ref, w1_ref, b1_ref, w2_ref, b2_ref, w3_ref, b3_ref, w4f_ref, w4b_ref, fr_ref, dl_ref,
                 k_ref, s_ref, *, nb):
    i = pl.program_id(0)
    hp = lax.Precision.HIGHEST
    fr = fr_ref[...]
    fwd = i < nb
    hh = jnp.sin(fr * (jnp.dot(feat_ref[...], w1_ref[...], precision=hp, preferred_element_type=F32) + b1_ref[...]))
    hh = jnp.sin(fr * (jnp.dot(hh, w2_ref[...], precision=hp, preferred_element_type=F32) + b2_ref[...]))
    hh = jnp.sin(fr * (jnp.dot(hh, w3_ref[...], precision=hp, preferred_element_type=F32) + b3_ref[...]))
    w4 = jnp.where(fwd, w4f_ref[...], w4b_ref[...])
    decay = jnp.exp(-t_ref[...] * jnp.abs(dl_ref[...]))
    hb16 = hh.astype(BF16)
    k = jnp.dot(hb16, w4.astype(BF16), preferred_element_type=F32) * decay
    row = lax.broadcasted_iota(jnp.int32, k.shape, 0)
    kb0 = (jnp.dot(hb16[0:BF16_ROWS], w4b_ref[...].astype(BF16), preferred_element_type=F32)
           * decay[0:BF16_ROWS])[0:1]
    k = jnp.where(jnp.logical_and(row == 0, i == 0), k + kb0, k)
    k = jnp.where(jnp.logical_and(row == 0, i == nb), 0.0, k)
    for s in range(k_ref.shape[0]):
        k_ref[s] = k[:, s * LANES:(s + 1) * LANES]
    part = jnp.sum(jnp.abs(k), axis=0, keepdims=True)

    @pl.when(i == 0)
    def _():
        s_ref[...] = part

    @pl.when(i > 0)
    def _():
        s_ref[...] += part


def _hyena_filter(L, D, w1, b1, w2, b2, w3, b3, w4, freq, tl):
    t = jnp.linspace(0.0, 1.0, L, dtype=F32)[:, None]
    w = 2.0 * math.pi * jnp.arange(L, dtype=F32)[:, None] / L
    bands = jnp.linspace(1e-4, HY_BANDS - 1, HY_BANDS, dtype=F32)[None, :]
    feats = jnp.concatenate([t, jnp.cos(w * bands), -jnp.sin(w * bands)], axis=-1)
    deltas = jnp.linspace(math.log(HY_TARGET) / HY_SLOW_DECAY, math.log(HY_TARGET) / HY_FAST_DECAY, D, dtype=F32)
    rev = lambda a: jnp.roll(jnp.flip(a, 0), 1, 0)
    feats2 = jnp.concatenate([feats, rev(feats)], axis=0)
    t2 = jnp.concatenate([t, rev(t)], axis=0)
    ff = w1.shape[1]
    nb = L // tl
    full = lambda shape: pl.BlockSpec(shape, lambda i: (0,) * len(shape))
    return pl.pallas_call(
        functools.partial(_filt_kernel, nb=nb),
        grid=(2 * nb,),
        in_specs=[pl.BlockSpec((tl, HY_EMB), lambda i: (i, 0)), pl.BlockSpec((tl, 1), lambda i: (i, 0)),
                  full((HY_EMB, ff)), full((1, ff)), full((ff, ff)), full((1, ff)), full((ff, ff)), full((1, ff)),
                  full((ff, D)), full((ff, D)), full((1, ff)), full((1, D))],
        out_specs=[pl.BlockSpec((D // LANES, tl, LANES), lambda i: (0, i, 0)), pl.BlockSpec((1, D), lambda i: (0, 0))],
        out_shape=[jax.ShapeDtypeStruct((D // LANES, 2 * L, LANES), F32), jax.ShapeDtypeStruct((1, D), F32)],
        compiler_params=_cparams(("arbitrary",)),
    )(feats2, t2, w1, b1.reshape(1, ff), w2, b2.reshape(1, ff), w3, b3.reshape(1, ff), w4[:, :D], w4[:, D:],
      freq.reshape(1, ff), deltas.reshape(1, D))


def _angles(rows, cols, n):
    r = (rows.astype(np.int64) * cols.astype(np.int64)) % n
    th = r.astype(np.float64) * (2.0 * math.pi / n)
    return jnp.asarray(np.cos(th), F32), jnp.asarray(np.sin(th), F32)


def _dft_mats(N1, N2):
    N = N1 * N2
    h = N1 // 2
    ar = lambda n: np.arange(n)
    def add(c1, s1, c2, s2):
        return c1 * c2 - s1 * s2, s1 * c2 + c1 * s2

    cp, sp = _angles(ar(N1)[:, None], ar(N1)[None, :], N1)
    cq, sq = _angles(ar(N2)[:, None], ar(N1)[None, :], N)
    cf, sf = add(cp[None], sp[None], cq[:, :, None], sq[:, :, None])
    c, s = cf[:, :, :h], sf[:, :, :h]
    a_data = jnp.concatenate([jnp.concatenate([c, s], -1), jnp.concatenate([-s, c], -1)], 1)
    a_filt = jnp.concatenate([cf, -sf], 1)
    c, s = _angles(ar(N2)[:, None], ar(N2)[None, :], N2)
    m2 = jnp.concatenate([jnp.concatenate([c, s], -1), jnp.concatenate([-s, c], -1)], 0)
    cr, sr = _angles(ar(N1)[:, None], ar(N2)[None, :], N)
    cs, ss = _angles(ar(N2)[:, None], ar(N2)[None, :], N2)
    c, s = add(cr[:, :, None], sr[:, :, None], cs[None], ss[None])
    hm =jnp.concatenate([jnp.concatenate([c, -s], -1), jnp.concatenate([s, c], -1)], 1)
    c, s = _angles(ar(h)[:, None], ar(N1)[None, :], N1)
    i2 = jnp.concatenate([jnp.concatenate([c, -s], -1), jnp.concatenate([s, c], -1)], 0) / N
    return (a_data.astype(BF16), a_filt.astype(BF16), m2.astype(BF16), hm.astype(BF16), i2.astype(BF16))


BF16_ROWS = 16


def _lane_cat(parts):
    return parts[0] if len(parts) == 1 else jnp.concatenate(parts, axis=1)


def _flatten_in(v, dst_ref, region, nrows):
    for hf in range(BF16_ROWS // SUBLANES):
        dst_ref[pl.ds((2 * region + hf) * nrows, nrows), :] = (
            v[:, hf * SUBLANES:(hf + 1) * SUBLANES, :].reshape(nrows, LANES))


def _unflatten_out(src_ref, region, n, nrows):
    return jnp.concatenate(
        [src_ref[pl.ds((2 * region + hf) * nrows, nrows), :].reshape(n, SUBLANES, LANES)
         for hf in range(BF16_ROWS // SUBLANES)], axis=1)


def _tstep(region, j, n, nrows):
    hf, jj = divmod(j, SUBLANES)
    return pl.ds((2 * region + hf) * nrows + jj, n, stride=SUBLANES)


def _fft1_kernel(*refs, scaled):
    a_ref = refs[0]
    inv_s = refs[1][...] if scaled else None
    x_ref, o_ref, xs_ref, os_ref = refs[-4:]
    ns, nin, rows = x_ref.shape[0], x_ref.shape[1], x_ref.shape[2]
    n1 = o_ref.shape[2]
    xr, orr = rows * SUBLANES, n1 * SUBLANES
    for s in range(ns):
        for q in range(nin):
            _flatten_in(x_ref[s, q], xs_ref, s * nin + q, xr)
    for j in range(BF16_ROWS):
        parts = [_lane_cat([xs_ref[_tstep(s * nin + q, j, rows, xr), :] for s in range(ns)]) for q in range(nin)]
        x = parts[0] if nin == 1 else jnp.concatenate(parts, axis=0)
        if scaled:
            x = x * inv_s
        y = jnp.dot(a_ref[j], x.astype(BF16), preferred_element_type=F32)
        for s in range(ns):
            for c in range(2):
                os_ref[_tstep(s * 2 + c, j, n1, orr), :] = y[c * n1:(c + 1) * n1, s * LANES:(s + 1) * LANES]
    for s in range(ns):
        for c in range(2):
            o_ref[s, c] = _unflatten_out(os_ref, s * 2 + c, n1, orr).astype(BF16)


def _fft1_data(z6, a_data, dt):
    S, _, P, h, N2, _ = z6.shape
    N1 = 2 * h
    ns = dt // LANES
    jb = BF16_ROWS
    return pl.pallas_call(
        functools.partial(_fft1_kernel, scaled=False),
        grid=(N2 // jb, P, S // ns),
        in_specs=[pl.BlockSpec((jb, 2 * N1, N1), lambda j, p, c: (j, 0, 0)),
                  pl.BlockSpec((ns, 2, None, h, jb, LANES), lambda j, p, c: (c, 0, p, 0, j, 0))],
        out_specs=pl.BlockSpec((None, ns, 2, N1, jb, LANES), lambda j, p, c: (p, c, 0, 0, j, 0)),
        out_shape=jax.ShapeDtypeStruct((P, S, 2, N1, N2, LANES), BF16),
        scratch_shapes=[pltpu.VMEM((ns * 2 * h * jb, LANES), F32),
                        pltpu.VMEM((ns * 2 * N1 * jb, LANES), F32)],
        compiler_params=_cparams(("parallel", "parallel", "parallel")),
    )(a_data, z6)


def _fft1_filt(k5, inv_s, a_filt, dt):
    S, _, N1, N2, _ = k5.shape
    ns = dt // LANES
    jb = BF16_ROWS
    return pl.pallas_call(
        functools.partial(_fft1_kernel, scaled=True),
        grid=(N2 // jb, S // ns),
        in_specs=[pl.BlockSpec((jb, 2 * N1, N1), lambda j, c: (j, 0, 0)),
                  pl.BlockSpec((1, dt), lambda j, c: (0, c)),
                  pl.BlockSpec((ns, 1, N1, jb, LANES), lambda j, c: (c, 0, 0, j, 0))],
        out_specs=pl.BlockSpec((ns, 2, N1, jb, LANES), lambda j, c: (c, 0, 0, j, 0)),
        out_shape=jax.ShapeDtypeStruct((S, 2, N1, N2, LANES), BF16),
        scratch_shapes=[pltpu.VMEM((ns * N1 * jb, LANES), F32),
                        pltpu.VMEM((ns * 2 * N1 * jb, LANES), F32)],
        compiler_params=_cparams(("parallel", "parallel")),
    )(a_filt, inv_s, k5)


FFT_MID_K1 = 4


def _fft_mid_kernel(m2_ref, h_ref, y_ref, yk_ref, o_ref, kf_ref):
    S, kb, n2 = y_ref.shape[0], y_ref.shape[2], y_ref.shape[3]

    def stacked(ref, kk):
        return jnp.concatenate([_lane_cat([ref[s, c, kk] for s in range(S)]) for c in range(2)], axis=0)

    @pl.when(pl.program_id(1) == 0)
    def _():
        for kk in range(kb):
            kf_ref[kk] = jnp.dot(m2_ref[...], stacked(yk_ref, kk), preferred_element_type=F32)

    for kk in range(kb):
        x = jnp.dot(m2_ref[...], stacked(y_ref, kk), preferred_element_type=F32)
        xr, xi = x[:n2], x[n2:]
        kr, ki = kf_ref[kk, 0:n2, :], kf_ref[kk, n2:, :]
        pr = xr * kr - xi * ki
        pi = xr * ki + xi * kr
        p = jnp.concatenate([pr, pi], axis=0).astype(BF16)
        u = jnp.dot(h_ref[kk], p, preferred_element_type=F32).astype(BF16)
        for s in range(S):
            for c in range(2):
                o_ref[s, c, kk] = u[c * n2:(c + 1) * n2, s * LANES:(s + 1) * LANES]


def _fft_mid(y, yk, m2, hm):
    P, S, _, N1, N2, _ = y.shape
    kb = FFT_MID_K1
    return pl.pallas_call(
        _fft_mid_kernel,
        grid=(N1 // kb, P),
        in_specs=[pl.BlockSpec((2 * N2, 2 * N2), lambda k, p: (0, 0)),
                  pl.BlockSpec((kb, 2 * N2, 2 * N2), lambda k, p: (k, 0, 0)),
                  pl.BlockSpec((None, S, 2, kb, N2, LANES), lambda k, p: (p, 0, 0, k, 0, 0)),
                  pl.BlockSpec((S, 2, kb, N2, LANES), lambda k, p: (0, 0, k, 0, 0))],
        out_specs=pl.BlockSpec((None, S, 2, kb, N2, LANES), lambda k, p: (p, 0, 0, k, 0, 0)),
        out_shape=jax.ShapeDtypeStruct((P, S, 2, N1, N2, LANES), BF16),
        scratch_shapes=[pltpu.VMEM((kb, 2 * N2, S * LANES), F32)],
        compiler_params=_cparams(("parallel", "arbitrary")),
    )(m2, hm, y, yk)


def _fft_inv_kernel(i2_ref, u_ref, z_ref, x1_ref, bias_ref, o_ref, us_ref, zs_ref, xs_ref, os_ref):
    ns, n1 = u_ref.shape[0], u_ref.shape[2]
    h = o_ref.shape[2]
    ur, hr = n1 * SUBLANES, h * SUBLANES
    bias = bias_ref[...]
    for s in range(ns):
        for c in range(2):
            _flatten_in(u_ref[s, c].astype(F32), us_ref, s * 2 + c, ur)
            _flatten_in(z_ref[s, c], zs_ref, s * 2 + c, hr)
            _flatten_in(x1_ref[s, c], xs_ref, s * 2 + c, hr)
    for j in range(BF16_ROWS):
        up = jnp.concatenate([_lane_cat([us_ref[_tstep(s * 2 + c, j, n1, ur), :] for s in range(ns)])
                              for c in range(2)], axis=0)
        y = jnp.dot(i2_ref[...], up.astype(BF16), preferred_element_type=F32)
        for s in range(ns):
            lanes = slice(s * LANES, (s + 1) * LANES)
            for q in range(2):
                t = _tstep(s * 2 + q, j, h, hr)
                os_ref[t, :] = xs_ref[t, :] * (y[q * h:(q + 1) * h, lanes] + bias[:, lanes] * zs_ref[t, :])
    for s in range(ns):
        for q in range(2):
            o_ref[s, q] = _unflatten_out(os_ref, s * 2 + q, h, hr)


def _fft_inv(u, z6, x16, bias, i2, dt):
    P, S, _, N1, N2, _ = u.shape
    h = N1 // 2
    ns = dt // LANES
    jb = BF16_ROWS
    pair = pl.BlockSpec((ns, 2, None, h, jb, LANES), lambda j, p, c: (c, 0, p, 0, j, 0))
    return pl.pallas_call(
        _fft_inv_kernel,
        grid=(N2 // jb, P, S // ns),
        in_specs=[pl.BlockSpec((N1, 2 * N1), lambda j, p, c: (0, 0)),
                  pl.BlockSpec((None, ns, 2, N1, jb, LANES), lambda j, p, c: (p, c, 0, 0, j, 0)),
                  pair, pair,
                  pl.BlockSpec((1, dt), lambda j, p, c: (0, c))],
        out_specs=pair,
        out_shape=jax.ShapeDtypeStruct((S, 2, P, h, N2, LANES), F32),
        scratch_shapes=[pltpu.VMEM((ns * 2 * N1 * jb, LANES), F32)]
        + [pltpu.VMEM((ns * 2 * h * jb, LANES), F32)] * 3,
        compiler_params=_cparams(("parallel", "parallel", "parallel")),
    )(i2, u, z6, x16, bias)


def _hyena(p3, D, col0, conv_w, conv_b, w1, b1, w2, b2, w3, b3, w4, freq, bias, tl, ct, dt):
    B, L, _ = p3.shape
    N2 = FFT_N2
    N1 = 2 * L // N2
    h = N1 // 2
    S = D // LANES
    z, x1c = _hy_pre(p3, conv_w, conv_b, D, col0, tl, ct)
    k_circ, s = _hyena_filter(L, D, w1, b1, w2, b2, w3, b3, w4, freq, min(tl, 512))
    a_data, a_filt, m2, hm, i2 = _dft_mats(N1, N2)
    yk = _fft1_filt(k_circ.reshape(S, 1, N1, N2, LANES), 1.0 / s, a_filt, dt)
    z6 = z.reshape(S, 2, B // 2, h, N2, LANES)
    y = _fft1_data(z6, a_data, dt)
    u = _fft_mid(y, yk, m2, hm)
    out = _fft_inv(u, z6, x1c.reshape(S, 2, B // 2, h, N2, LANES), bias.reshape(1, D), i2, dt)
    return out.reshape(S, B * L, LANES)


def _merge_kernel(x_ref, yl_ref, yh_ref, gl_ref, gh_ref, wl_ref, wh_ref, wo_ref, gf_ref, wr_ref, br_ref,
                  h_ref, hn_ref, idx_ref, gate_ref):
    a = jnp.dot(yl_ref[...].astype(BF16), wl_ref[...], preferred_element_type=F32)
    yh = _lane_cat([yh_ref[s] for s in range(yh_ref.shape[0])])
    b = jnp.dot(yh.astype(BF16), wh_ref[...], preferred_element_type=F32)
    mix = gl_ref[...] * a + gh_ref[...] * b
    h = x_ref[...] + jnp.dot(mix.astype(BF16), wo_ref[...], preferred_element_type=F32)
    h_ref[...] = h
    ms = jnp.mean(h * h, axis=-1, keepdims=True)
    hn = h * lax.rsqrt(ms + EPS) * gf_ref[...]
    nc = hn.shape[1] // LANES
    for c in range(nc):
        hn_ref[pl.ds(c, hn.shape[0], stride=nc), :] = hn[:, c * LANES:(c + 1) * LANES]
    ne = br_ref.shape[1]
    h_hi = hn.astype(BF16)
    h_lo = (hn - h_hi.astype(F32)).astype(BF16)
    a2 = jnp.dot(h_hi, wr_ref[...], preferred_element_type=F32)
    b2 = jnp.dot(h_lo, wr_ref[:, :LANES], preferred_element_type=F32)
    logits = (a2[:, :LANES] + a2[:, LANES:] + b2)[:, :ne] + br_ref[...]
    lane = lax.broadcasted_iota(jnp.int32, logits.shape, 1)
    vals = []
    for k in range(TOP_K):
        m = jnp.max(logits, axis=-1, keepdims=True)
        ix = jnp.min(jnp.where(logits == m, lane, ne), axis=-1, keepdims=True)
        vals.append(m)
        idx_ref[:, k:k + 1] = ix
        logits = jnp.where(lane == ix, -jnp.inf, logits)
    ex = [jnp.exp(v - vals[0]) for v in vals]
    tot = ex[0] + ex[1] + ex[2] + ex[3]
    for k in range(TOP_K):
        gate_ref[:, k:k + 1] = ex[k] / tot


def _merge(x2, y_lru, y_hy, p_all, w_lru_br, w_hy_br, w_out, g_ffn, w_router, b_router, tm):
    T, D = x2.shape
    ne = w_router.shape[1]
    gcol = p_all.shape[1] // D - 2
    row = lambda i: (i, 0)
    full = lambda i: (0, 0)
    w_hi = w_router.astype(BF16)
    w_lo = (w_router - w_hi.astype(F32)).astype(BF16)
    lane_pad = lambda w: jnp.pad(w, ((0, 0), (0, LANES - ne)))
    wr2 = jnp.concatenate([lane_pad(w_hi), lane_pad(w_lo)], axis=1)
    return pl.pallas_call(
        _merge_kernel,
        grid=(T // tm,),
        in_specs=[pl.BlockSpec((tm, D), row), pl.BlockSpec((tm, D), row),
                  pl.BlockSpec((D // LANES, tm, LANES), lambda i: (0, i, 0)),
                  pl.BlockSpec((tm, D), lambda i: (i, gcol)), pl.BlockSpec((tm, D), lambda i: (i, gcol + 1)),
                  pl.BlockSpec((D, D), full), pl.BlockSpec((D, D), full), pl.BlockSpec((D, D), full),
                  pl.BlockSpec((1, D), full), pl.BlockSpec((D, 2 * LANES), full), pl.BlockSpec((1, ne), full)],
        out_specs=[pl.BlockSpec((tm, D), row), pl.BlockSpec((tm * (D // LANES), LANES), row),
                   pl.BlockSpec((tm, TOP_K), row), pl.BlockSpec((tm, TOP_K), row)],
        out_shape=[jax.ShapeDtypeStruct((T, D), F32), jax.ShapeDtypeStruct((T * (D // LANES), LANES), F32),
                   jax.ShapeDtypeStruct((T, TOP_K), jnp.int32), jax.ShapeDtypeStruct((T, TOP_K), F32)],
        compiler_params=_cparams(("parallel",)),
    )(x2, y_lru, y_hy, p_all, p_all, w_lru_br.astype(BF16), w_hy_br.astype(BF16), w_out.astype(BF16),
      g_ffn.reshape(1, D), wr2, b_router.reshape(1, ne))


ROW_DMA_UNROLL = 8


def _for_rows(n, fn):
    def body(g, c):
        for k in range(ROW_DMA_UNROLL):
            fn(g * ROW_DMA_UNROLL + k, k % 2)
        return c

    lax.fori_loop(0, n // ROW_DMA_UNROLL, body, 0)


def _moe_kernel(ib_ref, ie_ref, bnd_ref, ni_ref, ord_ref, nord_ref, hn_hbm, gate_ref, wgu_ref, bgu_ref,
                wd_ref, bd_ref, yk_hbm, xbuf, ybuf, xs_ref, wgu_bf, wd_bf, gsem, ssem, *, nblk):
    i = pl.program_id(0)
    ni = ni_ref[0]
    nrow = xs_ref.shape[0]
    nc = xbuf.shape[1] // nrow
    dff = wd_ref.shape[0]

    def gather_copy(idx_ref, r, slot):
        tok = lax.shift_right_logical(idx_ref[0, 0, r], 2)
        return pltpu.make_async_copy(hn_hbm.at[tok], xbuf.at[slot, pl.ds(r * nc, nc), :], gsem.at[slot])

    def scatter_copy(r, slot):
        return pltpu.make_async_copy(ybuf.at[slot, pl.ds(r * nc, nc), :], yk_hbm.at[ord_ref[0, 0, r]],
                                     ssem.at[slot])

    @pl.when(i < ni)
    def _():
        b = ib_ref[i]
        e = ie_ref[i]
        slot = b % 2
        first = jnp.logical_or(i == 0, ib_ref[jnp.maximum(i - 1, 0)] != b)
        last = jnp.logical_or(i == ni - 1, ib_ref[jnp.minimum(i + 1, pl.num_programs(0) - 1)] != b)

        @pl.when(i == 0)
        def _():
            _for_rows(nrow, lambda r, p: gather_copy(ord_ref, r, 0).start(priority=p))

        @pl.when(first)
        def _():
            @pl.when(b + 1 < nblk)
            def _():
                _for_rows(nrow, lambda r, p: gather_copy(nord_ref, r, 1 - slot).start(priority=p))

            _for_rows(nrow, lambda r, p: gather_copy(ord_ref, r, slot).wait())

            @pl.when(b >= 2)
            def _():
                _for_rows(nrow, lambda r, p: scatter_copy(r, slot).wait())

        @pl.when(jnp.logical_or(i == 0, ie_ref[jnp.maximum(i - 1, 0)] != e))
        def _():
            wgu_bf[...] = wgu_ref[...].astype(BF16)
            wd_bf[...] = wd_ref[...].astype(BF16)

        for c in range(nc):
            xs_ref[:, c * LANES:(c + 1) * LANES] = xbuf[slot, pl.ds(c, nrow, stride=nc), :].astype(BF16)
        xs = xs_ref[...]
        hd = dff // 2
        y = bd_ref[...]
        for hf in range(2):
            gc = slice(hf * hd, (hf + 1) * hd)
            uc = slice(dff + hf * hd, dff + (hf + 1) * hd)
            g = jnp.dot(xs, wgu_bf[:, gc], preferred_element_type=F32) + bgu_ref[:, gc]
            u = jnp.dot(xs, wgu_bf[:, uc], preferred_element_type=F32) + bgu_ref[:, uc]
            g = jnp.minimum(g, SWIGLU_LIMIT)
            u = jnp.clip(u, -SWIGLU_LIMIT, SWIGLU_LIMIT)
            act = (u + 1.0) * (g * (0.5 * jnp.tanh((0.5 * SWIGLU_ALPHA) * g) + 0.5))
            y = y + jnp.dot(act.astype(BF16), wd_bf[gc, :], preferred_element_type=F32)
        y = y * gate_ref[...]
        row = lax.broadcasted_iota(jnp.int32, (nrow, 1), 0) + b * nrow
        mine = jnp.logical_and(row >= bnd_ref[e], row < bnd_ref[e + 1])

        @pl.when(first)
        def _():
            for c in range(nc):
                ybuf[slot, pl.ds(c, nrow, stride=nc), :] = jnp.where(mine, y[:, c * LANES:(c + 1) * LANES], 0.0)

        @pl.when(jnp.logical_not(first))
        def _():
            for c in range(nc):
                rows = pl.ds(c, nrow, stride=nc)
                ybuf[slot, rows, :] = jnp.where(mine, y[:, c * LANES:(c + 1) * LANES], ybuf[slot, rows, :])

        @pl.when(last)
        def _():
            _for_rows(nrow, lambda r, p: scatter_copy(r, slot).start(priority=p))

        @pl.when(i == ni - 1)
        def _():
            _for_rows(nrow, lambda r, p: scatter_copy(r, slot).wait())

            @pl.when(b >= 1)
            def _():
                _for_rows(nrow, lambda r, p: scatter_copy(r, 1 - slot).wait())


def _moe_items(hn, order, g_sorted, item_blk, item_exp, bounds, n_items, w_gate_up, b_gate_up, w_down, b_down):
    T, nc, _ = hn.shape
    D = nc * LANES
    N = order.shape[0]
    nblk = N // MOE_BLOCK
    n_steps = item_blk.shape[0]
    dff = w_down.shape[1]
    order3 = order.reshape(nblk, 1, MOE_BLOCK)
    grid_spec = pltpu.PrefetchScalarGridSpec(
        num_scalar_prefetch=4,
        grid=(n_steps,),
        in_specs=[
            pl.BlockSpec((1, 1, MOE_BLOCK), lambda i, ib, ie, bd, ni: (ib[i], 0, 0), memory_space=pltpu.SMEM),
            pl.BlockSpec((1, 1, MOE_BLOCK), lambda i, ib, ie, bd, ni: (jnp.minimum(ib[i] + 1, nblk - 1), 0, 0),
                         memory_space=pltpu.SMEM),
            pl.BlockSpec(memory_space=pl.ANY),
            pl.BlockSpec((MOE_BLOCK, 1), lambda i, ib, ie, bd, ni: (ib[i], 0)),
            pl.BlockSpec((None, D, 2 * dff), lambda i, ib, ie, bd, ni: (ie[i], 0, 0)),
            pl.BlockSpec((None, 1, 2 * dff), lambda i, ib, ie, bd, ni: (ie[i], 0, 0)),
            pl.BlockSpec((None, dff, D), lambda i, ib, ie, bd, ni: (ie[i], 0, 0)),
            pl.BlockSpec((None, 1, D), lambda i, ib, ie, bd, ni: (ie[i], 0, 0)),
        ],
        out_specs=pl.BlockSpec(memory_space=pl.ANY),
        scratch_shapes=[pltpu.VMEM((2, MOE_BLOCK * nc, LANES), F32), pltpu.VMEM((2, MOE_BLOCK * nc, LANES), F32),
                        pltpu.VMEM((MOE_BLOCK, D), BF16),
                        pltpu.VMEM((D, 2 * dff), BF16), pltpu.VMEM((dff, D), BF16),
                        pltpu.SemaphoreType.DMA((2,)), pltpu.SemaphoreType.DMA((2,))],
    )
    return pl.pallas_call(
        functools.partial(_moe_kernel, nblk=nblk),
        grid_spec=grid_spec,
        out_shape=jax.ShapeDtypeStruct((N, nc, LANES), F32),
        compiler_params=_cparams(("arbitrary",)),
    )(item_blk, item_exp, bounds, n_items, order3, order3, hn, g_sorted.reshape(N, 1),
      w_gate_up, b_gate_up.reshape(N_EXPERTS, 1, 2 * dff), w_down, b_down.reshape(N_EXPERTS, 1, D))


def _combine_kernel(yk_ref, h_ref, g_ref, o_ref, ys_ref, hs_ref):
    tq = h_ref.shape[0]
    nc = yk_ref.shape[1] // TOP_K
    y = yk_ref[:, 0:nc, :]
    for k in range(1, TOP_K):
        y = y + yk_ref[:, k * nc:(k + 1) * nc, :]
    ys_ref[...] = y.reshape(tq * nc, LANES)
    for c in range(nc):
        hs_ref[:, c * LANES:(c + 1) * LANES] = (h_ref[:, c * LANES:(c + 1) * LANES]
                                                + ys_ref[pl.ds(c, tq, stride=nc), :])
    h = hs_ref[...]
    ms = jnp.mean(h * h, axis=-1, keepdims=True)
    o_ref[...] = h * lax.rsqrt(ms + EPS) * g_ref[...]


def _combine(yk, h, g_final, tq):
    T, D = h.shape
    nc = yk.shape[1]
    return pl.pallas_call(
        _combine_kernel,
        grid=(T // tq,),
        in_specs=[pl.BlockSpec((tq, TOP_K * nc, LANES), lambda i: (i, 0, 0)),
                  pl.BlockSpec((tq, D), lambda i: (i, 0)),
                  pl.BlockSpec((1, D), lambda i: (0, 0))],
        out_specs=pl.BlockSpec((tq, D), lambda i: (i, 0)),
        out_shape=jax.ShapeDtypeStruct((T, D), F32),
        scratch_shapes=[pltpu.VMEM((tq * nc, LANES), F32), pltpu.VMEM((tq, D), F32)],
        compiler_params=_cparams(("parallel",)),
    )(yk.reshape(T, TOP_K * nc, LANES), h, g_final.reshape(1, D))


def _route(top_idx, gates):
    T = top_idx.shape[0]
    N = T * TOP_K
    nblk = N // MOE_BLOCK
    i32 = jnp.int32
    e_sorted, order, g_sorted = lax.sort(
        (top_idx.reshape(N), jnp.arange(N, dtype=i32), gates.reshape(N)), num_keys=1)
    bounds = jnp.sum(e_sorted[None, :] < jnp.arange(N_EXPERTS + 1, dtype=i32)[:, None], axis=1, dtype=i32)
    starts, ends = bounds[:-1], bounds[1:]
    first_blk = starts // MOE_BLOCK
    n_of = jnp.where(ends > starts, (ends - 1) // MOE_BLOCK - first_blk + 1, 0)
    item_end = jnp.cumsum(n_of)
    item_start = item_end - n_of
    n_items = item_end[-1:]
    n_steps = nblk + N_EXPERTS - 1
    i = jnp.arange(n_steps, dtype=i32)
    ie = jnp.minimum(jnp.sum(item_end[None, :] <= i[:, None], axis=1, dtype=i32), N_EXPERTS - 1)
    ib = jnp.clip(first_blk[ie] + i - item_start[ie], 0, nblk - 1).astype(i32)
    return order, g_sorted, ib, ie, bounds, n_items.astype(i32)


def kernel(x, g_mix, w_in, b_in, lru_conv_w, lru_conv_b, lru_wa, lru_ba, lru_wi, lru_bi, lru_lambda, w_lru_br, hy_conv_w, hy_conv_b, hy_w1, hy_b1, hy_w2, hy_b2, hy_w3, hy_b3, hy_w4, hy_sin_freq, hy_bias, w_hy_br, w_out, g_ffn, w_router, b_router, w_gate_up, b_gate_up, w_down, b_down, g_final):
    B, L, D = x.shape
    T = B * L
    x2 = x.reshape(T, D)
    tm = min(2048, T)
    p_all = _inproj(x2, g_mix, w_in, b_in, tm)
    p3 = p_all.reshape(B, L, p_all.shape[1])
    y_lru = _lru(p3, lru_conv_w, lru_conv_b, lru_wa, lru_ba, lru_wi, lru_bi, lru_lambda, min(512, L))
    y_hy = _hyena(p3, D, 2 * D, hy_conv_w, hy_conv_b, hy_w1, hy_b1, hy_w2, hy_b2, hy_w3, hy_b3, hy_w4,
                  hy_sin_freq, hy_bias, min(1024, L), min(512, D), min(256, D))
    h, hn, top_idx, gates = _merge(x2, y_lru.reshape(T, D), y_hy, p_all, w_lru_br, w_hy_br, w_out,
                                   g_ffn, w_router, b_router, min(512, T))
    order, g_sorted, item_blk, item_exp, bounds, n_items = _route(top_idx, gates)
    yk = _moe_items(hn.reshape(T, D // LANES, LANES), order, g_sorted, item_blk, item_exp, bounds, n_items,
                    w_gate_up, b_gate_up, w_down, b_down)
    out = _combine(yk, h, g_final, min(512, T))
    return out.reshape(B, L, D)
```
